```python
import math
import jax, jax.numpy as jnp
from jax import lax
import numpy as np

D_MODEL = 2048
BATCH = 8
SEQ = 2048
DEPTH = 4
DEC_BATCH = 8
DEC_SEQ = 64
PAST_LEN = 4096

CHUNK = 64
ROPE_THETA = 500000.0
RMS_EPS = 1e-6
QBLK = 128
N_DIFF_HEADS = 8
DIFF_QK_DIM = 64
DIFF_V_DIM = 2 * DIFF_QK_DIM
DIFF_WIDTH = N_DIFF_HEADS * DIFF_V_DIM
N_DSA_HEADS = 8
DSA_HEAD_DIM = 128
DSA_WIDTH = N_DSA_HEADS * DSA_HEAD_DIM
N_IDX_HEADS = 16
IDX_DIM = 64
TOPK_MAX = 256
N_BRANCH = 2
IN_SIZES = (
    N_DIFF_HEADS * 2 * DIFF_QK_DIM,
    N_DIFF_HEADS * 2 * DIFF_QK_DIM,
    DIFF_WIDTH,
    DSA_WIDTH,
    DSA_WIDTH,
    DSA_WIDTH,
    N_IDX_HEADS * IDX_DIM,
    IDX_DIM,
    N_IDX_HEADS,
    N_BRANCH * D_MODEL,
)
D_IN = sum(IN_SIZES)
IN_SPLITS = tuple(sum(IN_SIZES[: i + 1]) for i in range(len(IN_SIZES) - 1))
N_GROUPS = 4
EXPERTS_PER_GROUP = 8
EXPERT_FF = 256
TOP_K_EXPERTS = 2

kernel_name = "hybrid_diffattn_dsa_hmoe_stream_step"


def _rmsnorm(x, g):
    xf = x.astype(jnp.float32)
    y = xf * lax.rsqrt(jnp.mean(xf * xf, axis=-1, keepdims=True) + RMS_EPS)
    return (y * g.astype(jnp.float32)).astype(x.dtype)


def _rope(x, pos):
    dh = x.shape[-1]
    rot = dh // 4
    half = rot // 2
    inv = ROPE_THETA ** (-jnp.arange(half, dtype=jnp.float32) * (2.0 / rot))
    ang = pos.astype(jnp.float32)[:, None] * inv[None, :]
    cos = jnp.cos(ang)[:, None, :]
    sin = jnp.sin(ang)[:, None, :]
    xf = x.astype(jnp.float32)
    x1, x2, rest = xf[..., :half], xf[..., half:rot], xf[..., rot:]
    out = jnp.concatenate([x1 * cos - x2 * sin, x2 * cos + x1 * sin, rest], axis=-1)
    return out.astype(x.dtype)


def _block_attention(qd, qs, qi, wi, qp, kd, vd, ks, vs, ki, kp, lam, lam_init, subln_g, topk):
    f32 = jnp.float32
    nq = qd.shape[0]
    q_chunk = qp // CHUNK
    admissible = (kp[None, :] // CHUNK) <= q_chunk[:, None]
    s = jnp.einsum('qhmd,khmd->hmqk', qd.astype(f32), kd.astype(f32)) * (DIFF_QK_DIM ** -0.5)
    p = jax.nn.softmax(jnp.where(admissible[None, None], s, -jnp.inf), axis=-1)
    a = p[:, 0] - lam * p[:, 1]
    od = jnp.einsum('hqk,khd->qhd', a, vd.astype(f32))
    od = od * lax.rsqrt(jnp.mean(od * od, axis=-1, keepdims=True) + RMS_EPS)
    od = od * subln_g.astype(f32) * (1.0 - lam_init)
    iq = jax.nn.relu(jnp.einsum('qhd,kd->qhk', qi.astype(f32), ki.astype(f32)))
    w = wi.astype(f32) * (N_IDX_HEADS ** -0.5) * (IDX_DIM ** -0.5)
    score = jnp.einsum('qh,qhk->qk', w, iq)
    score = jnp.where(admissible, score, -jnp.inf)
    _, sel = lax.top_k(score, topk)
    valid = (kp[sel] // CHUNK) <= q_chunk[:, None]
    k_sel = ks[sel]
    v_sel = vs[sel]
    ls = jnp.einsum('qhd,qkhd->hqk', qs.astype(f32), k_sel.astype(f32)) * (DSA_HEAD_DIM ** -0.5)
    ps = jax.nn.softmax(jnp.where(valid[None], ls, -jnp.inf), axis=-1)
    os_ = jnp.einsum('hqk,qkhd->qhd', ps, v_sel.astype(f32))
    return od.reshape(nq, DIFF_WIDTH), os_.reshape(nq, DSA_WIDTH)


def _mix(q_args, k_args, qpos, kpos, lam, lam_init, subln_g):
    B, T = q_args[0].shape[:2]
    L = k_args[0].shape[1]
    topk = min(TOPK_MAX, L // 4)
    qb = min(QBLK, T)
    nb = T // qb

    def body(i):
        b = i // nb
        start = (i % nb) * qb
        qa = [lax.dynamic_slice_in_dim(a[b], start, qb, axis=0) for a in q_args]
        qp = lax.dynamic_slice_in_dim(qpos, start, qb, axis=0)
        ka = [a[b] for a in k_args]
        return _block_attention(qa[0], qa[1], qa[2], qa[3], qp, ka[0], ka[1], ka[2], ka[3], ka[4],
                                kpos, lam, lam_init, subln_g, topk)

    od, os_ = lax.map(body, jnp.arange(B * nb))
    return od.reshape(B, T, DIFF_WIDTH), os_.reshape(B, T, DSA_WIDTH)


def _hier_moe(h, w_rg, b_rg, w_re, b_re, w1, w3, w2):
    f32 = jnp.float32
    n = h.shape[0]
    glog = (h @ w_rg + b_rg).astype(f32)
    pg = jax.nn.softmax(glog, axis=-1)
    g_w = pg * jax.nn.one_hot(jnp.argmax(glog, axis=-1), N_GROUPS, dtype=f32)
    elog = (h @ w_re + b_re).astype(f32).reshape(n, N_GROUPS, EXPERTS_PER_GROUP)
    tv, ti = lax.top_k(elog, TOP_K_EXPERTS)
    pe = jax.nn.softmax(tv, axis=-1)
    within = jnp.sum(jax.nn.one_hot(ti, EXPERTS_PER_GROUP, dtype=f32) * pe[..., None], axis=-2)
    comb = (within * g_w[..., None]).astype(h.dtype)
    y = jnp.zeros_like(h)
    for g in range(N_GROUPS):
        a = jnp.einsum('nd,edf->nef', h, w1[g])
        u = jnp.einsum('nd,edf->nef', h, w3[g])
        act = jax.nn.silu(a) * u * comb[:, g, :, None]
        y = y + jnp.einsum('nef,efd->nd', act, w2[g])
    return y


def _trunk(x, c, qpos, kpos, past, params):
    (norm1, norm2, norm_f, w_ada, b_ada, w_in, lq1, lk1, lq2, lk2, subln_g,
     w_br_diff, w_br_dsa, w_out, w_rg, b_rg, w_re, b_re, w_e1, w_e3, w_e2) = params
    B, T, D = x.shape
    new_rows = ([], [], [], [], [])
    for l in range(DEPTH):
        mod = (jax.nn.silu(c) @ w_ada[l] + b_ada[l])[:, None, :]
        sh1, sc1, g1, sh2, sc2, g2 = jnp.split(mod, 6, axis=-1)
        h = _rmsnorm(x, norm1[l]) * (1.0 + sc1) + sh1
        proj = h @ w_in[l]
        qd, kd, vd, qs, ks, vs, qi, ki, wi, gt = jnp.split(proj, IN_SPLITS, axis=-1)
        qd = _rope(qd.reshape(B, T, N_DIFF_HEADS * 2, DIFF_QK_DIM), qpos).reshape(B, T, N_DIFF_HEADS, 2, DIFF_QK_DIM)
        kd = _rope(kd.reshape(B, T, N_DIFF_HEADS * 2, DIFF_QK_DIM), qpos).reshape(B, T, N_DIFF_HEADS, 2, DIFF_QK_DIM)
        vd = vd.reshape(B, T, N_DIFF_HEADS, DIFF_V_DIM)
        qs = _rope(qs.reshape(B, T, N_DSA_HEADS, DSA_HEAD_DIM), qpos)
        ks = _rope(ks.reshape(B, T, N_DSA_HEADS, DSA_HEAD_DIM), qpos)
        vs = vs.reshape(B, T, N_DSA_HEADS, DSA_HEAD_DIM)
        qi = _rope(qi.reshape(B, T, N_IDX_HEADS, IDX_DIM), qpos)
        ki = _rope(ki[:, :, None, :], qpos)[:, :, 0, :]
        rows = (kd, vd, ks, vs, ki)
        for lst, r in zip(new_rows, rows):
            lst.append(r)
        if past is None:
            keys = rows
        else:
            keys = tuple(jnp.concatenate([pc[l], r], axis=1) for pc, r in zip(past, rows))
        lam_init = 0.8 - 0.6 * math.exp(-0.3 * l)
        lam = (jnp.exp(jnp.sum(lq1[l].astype(jnp.float32) * lk1[l].astype(jnp.float32)))
               - jnp.exp(jnp.sum(lq2[l].astype(jnp.float32) * lk2[l].astype(jnp.float32))) + lam_init)
        od, os_ = _mix((qd, qs, qi, wi), keys, qpos, kpos, lam, lam_init, subln_g[l])
        branch_a = od.astype(x.dtype) @ w_br_diff[l]
        branch_b = os_.astype(x.dtype) @ w_br_dsa[l]
        ga, gb = jnp.split(gt, 2, axis=-1)
        merged = jax.nn.sigmoid(ga) * branch_a + jax.nn.sigmoid(gb) * branch_b
        x = x + g1 * (merged @ w_out[l])
        h2 = _rmsnorm(x, norm2[l]) * (1.0 + sc2) + sh2
        ff = _hier_moe(h2.reshape(B * T, D), w_rg[l], b_rg[l], w_re[l], b_re[l], w_e1[l], w_e3[l], w_e2[l])
        x = x + g2 * ff.reshape(B, T, D)
    y = _rmsnorm(x, norm_f)
    return y, tuple(jnp.stack(lst, axis=0) for lst in new_rows)


def setup_inputs(seed: int = 0) -> dict:
    key = jax.random.key(seed)
    ks = jax.random.split(key, 40)
    f32 = jnp.float32

    def nrm(k, shape, scale):
        return jax.random.normal(k, shape, f32) * scale

    G, E, F, D = N_GROUPS, EXPERTS_PER_GROUP, EXPERT_FF, D_MODEL
    return {
        "x_prompt": nrm(ks[0], (BATCH, SEQ, D), 1.0),
        "x_sample": nrm(ks[1], (DEC_BATCH, DEC_SEQ, D), 1.0),
        "cache_diff_k": nrm(ks[2], (DEPTH, DEC_BATCH, PAST_LEN, N_DIFF_HEADS, 2, DIFF_QK_DIM), 1.0),
        "cache_diff_v": nrm(ks[3], (DEPTH, DEC_BATCH, PAST_LEN, N_DIFF_HEADS, DIFF_V_DIM), 1.0),
        "cache_dsa_k": nrm(ks[4], (DEPTH, DEC_BATCH, PAST_LEN, N_DSA_HEADS, DSA_HEAD_DIM), 1.0),
        "cache_dsa_v": nrm(ks[5], (DEPTH, DEC_BATCH, PAST_LEN, N_DSA_HEADS, DSA_HEAD_DIM), 1.0),
        "cache_idx_k": nrm(ks[6], (DEPTH, DEC_BATCH, PAST_LEN, IDX_DIM), 1.0),
        "c_prompt": nrm(ks[7], (BATCH, D), 1.0),
        "c_sample": nrm(ks[8], (DEC_BATCH, D), 1.0),
        "norm1": 1.0 + nrm(ks[9], (DEPTH, D), 0.1),
        "norm2": 1.0 + nrm(ks[10], (DEPTH, D), 0.1),
        "norm_f": 1.0 + nrm(ks[11], (D,), 0.1),
        "w_ada": nrm(ks[12], (DEPTH, D, 6 * D), 0.5 * D ** -0.5),
        "b_ada": nrm(ks[13], (DEPTH, 6 * D), 0.02),
        "w_in": nrm(ks[14], (DEPTH, D, D_IN), D ** -0.5),
        "lambda_q1": nrm(ks[15], (DEPTH, DIFF_QK_DIM), 0.1),
        "lambda_k1": nrm(ks[16], (DEPTH, DIFF_QK_DIM), 0.1),
        "lambda_q2": nrm(ks[17], (DEPTH, DIFF_QK_DIM), 0.1),
        "lambda_k2": nrm(ks[18], (DEPTH, DIFF_QK_DIM), 0.1),
        "subln_g": 1.0 + nrm(ks[19], (DEPTH, DIFF_V_DIM), 0.1),
        "w_br_diff": nrm(ks[20], (DEPTH, DIFF_WIDTH, D), DIFF_WIDTH ** -0.5),
        "w_br_dsa": nrm(ks[21], (DEPTH, DSA_WIDTH, D), DSA_WIDTH ** -0.5),
        "w_out": nrm(ks[22], (DEPTH, D, D), D ** -0.5),
        "w_router_group": nrm(ks[23], (DEPTH, D, G), D ** -0.5),
        "b_router_group": nrm(ks[24], (DEPTH, G), 0.01),
        "w_router_expert": nrm(ks[25], (DEPTH, D, G * E), D ** -0.5),
        "b_router_expert": nrm(ks[26], (DEPTH, G * E), 0.01),
        "w_expert_gate": nrm(ks[27], (DEPTH, G, E, D, F), D ** -0.5),
        "w_expert_up": nrm(ks[28], (DEPTH, G, E, D, F), D ** -0.5),
        "w_expert_down": nrm(ks[29], (DEPTH, G, E, F, D), F ** -0.5),
    }


def reference(x_prompt, x_sample, cache_diff_k, cache_diff_v, cache_dsa_k, cache_dsa_v, cache_idx_k,
              c_prompt, c_sample, norm1, norm2, norm_f, w_ada, b_ada, w_in,
              lambda_q1, lambda_k1, lambda_q2, lambda_k2, subln_g, w_br_diff, w_br_dsa, w_out,
              w_router_group, b_router_group, w_router_expert, b_router_expert,
              w_expert_gate, w_expert_up, w_expert_down):
    params = (norm1, norm2, norm_f, w_ada, b_ada, w_in, lambda_q1, lambda_k1, lambda_q2, lambda_k2,
              subln_g, w_br_diff, w_br_dsa, w_out, w_router_group, b_router_group,
              w_router_expert, b_router_expert, w_expert_gate, w_expert_up, w_expert_down)
    t_p = x_prompt.shape[1]
    pos_p = jnp.arange(t_p, dtype=jnp.int32)
    y_prompt, st_p = _trunk(x_prompt, c_prompt, pos_p, pos_p, None, params)
    t_s = x_sample.shape[1]
    past_len = cache_diff_k.shape[2]
    pos_s = past_len + jnp.arange(t_s, dtype=jnp.int32)
    kpos_s = jnp.arange(past_len + t_s, dtype=jnp.int32)
    past = (cache_diff_k, cache_diff_v, cache_dsa_k, cache_dsa_v, cache_idx_k)
    y_sample, st_s = _trunk(x_sample, c_sample, pos_s, kpos_s, past, params)
    dk_p, dv_p, sk_p, sv_p, ik_p = st_p
    dk_s, dv_s, sk_s, sv_s, ik_s = st_s
    return (y_prompt, y_sample, dk_p, dv_p, sk_p, sv_p, ik_p, dk_s, dv_s, sk_s, sv_s, ik_s)
```

```python
import functools
import math

import numpy as np
import jax
import jax.numpy as jnp
from jax import lax
from jax.experimental import pallas as pl
from jax.experimental.pallas import tpu as pltpu

F32 = jnp.float32
BF16 = jnp.bfloat16
I32 = jnp.int32

CHUNK = 64
ROPE_THETA = 500000.0
RMS_EPS = 1e-6
N_DIFF_HEADS = 8
DIFF_QK_DIM = 64
DIFF_V_DIM = 128
N_DSA_HEADS = 8
DSA_HEAD_DIM = 128
N_IDX_HEADS = 16
IDX_DIM = 64
TOPK_MAX = 256
N_GROUPS = 4
EXPERTS_PER_GROUP = 8
HEAD_W = 1024
LANE = 128
NEG = -1e30
VMEM_LIMIT = 56 * 1024 * 1024

_NEG_INF_BITS = int(np.array(-np.inf, np.float32).view(np.int32))
KEY_NEG_INF = int(np.int32(_NEG_INF_BITS ^ 0x7FFFFFFF))
INT_MIN = -(2 ** 31)


def _cparams(sem):
    return pltpu.CompilerParams(dimension_semantics=sem, vmem_limit_bytes=VMEM_LIMIT)


def _nt_dot(a, b):
    return lax.dot_general(a, b, (((1,), (1,)), ((), ())), preferred_element_type=F32)


def _ada_kernel(c_ref, w_ref, b_ref, o_ref):
    c = c_ref[...]
    a = (c * (1.0 / (1.0 + jnp.exp(-c)))).astype(BF16)
    o_ref[...] = jnp.dot(a, w_ref[...].astype(BF16), preferred_element_type=F32) + b_ref[...]


def _ada(c_all, w_ada, b_ada):
    depth, d, n6 = w_ada.shape
    r = c_all.shape[0]
    tn = 1024
    return pl.pallas_call(
        _ada_kernel,
        grid=(depth, n6 // tn),
        in_specs=[
            pl.BlockSpec((r, d), lambda l, j: (0, 0)),
            pl.BlockSpec((None, d, tn), lambda l, j: (l, 0, j)),
            pl.BlockSpec((None, 1, tn), lambda l, j: (l, 0, j)),
        ],
        out_specs=pl.BlockSpec((None, r, tn), lambda l, j: (l, 0, j)),
        out_shape=jax.ShapeDtypeStruct((depth, r, n6), F32),
        compiler_params=_cparams(("arbitrary", "arbitrary")),
        name="ada",
    )(c_all, w_ada, b_ada.reshape(depth, 1, n6))


def _normmod_kernel(x_ref, g_ref, sc_ref, sh_ref, o_ref):
    x = x_ref[...]
    y = x * lax.rsqrt(jnp.mean(x * x, axis=-1, keepdims=True) + RMS_EPS) * g_ref[...]
    o_ref[...] = (y * (1.0 + sc_ref[...]) + sh_ref[...]).astype(o_ref.dtype)


def _normmod(x, g, mod5, layer, boff, k_scale, k_shift):
    b, t, d = x.shape
    tt = min(t, 512)
    return pl.pallas_call(
        _normmod_kernel,
        grid=(b, t // tt),
        in_specs=[
            pl.BlockSpec((None, tt, d), lambda bi, i: (bi, i, 0)),
            pl.BlockSpec((1, d), lambda bi, i: (0, 0)),
            pl.BlockSpec((None, None, None, 1, d), lambda bi, i: (layer, boff + bi, k_scale, 0, 0)),
            pl.BlockSpec((None, None, None, 1, d), lambda bi, i: (layer, boff + bi, k_shift, 0, 0)),
        ],
        out_specs=pl.BlockSpec((None, tt, d), lambda bi, i: (bi, i, 0)),
        out_shape=jax.ShapeDtypeStruct((b, t, d), BF16),
        compiler_params=_cparams(("arbitrary", "arbitrary")),
        name="normmod",
    )(x, g.reshape(1, d), mod5, mod5)


def _rope_tables(pos, head_dim, width, active):
    rot = head_dim // 4
    half = rot // 2
    inv = ROPE_THETA ** (-jnp.arange(half, dtype=F32) * (2.0 / rot))
    ang = pos.astype(F32)[:, None] * inv[None, :]
    cos, sin = jnp.cos(ang), jnp.sin(ang)
    col = np.arange(width)
    ch = col % head_dim
    first = (ch < half) & (col < active)
    second = (ch >= half) & (ch < rot) & (col < active)
    idx = np.where(ch < half, ch, np.where(ch < rot, ch - half, 0))
    c = jnp.where((first | second)[None, :], cos[:, idx], 1.0)
    s1 = jnp.where(first[None, :], -sin[:, idx], 0.0)
    s2 = jnp.where(second[None, :], sin[:, idx], 0.0)
    return jnp.stack([c, s1, s2]).astype(F32)


def _proj_kernel(*refs, kinds, halves, n_tab, n_out):
    h_ref, w_ref = refs[0], refs[1]
    tab_refs = refs[2:2 + n_tab]
    out_refs = refs[2 + n_tab:2 + n_tab + n_out]
    j = pl.program_id(1)
    acc = jnp.dot(h_ref[...], w_ref[...], preferred_element_type=F32)
    tn = acc.shape[1]

    def emit(kind):
        if kind < 0:
            for o in out_refs:
                o[...] = acc.astype(o.dtype)
            return
        tab = tab_refs[kind]
        half = halves[kind]
        tw = tab.shape[2]
        for g in range(tn // LANE):
            xg = acc[:, g * LANE:(g + 1) * LANE]
            t0 = (g * LANE) % tw
            c = tab[0, :, t0:t0 + LANE]
            s1 = tab[1, :, t0:t0 + LANE]
            s2 = tab[2, :, t0:t0 + LANE]
            og = xg * c + pltpu.roll(xg, LANE - half, 1) * s1 + pltpu.roll(xg, half, 1) * s2
            for o in out_refs:
                o[:, g * LANE:(g + 1) * LANE] = og.astype(o.dtype)

    distinct = sorted(set(kinds))
    if len(distinct) == 1:
        emit(distinct[0])
    else:
        for kind in distinct:
            pred = functools.reduce(jnp.logical_or, [j == jj for jj, kk in enumerate(kinds) if kk == kind])
            pl.when(pred)(functools.partial(emit, kind))


def _proj(h2d, w, tabs, halves, kinds, tn, out_dtypes, name):
    n, d = h2d.shape
    c = w.shape[1]
    tm = min(n, 1024)
    assert c == tn * len(kinds)
    in_specs = [
        pl.BlockSpec((tm, d), lambda i, j: (i, 0)),
        pl.BlockSpec((d, tn), lambda i, j: (0, j)),
    ]
    for tab in tabs:
        nblk = tab.shape[1] // tm
        in_specs.append(pl.BlockSpec((3, tm, tab.shape[2]), lambda i, j, nblk=nblk: (0, i % nblk, 0)))
    return pl.pallas_call(
        functools.partial(_proj_kernel, kinds=tuple(kinds), halves=tuple(halves), n_tab=len(tabs),
                          n_out=len(out_dtypes)),
        grid=(n // tm, c // tn),
        in_specs=in_specs,
        out_specs=[pl.BlockSpec((tm, tn), lambda i, j: (i, j)) for _ in out_dtypes],
        out_shape=[jax.ShapeDtypeStruct((n, c), dt) for dt in out_dtypes],
        compiler_params=_cparams(("arbitrary", "arbitrary")),
        name=name,
    )(h2d, w, *tabs)


def _diff_kernel(*refs, tq, wp, ncp, t_new, lam_init):
    lq1, lk1, lq2, lk2, g_ref, q_ref, kn_ref, vn_ref = refs[:8]
    if ncp:
        pk_ref, pv_ref, o_ref = refs[8:]
    else:
        (o_ref,) = refs[8:]
    i = pl.program_id(2)
    q = q_ref[...] * jnp.asarray(DIFF_QK_DIM ** -0.5, BF16)
    lane = lax.broadcasted_iota(I32, (tq, LANE), 1)
    zero = jnp.zeros_like(q)
    qq = jnp.concatenate([jnp.where(lane < DIFF_QK_DIM, q, zero),
                          jnp.where(lane >= DIFF_QK_DIM, q, zero)], axis=0)

    def step(k, v, carry, mask):
        m, l, acc = carry
        s = _nt_dot(qq, k)
        if mask is not None:
            s = jnp.where(mask, s, NEG)
        m_new = jnp.maximum(m, jnp.max(s, axis=1, keepdims=True))
        alpha = jnp.exp(m - m_new)
        p = jnp.exp(s - m_new)
        l = alpha * l + jnp.sum(p, axis=1, keepdims=True)
        acc = alpha * acc + jnp.dot(p.astype(BF16), v, preferred_element_type=F32)
        return m_new, l, acc

    carry = (jnp.full((2 * tq, 1), NEG, F32), jnp.zeros((2 * tq, 1), F32), jnp.zeros((2 * tq, LANE), F32))

    if ncp:
        def past_body(c, carry):
            r0 = pl.multiple_of(c * wp, wp)
            return step(pk_ref[pl.ds(r0, wp), :].astype(BF16), pv_ref[pl.ds(r0, wp), :].astype(BF16), carry, None)
        carry = lax.fori_loop(0, ncp, past_body, carry)

    wn = min(tq, t_new)

    def new_body(jn, carry):
        r0 = pl.multiple_of(jn * wn, wn)
        return step(kn_ref[pl.ds(r0, wn), :], vn_ref[pl.ds(r0, wn), :], carry, None)
    carry = lax.fori_loop(0, i, new_body, carry)

    row = lax.broadcasted_iota(I32, (2 * tq, wn), 0)
    row = jnp.where(row >= tq, row - tq, row)
    col = lax.broadcasted_iota(I32, (2 * tq, wn), 1)
    diag_mask = (col // CHUNK) <= (row // CHUNK)
    r0 = pl.multiple_of(i * wn, wn)
    m, l, acc = step(kn_ref[pl.ds(r0, wn), :], vn_ref[pl.ds(r0, wn), :], carry, diag_mask)

    lam = (jnp.exp(jnp.sum(lq1[...] * lk1[...], axis=1, keepdims=True))
           - jnp.exp(jnp.sum(lq2[...] * lk2[...], axis=1, keepdims=True)) + lam_init)
    o = acc / l
    od = o[:tq] - lam * o[tq:]
    od = od * lax.rsqrt(jnp.mean(od * od, axis=1, keepdims=True) + RMS_EPS)
    o_ref[...] = (od * g_ref[...] * (1.0 - lam_init)).astype(o_ref.dtype)


def _diff_attention(qx, yb, lam_params, subln, past, layer, b, t, lam_init):
    n = b * t
    tq = min(t, 256)
    nq = t // tq
    nh = N_DIFF_HEADS
    vec = lambda a: a.reshape(1, -1).astype(F32)
    small = pl.BlockSpec((1, DIFF_QK_DIM), lambda bi, h, i: (0, 0))
    in_specs = [small, small, small, small,
                pl.BlockSpec((1, DIFF_V_DIM), lambda bi, h, i: (0, 0)),
                pl.BlockSpec((tq, LANE), lambda bi, h, i: (bi * nq + i, h)),
                pl.BlockSpec((t, LANE), lambda bi, h, i: (bi, h)),
                pl.BlockSpec((t, LANE), lambda bi, h, i: (bi, 2 * nh + h))]
    args = [vec(p) for p in lam_params] + [vec(subln), qx, yb, yb]
    ncp, wp = 0, 0
    if past is not None:
        pk, pv = past
        plen = pk.shape[2]
        wp = 256
        ncp = plen // wp
        in_specs += [pl.BlockSpec((None, None, plen, LANE), lambda bi, h, i: (layer, bi, 0, h)),
                     pl.BlockSpec((None, None, plen, LANE), lambda bi, h, i: (layer, bi, 0, h))]
        args += [pk, pv]
    return pl.pallas_call(
        functools.partial(_diff_kernel, tq=tq, wp=wp, ncp=ncp, t_new=t, lam_init=lam_init),
        grid=(b, nh, nq),
        in_specs=in_specs,
        out_specs=pl.BlockSpec((tq, LANE), lambda bi, h, i: (bi * nq + i, h)),
        out_shape=jax.ShapeDtypeStruct((n, HEAD_W), BF16),
        compiler_params=_cparams(("arbitrary", "arbitrary", "arbitrary")),
        name="diff_attn",
    )(*args)


def _dsa_kernel(*refs, tq, w, ncp, t_new, hg_heads, nhg, topk, n_new_static):
    qs_ref, qi_ref, wq_ref, kn_ref, vn_ref, zk_ref = refs[:6]
    if ncp:
        pk_ref, pv_ref, pki_ref, o_ref, key_sc, bias_sc = refs[6:]
    else:
        o_ref, key_sc, bias_sc = refs[6:]
    i = pl.program_id(1)
    nnew = n_new_static if n_new_static is not None else i + 1
    ntot = ncp + nnew
    nblk = w // LANE
    kf = float(topk)

    def pad_rows(a):
        if a.shape[0] == w:
            return a
        return jnp.concatenate([a, jnp.zeros((w - a.shape[0], a.shape[1]), a.dtype)], axis=0)

    def new_rows(ref, jn, c0, c1):
        if t_new < w:
            return pad_rows(ref[:, c0:c1])
        r0 = pl.multiple_of(jn * w, w)
        return ref[pl.ds(r0, w), c0:c1]

    def indexer():
        lane = lax.broadcasted_iota(I32, (tq, LANE), 1)
        wi = wq_ref[...] * (N_IDX_HEADS ** -0.5 * IDX_DIM ** -0.5)
        qqs, wcats = [], []
        for p in range(N_IDX_HEADS // 2):
            qp = qi_ref[:, p * LANE:(p + 1) * LANE]
            zero = jnp.zeros_like(qp)
            qqs.append(jnp.concatenate([jnp.where(lane < IDX_DIM, qp, zero),
                                        jnp.where(lane >= IDX_DIM, qp, zero)], axis=0))
            wa = jnp.sum(jnp.where(lane == 2 * p, wi, 0.0), axis=1, keepdims=True)
            wb = jnp.sum(jnp.where(lane == 2 * p + 1, wi, 0.0), axis=1, keepdims=True)
            wcats.append(jnp.concatenate([wa, wb], axis=0))

        def scores(kdup):
            acc = jnp.zeros((tq, w), F32)
            for qq, wc in zip(qqs, wcats):
                r = jnp.maximum(_nt_dot(qq, kdup), 0.0) * wc
                acc = acc + r[:tq] + r[tq:]
            return acc

        def to_key(s, visible):
            s = jnp.where(s == 0.0, 0.0, s)
            if visible is not None:
                s = jnp.where(visible, s, -jnp.inf)
            bits = lax.bitcast_convert_type(s, I32)
            return bits ^ (jnp.right_shift(bits, 31) & 0x7FFFFFFF)

        if ncp:
            def past_body(c, _):
                r0 = pl.multiple_of(c * w, w)
                key_sc[c] = to_key(scores(pki_ref[pl.ds(r0, w), :]), None)
                return 0
            lax.fori_loop(0, ncp, past_body, 0)

        def new_body(jn, _):
            s = scores(new_rows(zk_ref, jn, 0, LANE))
            qpos = i * tq + lax.broadcasted_iota(I32, (tq, w), 0)
            kpos = jn * w + lax.broadcasted_iota(I32, (tq, w), 1)
            visible = (kpos < t_new) & ((kpos // CHUNK) <= (qpos // CHUNK))
            key_sc[ncp + jn] = to_key(s, visible)
            return 0
        lax.fori_loop(0, nnew, new_body, 0)

        def count(pred):
            def body(g, acc):
                kk = key_sc[g]
                for bq in range(nblk):
                    acc = acc + jnp.where(pred(kk[:, bq * LANE:(bq + 1) * LANE]), 1.0, 0.0)
                return acc
            acc = lax.fori_loop(0, ntot, body, jnp.zeros((tq, LANE), F32))
            return jnp.sum(acc, axis=1, keepdims=True)

        def pass_body(t, prefix):
            cand = prefix | jnp.left_shift(jnp.int32(1), 31 - t)
            cand_s = cand ^ INT_MIN
            cnt = count(lambda kk: kk >= cand_s)
            return jnp.where(cnt >= kf, cand, prefix)
        prefix = lax.fori_loop(0, 32, pass_body, jnp.zeros((tq, 1), I32))
        kth = prefix ^ INT_MIN

        need = kf - count(lambda kk: kk > kth)
        ra = lax.broadcasted_iota(I32, (LANE, LANE), 0)
        rb = lax.broadcasted_iota(I32, (LANE, LANE), 1)
        tri = jnp.where(ra <= rb, 1.0, 0.0).astype(BF16)

        def bias_body(g, seen):
            kk_all = key_sc[g]
            for bq in range(nblk):
                kk = kk_all[:, bq * LANE:(bq + 1) * LANE]
                eq = kk == kth
                eqf = jnp.where(eq, 1.0, 0.0)
                rank = seen + jnp.dot(eqf.astype(BF16), tri, preferred_element_type=F32)
                sel = ((kk > kth) | (eq & (rank <= need))) & (kk != KEY_NEG_INF)
                bias_sc[g, :, bq * LANE:(bq + 1) * LANE] = jnp.where(sel, 0.0, NEG)
                seen = seen + jnp.sum(eqf, axis=1, keepdims=True)
            return seen
        lax.fori_loop(0, ntot, bias_body, jnp.zeros((tq, 1), F32))

    if nhg == 1:
        indexer()
    else:
        pl.when(pl.program_id(2) == 0)(indexer)

    scale = DSA_HEAD_DIM ** -0.5
    for hh in range(hg_heads):
        c0, c1 = hh * LANE, (hh + 1) * LANE
        q = qs_ref[:, c0:c1]

        def step(k, v, g, carry, q=q):
            m, l, acc = carry
            s = _nt_dot(q, k) * scale + bias_sc[g]
            m_new = jnp.maximum(m, jnp.max(s, axis=1, keepdims=True))
            alpha = jnp.exp(m - m_new)
            p = jnp.exp(s - m_new)
            l = alpha * l + jnp.sum(p, axis=1, keepdims=True)
            acc = alpha * acc + jnp.dot(p.astype(BF16), v, preferred_element_type=F32)
            return m_new, l, acc

        carry = (jnp.full((tq, 1), NEG, F32), jnp.zeros((tq, 1), F32), jnp.zeros((tq, LANE), F32))
        if ncp:
            def past_body(c, carry, step=step, c0=c0, c1=c1):
                r0 = pl.multiple_of(c * w, w)
                return step(pk_ref[pl.ds(r0, w), c0:c1].astype(BF16), pv_ref[pl.ds(r0, w), c0:c1].astype(BF16),
                            c, carry)
            carry = lax.fori_loop(0, ncp, past_body, carry)

        def new_body(jn, carry, step=step, c0=c0, c1=c1):
            return step(new_rows(kn_ref, jn, c0, c1), new_rows(vn_ref, jn, c0, c1), ncp + jn, carry)
        m, l, acc = lax.fori_loop(0, nnew, new_body, carry)
        o_ref[:, c0:c1] = (acc / l).astype(o_ref.dtype)


def _dsa_attention(qx, yb, zf, zb, past, layer, b, t, hg_heads):
    n = b * t
    nh = N_DSA_HEADS
    nhg = nh // hg_heads
    hw = hg_heads * LANE
    plen = 0 if past is None else past[0].shape[2]
    topk = min(TOPK_MAX, (plen + t) // 4)
    if past is None:
        tq = w = min(t, 256)
        ncp, n_new_static, n_chunks = 0, None, t // w
    else:
        tq, w = t, LANE
        assert t <= w and plen % w == 0 and plen % CHUNK == 0
        ncp, n_new_static, n_chunks = plen // w, 1, plen // w + 1
    nq = t // tq
    qs_blk0 = (2 * HEAD_W) // hw
    ks_blk0 = HEAD_W // hw
    vs_blk0 = (3 * HEAD_W) // hw
    in_specs = [
        pl.BlockSpec((tq, hw), lambda bi, i, g: (bi * nq + i, qs_blk0 + g)),
        pl.BlockSpec((tq, HEAD_W), lambda bi, i, g: (bi * nq + i, 1)),
        pl.BlockSpec((tq, LANE), lambda bi, i, g: (bi * nq + i, 1)),
        pl.BlockSpec((t, hw), lambda bi, i, g: (bi, ks_blk0 + g)),
        pl.BlockSpec((t, hw), lambda bi, i, g: (bi, vs_blk0 + g)),
        pl.BlockSpec((t, LANE), lambda bi, i, g: (bi, 0)),
    ]
    args = [qx, qx, zf, yb, yb, zb]
    if past is not None:
        pk, pv, pki = past
        in_specs += [pl.BlockSpec((None, None, plen, hw), lambda bi, i, g: (layer, bi, 0, g)),
                     pl.BlockSpec((None, None, plen, hw), lambda bi, i, g: (layer, bi, 0, g)),
                     pl.BlockSpec((None, None, plen, LANE), lambda bi, i, g: (layer, bi, 0, 0))]
        args += [pk, pv, pki]
    return pl.pallas_call(
        functools.partial(_dsa_kernel, tq=tq, w=w, ncp=ncp, t_new=t, hg_heads=hg_heads, nhg=nhg, topk=topk,
                          n_new_static=n_new_static),
        grid=(b, nq, nhg),
        in_specs=in_specs,
        out_specs=pl.BlockSpec((tq, hw), lambda bi, i, g: (bi * nq + i, g)),
        out_shape=jax.ShapeDtypeStruct((n, HEAD_W), BF16),
        scratch_shapes=[pltpu.VMEM((n_chunks, tq, w), I32), pltpu.VMEM((n_chunks, tq, w), F32)],
        compiler_params=_cparams(("arbitrary", "arbitrary", "arbitrary")),
        name="dsa_attn",
    )(*args)


def _merge_kernel(od_ref, os_ref, wa_ref, wb_ref, ga_ref, gb_ref, o_ref):
    a = jnp.dot(od_ref[...], wa_ref[...], preferred_element_type=F32)
    bq = jnp.dot(os_ref[...], wb_ref[...], preferred_element_type=F32)
    sig = lambda z: 1.0 / (1.0 + jnp.exp(-z.astype(F32)))
    o_ref[...] = (sig(ga_ref[...]) * a + sig(gb_ref[...]) * bq).astype(o_ref.dtype)


def _merge(od, os_, wa, wb, qx, d):
    n = od.shape[0]
    tm = min(n, 1024)
    tn = min(d, 1024)
    gate0 = (3 * HEAD_W) // tn
    nd = d // tn
    return pl.pallas_call(
        _merge_kernel,
        grid=(n // tm, nd),
        in_specs=[
            pl.BlockSpec((tm, HEAD_W), lambda i, j: (i, 0)),
            pl.BlockSpec((tm, HEAD_W), lambda i, j: (i, 0)),
            pl.BlockSpec((HEAD_W, tn), lambda i, j: (0, j)),
            pl.BlockSpec((HEAD_W, tn), lambda i, j: (0, j)),
            pl.BlockSpec((tm, tn), lambda i, j: (i, gate0 + j)),
            pl.BlockSpec((tm, tn), lambda i, j: (i, gate0 + nd + j)),
        ],
        out_specs=pl.BlockSpec((tm, tn), lambda i, j: (i, j)),
        out_shape=jax.ShapeDtypeStruct((n, d), BF16),
        compiler_params=_cparams(("arbitrary", "arbitrary")),
        name="merge",
    )(od, os_, wa, wb, qx, qx)


def _outproj_kernel(m_ref, w_ref, x_ref, g1_ref, n2_ref, sc_ref, sh_ref, wrh_ref, wrl_ref, br_ref,
                    x1_ref, h2_ref, lg_ref):
    y = jnp.dot(m_ref[...], w_ref[...], preferred_element_type=F32)
    x1 = x_ref[...] + g1_ref[...] * y
    x1_ref[...] = x1
    h = x1 * lax.rsqrt(jnp.mean(x1 * x1, axis=-1, keepdims=True) + RMS_EPS) * n2_ref[...]
    h = h * (1.0 + sc_ref[...]) + sh_ref[...]
    hi = h.astype(BF16)
    h2_ref[...] = hi
    lo = (h - hi.astype(F32)).astype(BF16)
    lg = (jnp.dot(hi, wrh_ref[...], preferred_element_type=F32)
          + jnp.dot(lo, wrh_ref[...], preferred_element_type=F32)
          + jnp.dot(hi, wrl_ref[...], preferred_element_type=F32))
    lg_ref[...] = lg + br_ref[...]


def _outproj(merged, w_out, x, mod5, layer, boff, n2, wr_hi, wr_lo, br):
    b, t, d = x.shape
    tm = min(t, 512)
    nt = t // tm
    row = lambda k: pl.BlockSpec((None, None, None, 1, d), lambda bi, i, k=k: (layer, boff + bi, k, 0, 0))
    return pl.pallas_call(
        _outproj_kernel,
        grid=(b, nt),
        in_specs=[
            pl.BlockSpec((tm, d), lambda bi, i: (bi * nt + i, 0)),
            pl.BlockSpec((d, d), lambda bi, i: (0, 0)),
            pl.BlockSpec((None, tm, d), lambda bi, i: (bi, i, 0)),
            row(2),
            pl.BlockSpec((1, d), lambda bi, i: (0, 0)),
            row(4), row(3),
            pl.BlockSpec((d, LANE), lambda bi, i: (0, 0)),
            pl.BlockSpec((d, LANE), lambda bi, i: (0, 0)),
            pl.BlockSpec((1, LANE), lambda bi, i: (0, 0)),
        ],
        out_specs=[
            pl.BlockSpec((None, tm, d), lambda bi, i: (bi, i, 0)),
            pl.BlockSpec((tm, d), lambda bi, i: (bi * nt + i, 0)),
            pl.BlockSpec((tm, LANE), lambda bi, i: (bi * nt + i, 0)),
        ],
        out_shape=[jax.ShapeDtypeStruct((b, t, d), F32),
                   jax.ShapeDtypeStruct((b * t, d), BF16),
                   jax.ShapeDtypeStruct((b * t, LANE), F32)],
        compiler_params=_cparams(("arbitrary", "arbitrary")),
        name="outproj",
    )(merged, w_out, x, mod5, n2.reshape(1, d), mod5, mod5, wr_hi, wr_lo, br)


def _route_kernel(lg_ref, o_ref):
    lt = lg_ref[...].T
    tm = lt.shape[1]
    e = EXPERTS_PER_GROUP
    rid = lax.broadcasted_iota(I32, (e, tm), 0)
    gl = jnp.where(rid < N_GROUPS, lt[0:e], -jnp.inf)
    gmax = jnp.max(gl, axis=0, keepdims=True)
    ge = jnp.exp(gl - gmax)
    pg = ge / jnp.sum(ge, axis=0, keepdims=True)
    gidx = jnp.min(jnp.where(gl == gmax, rid, e), axis=0, keepdims=True)
    gw = jnp.where(rid == gidx, pg, 0.0)
    parts = []
    for g in range(N_GROUPS):
        el = lt[e * (g + 1):e * (g + 2)]
        m1 = jnp.max(el, axis=0, keepdims=True)
        i1 = jnp.min(jnp.where(el == m1, rid, e), axis=0, keepdims=True)
        el2 = jnp.where(rid == i1, -jnp.inf, el)
        m2 = jnp.max(el2, axis=0, keepdims=True)
        i2 = jnp.min(jnp.where(el2 == m2, rid, e), axis=0, keepdims=True)
        e2 = jnp.exp(m2 - m1)
        p1 = 1.0 / (1.0 + e2)
        p2 = e2 / (1.0 + e2)
        within = jnp.where(rid == i1, p1, 0.0) + jnp.where(rid == i2, p2, 0.0)
        parts.append(within * gw[g:g + 1])
    parts.append(jnp.zeros((LANE - N_GROUPS * e, tm), F32))
    o_ref[...] = jnp.concatenate(parts, axis=0).T


def _route(logits):
    n = logits.shape[0]
    tm = min(n, 512)
    return pl.pallas_call(
        _route_kernel,
        grid=(n // tm,),
        in_specs=[pl.BlockSpec((tm, LANE), lambda i: (i, 0))],
        out_specs=pl.BlockSpec((tm, LANE), lambda i: (i, 0)),
        out_shape=jax.ShapeDtypeStruct((n, LANE), F32),
        compiler_params=_cparams(("arbitrary",)),
        name="route",
    )(logits)


def _moe_up_kernel(h_ref, w1_ref, w3_ref, comb_ref, o_ref):
    j = pl.program_id(1)
    h = h_ref[...]
    a = jnp.dot(h, w1_ref[...], preferred_element_type=F32)
    u = jnp.dot(h, w3_ref[...], preferred_element_type=F32)
    comb = comb_ref[...]
    lane = lax.broadcasted_iota(I32, comb.shape, 1)
    cw = jnp.sum(jnp.where(lane == j, comb, 0.0), axis=1, keepdims=True)
    o_ref[...] = (a * (1.0 / (1.0 + jnp.exp(-a))) * u * cw).astype(o_ref.dtype)


def _moe_up(h2, w1, w3, comb):
    n, d = h2.shape
    ne, _, f = w1.shape
    tm = min(n, 1024)
    return pl.pallas_call(
        _moe_up_kernel,
        grid=(n // tm, ne),
        in_specs=[
            pl.BlockSpec((tm, d), lambda i, j: (i, 0)),
            pl.BlockSpec((None, d, f), lambda i, j: (j, 0, 0)),
            pl.BlockSpec((None, d, f), lambda i, j: (j, 0, 0)),
            pl.BlockSpec((tm, LANE), lambda i, j: (i, 0)),
        ],
        out_specs=pl.BlockSpec((tm, f), lambda i, j: (i, j)),
        out_shape=jax.ShapeDtypeStruct((n, ne * f), BF16),
        compiler_params=_cparams(("arbitrary", "arbitrary")),
        name="moe_up",
    )(h2, w1, w3, comb)


def _moe_down_kernel(a_ref, w_ref, x_ref, g2_ref, gn_ref, sc_ref, sh_ref, x2_ref, hn_ref, acc_ref, *, final):
    k = pl.program_id(2)

    @pl.when(k == 0)
    def _():
        acc_ref[...] = jnp.zeros_like(acc_ref)

    acc_ref[...] += jnp.dot(a_ref[...], w_ref[...], preferred_element_type=F32)

    @pl.when(k == pl.num_programs(2) - 1)
    def _():
        x2 = x_ref[...] + g2_ref[...] * acc_ref[...]
        x2_ref[...] = x2
        y = x2 * lax.rsqrt(jnp.mean(x2 * x2, axis=-1, keepdims=True) + RMS_EPS) * gn_ref[...]
        if not final:
            y = y * (1.0 + sc_ref[...]) + sh_ref[...]
        hn_ref[...] = y.astype(hn_ref.dtype)


def _moe_down(act, w2, x1, mod5, layer, boff, g_next, next_layer, final):
    b, t, d = x1.shape
    kk = act.shape[1]
    tm = min(t, 512)
    tk = min(kk, 1024)
    nt = t // tm
    row = lambda l, k: pl.BlockSpec((None, None, None, 1, d), lambda bi, i, kq, l=l, k=k: (l, boff + bi, k, 0, 0))
    return pl.pallas_call(
        functools.partial(_moe_down_kernel, final=final),
        grid=(b, nt, kk // tk),
        in_specs=[
            pl.BlockSpec((tm, tk), lambda bi, i, kq: (bi * nt + i, kq)),
            pl.BlockSpec((tk, d), lambda bi, i, kq: (kq, 0)),
            pl.BlockSpec((None, tm, d), lambda bi, i, kq: (bi, i, 0)),
            row(layer, 5),
            pl.BlockSpec((1, d), lambda bi, i, kq: (0, 0)),
            row(next_layer, 1), row(next_layer, 0),
        ],
        out_specs=[
            pl.BlockSpec((None, tm, d), lambda bi, i, kq: (bi, i, 0)),
            pl.BlockSpec((None, tm, d), lambda bi, i, kq: (bi, i, 0)),
        ],
        out_shape=[jax.ShapeDtypeStruct((b, t, d), F32),
                   jax.ShapeDtypeStruct((b, t, d), F32 if final else BF16)],
        scratch_shapes=[pltpu.VMEM((tm, d), F32)],
        compiler_params=_cparams(("arbitrary", "arbitrary", "arbitrary")),
        name="moe_down",
    )(act, w2, x1, mod5, g_next.reshape(1, d), mod5, mod5)


def _pack_weights(w_in, w_br_diff, w_br_dsa, w_out, w_rg, b_rg, w_re, b_re, w_e1, w_e3, w_e2, d):
    hw = HEAD_W
    o_qd, o_kd, o_vd, o_qs, o_ks, o_vs, o_qi = (k * hw for k in range(7))
    o_ki = 7 * hw
    o_wi = o_ki + IDX_DIM
    o_gt = o_wi + N_IDX_HEADS
    sl = lambda o, n: w_in[:, :, o:o + n]
    wx = jnp.concatenate([sl(o_qd, hw), sl(o_qi, hw), sl(o_qs, hw), sl(o_gt, 2 * d)], axis=-1).astype(BF16)
    wy = jnp.concatenate([sl(o_kd, hw), sl(o_ks, hw), sl(o_vd, hw), sl(o_vs, hw)], axis=-1).astype(BF16)
    depth = w_in.shape[0]
    pad = jnp.zeros((depth, d, 2 * LANE - 2 * IDX_DIM - N_IDX_HEADS), w_in.dtype)
    wz = jnp.concatenate([sl(o_ki, IDX_DIM), sl(o_ki, IDX_DIM), sl(o_wi, N_IDX_HEADS), pad], axis=-1).astype(BF16)
    ne = N_GROUPS * EXPERTS_PER_GROUP
    rpad0 = jnp.zeros((depth, d, EXPERTS_PER_GROUP - N_GROUPS), F32)
    rpad1 = jnp.zeros((depth, d, LANE - EXPERTS_PER_GROUP - ne), F32)
    wr = jnp.concatenate([w_rg, rpad0, w_re, rpad1], axis=-1)
    wr_hi = wr.astype(BF16)
    wr_lo = (wr - wr_hi.astype(F32)).astype(BF16)
    br = jnp.concatenate([b_rg, jnp.zeros((depth, EXPERTS_PER_GROUP - N_GROUPS), F32), b_re,
                          jnp.zeros((depth, LANE - EXPERTS_PER_GROUP - ne), F32)], axis=-1).reshape(depth, 1, LANE)
    f = w_e1.shape[-1]
    return dict(
        wx=wx, wy=wy, wz=wz,
        wa=w_br_diff.astype(BF16), wb=w_br_dsa.astype(BF16), wo=w_out.astype(BF16),
        wr_hi=wr_hi, wr_lo=wr_lo, br=br,
        w1=w_e1.reshape(depth, ne, d, f).astype(BF16), w3=w_e3.reshape(depth, ne, d, f).astype(BF16),
        w2=w_e2.reshape(depth, ne * f, d).astype(BF16),
    )


def _trunk(x, pos, boff, mod5, past, pw, norm1, norm2, norm_f, lam_params, subln_g):
    b, t, d = x.shape
    n = b * t
    depth = norm1.shape[0]
    rows = n if t < 1024 else t

    def tables(head_dim, width, active):
        tab = _rope_tables(pos, head_dim, width, active)
        return jnp.tile(tab, (1, rows // t, 1)) if rows != t else tab

    tab64 = tables(DIFF_QK_DIM, LANE, LANE)
    tab128 = tables(DSA_HEAD_DIM, LANE, LANE)
    tabz = tables(IDX_DIM, 2 * LANE, LANE)
    halves = (IDX_DIM // 8, DSA_HEAD_DIM // 8, IDX_DIM // 8)
    tabs = (tab64, tab128, tabz)
    gate_tiles = (2 * d) // HEAD_W

    h = _normmod(x, norm1[0], mod5, 0, boff, 1, 0)
    new_rows = ([], [], [], [], [])
    y = None
    for l in range(depth):
        h2d = h.reshape(n, d)
        (qx,) = _proj(h2d, pw["wx"][l], tabs, halves, (0, 0, 1) + (-1,) * gate_tiles, HEAD_W, (BF16,), "proj_q")
        yf, yb = _proj(h2d, pw["wy"][l], tabs, halves, (0, 1, -1, -1), HEAD_W, (F32, BF16), "proj_kv")
        zf, zb = _proj(h2d, pw["wz"][l], tabs, halves, (2,), 2 * LANE, (F32, BF16), "proj_idx")
        for lst, r in zip(new_rows, (yf[:, 0:HEAD_W], yf[:, 2 * HEAD_W:3 * HEAD_W], yf[:, HEAD_W:2 * HEAD_W],
                                     yf[:, 3 * HEAD_W:4 * HEAD_W], zf[:, 0:IDX_DIM])):
            lst.append(r)
        lam_init = 0.8 - 0.6 * math.exp(-0.3 * l)
        lp = tuple(p[l] for p in lam_params)
        if past is None:
            od = _diff_attention(qx, yb, lp, subln_g[l], None, l, b, t, lam_init)
            os_ = _dsa_attention(qx, yb, zf, zb, None, l, b, t, N_DSA_HEADS)
        else:
            od = _diff_attention(qx, yb, lp, subln_g[l], (past[0], past[1]), l, b, t, lam_init)
            os_ = _dsa_attention(qx, yb, zf, zb, (past[2], past[3], past[4]), l, b, t, 1)
        merged = _merge(od, os_, pw["wa"][l], pw["wb"][l], qx, d)
        x1, h2, logits = _outproj(merged, pw["wo"][l], x, mod5, l, boff, norm2[l],
                                  pw["wr_hi"][l], pw["wr_lo"][l], pw["br"][l])
        comb = _route(logits)
        act = _moe_up(h2, pw["w1"][l], pw["w3"][l], comb)
        final = l == depth - 1
        g_next = norm_f if final else norm1[l + 1]
        x, h = _moe_down(act, pw["w2"][l], x1, mod5, l, boff, g_next, 0 if final else l + 1, final)
        if final:
            y = h
    return y, new_rows


def kernel(x_prompt, x_sample, cache_diff_k, cache_diff_v, cache_dsa_k, cache_dsa_v, cache_idx_k, c_prompt, c_sample, norm1, norm2, norm_f, w_ada, b_ada, w_in, lambda_q1, lambda_k1, lambda_q2, lambda_k2, subln_g, w_br_diff, w_br_dsa, w_out, w_router_group, b_router_group, w_router_expert, b_router_expert, w_expert_gate, w_expert_up, w_expert_down):
    bp, tp, d = x_prompt.shape
    bs, ts, _ = x_sample.shape
    depth = norm1.shape[0]
    plen = cache_diff_k.shape[2]
    assert plen % CHUNK == 0 and tp % CHUNK == 0

    pw = _pack_weights(w_in, w_br_diff, w_br_dsa, w_out, w_router_group, b_router_group, w_router_expert,
                       b_router_expert, w_expert_gate, w_expert_up, w_expert_down, d)
    mod = _ada(jnp.concatenate([c_prompt, c_sample], axis=0), w_ada, b_ada)
    mod5 = mod.reshape(depth, bp + bs, 6, 1, d)
    lam_params = (lambda_q1, lambda_k1, lambda_q2, lambda_k2)

    pos_p = jnp.arange(tp, dtype=jnp.int32)
    y_p, rows_p = _trunk(x_prompt, pos_p, 0, mod5, None, pw, norm1, norm2, norm_f, lam_params, subln_g)

    pos_s = plen + jnp.arange(ts, dtype=jnp.int32)
    past = (cache_diff_k.reshape(depth, bs, plen, HEAD_W), cache_diff_v.reshape(depth, bs, plen, HEAD_W),
            cache_dsa_k.reshape(depth, bs, plen, HEAD_W), cache_dsa_v.reshape(depth, bs, plen, HEAD_W),
            jnp.concatenate([cache_idx_k, cache_idx_k], axis=-1).astype(BF16))
    y_s, rows_s = _trunk(x_sample, pos_s, bp, mod5, past, pw, norm1, norm2, norm_f, lam_params, subln_g)

    def finish(rows, b, t):
        dk, dv, sk, sv, ik = (jnp.stack(r, axis=0) for r in rows)
        return (dk.reshape(depth, b, t, N_DIFF_HEADS, 2, DIFF_QK_DIM), dv.reshape(depth, b, t, N_DIFF_HEADS, DIFF_V_DIM),
                sk.reshape(depth, b, t, N_DSA_HEADS, DSA_HEAD_DIM), sv.reshape(depth, b, t, N_DSA_HEADS, DSA_HEAD_DIM),
                ik.reshape(depth, b, t, IDX_DIM))

    return (y_p, y_s) + finish(rows_p, bp, tp) + finish(rows_s, bs, ts)
```

```python
import functools
import math

import numpy as np
import jax
import jax.numpy as jnp
from jax import lax
from jax.experimental import pallas as pl
from jax.experimental.pallas import tpu as pltpu

F32 = jnp.float32
BF16 = jnp.bfloat16
I32 = jnp.int32

CHUNK = 64
ROPE_THETA = 500000.0
RMS_EPS = 1e-6
N_DIFF_HEADS = 8
DIFF_QK_DIM = 64
DIFF_V_DIM = 128
N_DSA_HEADS = 8
DSA_HEAD_DIM = 128
N_IDX_HEADS = 16
IDX_DIM = 64
TOPK_MAX = 256
N_GROUPS = 4
EXPERTS_PER_GROUP = 8
HEAD_W = 1024
LANE = 128
SUBLANE = 8
NEG = -1e30
VMEM_LIMIT = 56 * 1024 * 1024
HEADS_PER_LOOP = 4
PAST_CHUNK = 512
COUNT_CHAINS = 4

_NEG_INF_BITS = int(np.array(-np.inf, np.float32).view(np.int32))
KEY_NEG_INF = int(np.int32(_NEG_INF_BITS ^ 0x7FFFFFFF))
INT_MIN = -(2 ** 31)


def _cparams(sem):
    return pltpu.CompilerParams(dimension_semantics=sem, vmem_limit_bytes=VMEM_LIMIT)


def _nt_dot(a, b):
    return lax.dot_general(a, b, (((1,), (1,)), ((), ())), preferred_element_type=F32)


def _ada_kernel(c_ref, w_ref, b_ref, o_ref):
    c = c_ref[...]
    a = (c * (1.0 / (1.0 + jnp.exp(-c)))).astype(BF16)
    o_ref[...] = jnp.dot(a, w_ref[...].astype(BF16), preferred_element_type=F32) + b_ref[...]


def _ada(c_all, w_ada, b_ada):
    depth, d, n6 = w_ada.shape
    r = c_all.shape[0]
    tn = 1024
    return pl.pallas_call(
        _ada_kernel,
        grid=(depth, n6 // tn),
        in_specs=[
            pl.BlockSpec((r, d), lambda l, j: (0, 0)),
            pl.BlockSpec((None, d, tn), lambda l, j: (l, 0, j)),
            pl.BlockSpec((None, 1, tn), lambda l, j: (l, 0, j)),
        ],
        out_specs=pl.BlockSpec((None, r, tn), lambda l, j: (l, 0, j)),
        out_shape=jax.ShapeDtypeStruct((depth, r, n6), F32),
        compiler_params=_cparams(("arbitrary", "arbitrary")),
        name="ada",
    )(c_all, w_ada, b_ada.reshape(depth, 1, n6))


def _normmod_kernel(x_ref, g_ref, sc_ref, sh_ref, o_ref):
    x = x_ref[...]
    y = x * lax.rsqrt(jnp.mean(x * x, axis=-1, keepdims=True) + RMS_EPS) * g_ref[...]
    o_ref[...] = (y * (1.0 + sc_ref[...]) + sh_ref[...]).astype(o_ref.dtype)


def _normmod(x, g, mod5, layer, boff, k_scale, k_shift):
    b, t, d = x.shape
    tt = min(t, 512)
    return pl.pallas_call(
        _normmod_kernel,
        grid=(b, t // tt),
        in_specs=[
            pl.BlockSpec((None, tt, d), lambda bi, i: (bi, i, 0)),
            pl.BlockSpec((1, d), lambda bi, i: (0, 0)),
            pl.BlockSpec((None, None, None, 1, d), lambda bi, i: (layer, boff + bi, k_scale, 0, 0)),
            pl.BlockSpec((None, None, None, 1, d), lambda bi, i: (layer, boff + bi, k_shift, 0, 0)),
        ],
        out_specs=pl.BlockSpec((None, tt, d), lambda bi, i: (bi, i, 0)),
        out_shape=jax.ShapeDtypeStruct((b, t, d), BF16),
        compiler_params=_cparams(("arbitrary", "arbitrary")),
        name="normmod",
    )(x, g.reshape(1, d), mod5, mod5)


def _rope_tables(pos, head_dim, width, active):
    rot = head_dim // 4
    half = rot // 2
    inv = ROPE_THETA ** (-jnp.arange(half, dtype=F32) * (2.0 / rot))
    ang = pos.astype(F32)[:, None] * inv[None, :]
    cos, sin = jnp.cos(ang), jnp.sin(ang)
    col = np.arange(width)
    ch = col % head_dim
    first = (ch < half) & (col < active)
    second = (ch >= half) & (ch < rot) & (col < active)
    idx = np.where(ch < half, ch, np.where(ch < rot, ch - half, 0))
    c = jnp.where((first | second)[None, :], cos[:, idx], 1.0)
    s1 = jnp.where(first[None, :], -sin[:, idx], 0.0)
    s2 = jnp.where(second[None, :], sin[:, idx], 0.0)
    return jnp.stack([c, s1, s2]).astype(F32)


def _proj_kernel(*refs, kinds, halves, n_tab, n_out):
    h_ref, w_ref = refs[0], refs[1]
    tab_refs = refs[2:2 + n_tab]
    out_refs = refs[2 + n_tab:2 + n_tab + n_out]
    j = pl.program_id(1)
    acc = jnp.dot(h_ref[...], w_ref[...], preferred_element_type=F32)
    tn = acc.shape[1]

    def emit(kind):
        if kind < 0:
            for o in out_refs:
                o[...] = acc.astype(o.dtype)
            return
        tab = tab_refs[kind]
        half = halves[kind]
        tw = tab.shape[2]
        for g in range(tn // LANE):
            xg = acc[:, g * LANE:(g + 1) * LANE]
            t0 = (g * LANE) % tw
            c = tab[0, :, t0:t0 + LANE]
            s1 = tab[1, :, t0:t0 + LANE]
            s2 = tab[2, :, t0:t0 + LANE]
            og = xg * c + pltpu.roll(xg, LANE - half, 1) * s1 + pltpu.roll(xg, half, 1) * s2
            for o in out_refs:
                o[:, g * LANE:(g + 1) * LANE] = og.astype(o.dtype)

    distinct = sorted(set(kinds))
    if len(distinct) == 1:
        emit(distinct[0])
    else:
        for kind in distinct:
            pred = functools.reduce(jnp.logical_or, [j == jj for jj, kk in enumerate(kinds) if kk == kind])
            pl.when(pred)(functools.partial(emit, kind))


def _proj(h2d, w, tabs, halves, kinds, tn, out_dtypes, name):
    n, d = h2d.shape
    c = w.shape[1]
    tm = min(n, 1024)
    assert c == tn * len(kinds)
    in_specs = [
        pl.BlockSpec((tm, d), lambda i, j: (i, 0)),
        pl.BlockSpec((d, tn), lambda i, j: (0, j)),
    ]
    for tab in tabs:
        nblk = tab.shape[1] // tm
        in_specs.append(pl.BlockSpec((3, tm, tab.shape[2]), lambda i, j, nblk=nblk: (0, i % nblk, 0)))
    return pl.pallas_call(
        functools.partial(_proj_kernel, kinds=tuple(kinds), halves=tuple(halves), n_tab=len(tabs),
                          n_out=len(out_dtypes)),
        grid=(n // tm, c // tn),
        in_specs=in_specs,
        out_specs=[pl.BlockSpec((tm, tn), lambda i, j: (i, j)) for _ in out_dtypes],
        out_shape=[jax.ShapeDtypeStruct((n, c), dt) for dt in out_dtypes],
        compiler_params=_cparams(("arbitrary", "arbitrary")),
        name=name,
    )(h2d, w, *tabs)


def _pad_rows(a, rows):
    if a.shape[0] == rows:
        return a
    return jnp.concatenate([a, jnp.zeros((rows - a.shape[0], a.shape[1]), a.dtype)], axis=0)


def _softmax_steps(sts, vts, carries):
    m_news = [jnp.maximum(c[0], jnp.max(st, axis=0, keepdims=True)) for st, c in zip(sts, carries)]
    ps = [jnp.exp(st - m_new) for st, m_new in zip(sts, m_news)]
    pvs = [jnp.dot(vt, p.astype(BF16), preferred_element_type=F32) for vt, p in zip(vts, ps)]
    out = []
    for (m, l, acc), m_new, p, pv in zip(carries, m_news, ps, pvs):
        alpha = jnp.exp(m - m_new)
        out.append((m_new, alpha * l + jnp.sum(p, axis=0, keepdims=True), alpha * acc + pv))
    return tuple(out)


def _softmax_init(n):
    return (jnp.full((1, n), NEG, F32), jnp.zeros((1, n), F32), jnp.zeros((LANE, n), F32))


def _attn_geometry(t, plen):
    tq = min(t, 256)
    tqp = max(tq, LANE)
    wp = min(plen, PAST_CHUNK) if plen else 0
    ncp = plen // wp if plen else 0
    assert t % tq == 0 and tq % CHUNK == 0 and (plen == 0 or plen % wp == 0)
    return tq, tqp, t // tq, wp, ncp


def _transposed_chunks(v2d, b, nchunk, w):
    return jnp.swapaxes(v2d.reshape(b, nchunk, w, HEAD_W), 2, 3)


def _diff_kernel(*refs, tq, tqp, wp, ncp, lam_init):
    lq1, lk1, lq2, lk2, g_ref, q_ref, kn_ref, vnt_ref = refs[:8]
    if ncp:
        pk_ref, pvt_ref, o_ref = refs[8:]
    else:
        (o_ref,) = refs[8:]
    i = pl.program_id(1)
    wn = tq
    n2 = 2 * tqp
    lane = lax.broadcasted_iota(I32, (tqp, LANE), 1)
    lam = (jnp.exp(jnp.sum(lq1[...] * lk1[...], axis=1, keepdims=True))
           - jnp.exp(jnp.sum(lq2[...] * lk2[...], axis=1, keepdims=True)) + lam_init)
    krow = lax.broadcasted_iota(I32, (wn, n2), 0)
    qcol = lax.broadcasted_iota(I32, (wn, n2), 1)
    qcol = jnp.where(qcol >= tqp, qcol - tqp, qcol)
    diag_mask = (krow // CHUNK) <= (qcol // CHUNK)

    for h0 in range(0, N_DIFF_HEADS, HEADS_PER_LOOP):
        heads = list(range(h0, h0 + HEADS_PER_LOOP))
        qqs = []
        for h in heads:
            q = _pad_rows(q_ref[:, h * LANE:(h + 1) * LANE], tqp) * jnp.asarray(DIFF_QK_DIM ** -0.5, BF16)
            zero = jnp.zeros_like(q)
            qqs.append(jnp.concatenate([jnp.where(lane < DIFF_QK_DIM, q, zero),
                                        jnp.where(lane >= DIFF_QK_DIM, q, zero)], axis=0))

        def step_all(get_k, get_vt, carries, mask, heads=heads, qqs=qqs):
            sts = [_nt_dot(get_k(h), qq) for h, qq in zip(heads, qqs)]
            if mask is not None:
                sts = [jnp.where(mask, st, NEG) for st in sts]
            return _softmax_steps(sts, [get_vt(h) for h in heads], carries)

        carries = tuple(_softmax_init(n2) for _ in heads)
        if ncp:
            def past_body(c, carries, step_all=step_all):
                r0 = pl.multiple_of(c * wp, wp)
                return step_all(lambda h: pk_ref[pl.ds(r0, wp), h * LANE:(h + 1) * LANE],
                                lambda h: pvt_ref[c, h * LANE:(h + 1) * LANE, :], carries, None)
            carries = lax.fori_loop(0, ncp, past_body, carries)

        def new_step(jn, carries, mask, step_all=step_all):
            r0 = pl.multiple_of(jn * wn, wn)
            return step_all(lambda h: kn_ref[pl.ds(r0, wn), h * LANE:(h + 1) * LANE],
                            lambda h: vnt_ref[jn, h * LANE:(h + 1) * LANE, :], carries, mask)

        carries = lax.fori_loop(0, i, lambda jn, cs, new_step=new_step: new_step(jn, cs, None), carries)
        carries = new_step(i, carries, diag_mask)

        for h, (m, l, acc) in zip(heads, carries):
            o = acc / l
            od = o[:, :tqp] - lam * o[:, tqp:]
            od = od * lax.rsqrt(jnp.mean(od * od, axis=0, keepdims=True) + RMS_EPS)
            od = od * g_ref[...] * (1.0 - lam_init)
            o_ref[:, h * LANE:(h + 1) * LANE] = od.T[:tq].astype(o_ref.dtype)


def _diff_attention(qx, yb, vnt, lam_params, subln, past, layer, b, t, lam_init):
    n = b * t
    plen = 0 if past is None else past[0].shape[2]
    tq, tqp, nq, wp, ncp = _attn_geometry(t, plen)
    vec = lambda a: a.reshape(1, -1).astype(F32)
    small = pl.BlockSpec((1, DIFF_QK_DIM), lambda bi, i: (0, 0))
    in_specs = [small, small, small, small,
                pl.BlockSpec((DIFF_V_DIM, 1), lambda bi, i: (0, 0)),
                pl.BlockSpec((tq, HEAD_W), lambda bi, i: (bi * nq + i, 0)),
                pl.BlockSpec((t, HEAD_W), lambda bi, i: (bi, 0)),
                pl.BlockSpec((None, nq, HEAD_W, tq), lambda bi, i: (bi, 0, 0, 0))]
    args = [vec(p) for p in lam_params] + [subln.reshape(-1, 1).astype(F32), qx, yb, vnt]
    if past is not None:
        pk, pvt = past
        in_specs += [pl.BlockSpec((None, None, plen, HEAD_W), lambda bi, i: (layer, bi, 0, 0)),
                     pl.BlockSpec((None, None, ncp, HEAD_W, wp), lambda bi, i: (layer, bi, 0, 0, 0))]
        args += [pk, pvt]
    return pl.pallas_call(
        functools.partial(_diff_kernel, tq=tq, tqp=tqp, wp=wp, ncp=ncp, lam_init=lam_init),
        grid=(b, nq),
        in_specs=in_specs,
        out_specs=pl.BlockSpec((tq, HEAD_W), lambda bi, i: (bi * nq + i, 0)),
        out_shape=jax.ShapeDtypeStruct((n, HEAD_W), BF16),
        compiler_params=_cparams(("arbitrary", "arbitrary")),
        name="diff_attn",
    )(*args)


def _dsa_kernel(*refs, tq, tqp, wp, ncp, topk):
    qs_ref, qi_ref, wq_ref, kn_ref, vnt_ref, zk_ref = refs[:6]
    if ncp:
        pk_ref, pvt_ref, pki_ref, o_ref, keyn_sc, biasn_sc, keyp_sc, biasp_sc = refs[6:]
    else:
        o_ref, keyn_sc, biasn_sc = refs[6:]
    i = pl.program_id(1)
    wn = tq
    nnew = i + 1
    kf = float(topk)

    lane = lax.broadcasted_iota(I32, (tqp, LANE), 1)
    wit = _pad_rows(wq_ref[...], tqp).T * (N_IDX_HEADS ** -0.5 * IDX_DIM ** -0.5)
    qqs, wcats = [], []
    for p in range(N_IDX_HEADS // 2):
        qp = _pad_rows(qi_ref[:, p * LANE:(p + 1) * LANE], tqp)
        zero = jnp.zeros_like(qp)
        qqs.append(jnp.concatenate([jnp.where(lane < IDX_DIM, qp, zero),
                                    jnp.where(lane >= IDX_DIM, qp, zero)], axis=0))
        wcats.append(jnp.concatenate([wit[2 * p:2 * p + 1], wit[2 * p + 1:2 * p + 2]], axis=1))

    def scores_t(kdup):
        acc = jnp.zeros((kdup.shape[0], tqp), F32)
        for qq, wc in zip(qqs, wcats):
            r = jnp.maximum(_nt_dot(kdup, qq), 0.0) * wc
            acc = acc + r[:, :tqp] + r[:, tqp:]
        return acc

    def to_key(s, visible):
        s = jnp.where(s == 0.0, 0.0, s)
        if visible is not None:
            s = jnp.where(visible, s, -jnp.inf)
        bits = lax.bitcast_convert_type(s, I32)
        return bits ^ (jnp.right_shift(bits, 31) & 0x7FFFFFFF)

    if ncp:
        def past_keys(c, _):
            r0 = pl.multiple_of(c * wp, wp)
            keyp_sc[c] = to_key(scores_t(pki_ref[pl.ds(r0, wp), :]), None)
            return 0
        lax.fori_loop(0, ncp, past_keys, 0)

    def new_keys(jn, _):
        r0 = pl.multiple_of(jn * wn, wn)
        s = scores_t(zk_ref[pl.ds(r0, wn), :])
        kpos = jn * wn + lax.broadcasted_iota(I32, (wn, tqp), 0)
        qpos = i * tq + lax.broadcasted_iota(I32, (wn, tqp), 1)
        keyn_sc[jn] = to_key(s, (kpos // CHUNK) <= (qpos // CHUNK))
        return 0
    lax.fori_loop(0, nnew, new_keys, 0)

    def count(pred):
        fr = COUNT_CHAINS * SUBLANE

        def fold(kk):
            m = jnp.where(pred(kk), 1.0, 0.0)
            return jnp.sum(m.reshape(kk.shape[0] // fr, fr, tqp), axis=0)
        acc = jnp.zeros((fr, tqp), F32)
        if ncp:
            acc = lax.fori_loop(0, ncp, lambda c, a: a + fold(keyp_sc[c]), acc)
        acc = lax.fori_loop(0, nnew, lambda jn, a: a + fold(keyn_sc[jn]), acc)
        return jnp.sum(acc, axis=0, keepdims=True)

    def pass_body(t, kth):
        cand = kth + jnp.left_shift(jnp.int32(1), 31 - t)
        cnt = count(lambda kk: kk >= cand)
        return jnp.where(cnt >= kf, cand, kth)
    kth = lax.fori_loop(0, 32, pass_body, jnp.full((1, tqp), INT_MIN, I32))

    need = kf - count(lambda kk: kk > kth)

    def lower_tri(w):
        ra = lax.broadcasted_iota(I32, (w, w), 0)
        rb = lax.broadcasted_iota(I32, (w, w), 1)
        return jnp.where(rb <= ra, 1.0, 0.0).astype(BF16)

    def bias_chunk(kk, seen, tril):
        eq = kk == kth
        eqf = jnp.where(eq, 1.0, 0.0)
        rank = seen + jnp.dot(tril, eqf.astype(BF16), preferred_element_type=F32)
        sel = ((kk > kth) | (eq & (rank <= need))) & (kk != KEY_NEG_INF)
        return jnp.where(sel, 0.0, NEG), seen + jnp.sum(eqf, axis=0, keepdims=True)

    seen = jnp.zeros((1, tqp), F32)
    if ncp:
        tril_p = lower_tri(wp)

        def past_bias(c, seen):
            bias, seen = bias_chunk(keyp_sc[c], seen, tril_p)
            biasp_sc[c] = bias
            return seen
        seen = lax.fori_loop(0, ncp, past_bias, seen)
    tril_n = lower_tri(wn)

    def new_bias(jn, seen):
        bias, seen = bias_chunk(keyn_sc[jn], seen, tril_n)
        biasn_sc[jn] = bias
        return seen
    lax.fori_loop(0, nnew, new_bias, seen)

    scale = DSA_HEAD_DIM ** -0.5
    for h0 in range(0, N_DSA_HEADS, HEADS_PER_LOOP):
        heads = list(range(h0, h0 + HEADS_PER_LOOP))
        qs = [_pad_rows(qs_ref[:, h * LANE:(h + 1) * LANE], tqp) for h in heads]

        def step_all(get_k, get_vt, bias, carries, heads=heads, qs=qs):
            sts = [_nt_dot(get_k(h), q) * scale + bias for h, q in zip(heads, qs)]
            return _softmax_steps(sts, [get_vt(h) for h in heads], carries)

        carries = tuple(_softmax_init(tqp) for _ in heads)
        if ncp:
            def past_body(c, carries, step_all=step_all):
                r0 = pl.multiple_of(c * wp, wp)
                return step_all(lambda h: pk_ref[pl.ds(r0, wp), h * LANE:(h + 1) * LANE],
                                lambda h: pvt_ref[c, h * LANE:(h + 1) * LANE, :], biasp_sc[c], carries)
            carries = lax.fori_loop(0, ncp, past_body, carries)

        def new_body(jn, carries, step_all=step_all):
            r0 = pl.multiple_of(jn * wn, wn)
            return step_all(lambda h: kn_ref[pl.ds(r0, wn), h * LANE:(h + 1) * LANE],
                            lambda h: vnt_ref[jn, h * LANE:(h + 1) * LANE, :], biasn_sc[jn], carries)
        carries = lax.fori_loop(0, nnew, new_body, carries)

        for h, (m, l, acc) in zip(heads, carries):
            o_ref[:, h * LANE:(h + 1) * LANE] = (acc / l).T[:tq].astype(o_ref.dtype)


def _dsa_attention(qx, yb, vnt, zf, zb, past, layer, b, t):
    n = b * t
    plen = 0 if past is None else past[0].shape[2]
    topk = min(TOPK_MAX, (plen + t) // 4)
    tq, tqp, nq, wp, ncp = _attn_geometry(t, plen)
    in_specs = [
        pl.BlockSpec((tq, HEAD_W), lambda bi, i: (bi * nq + i, 2)),
        pl.BlockSpec((tq, HEAD_W), lambda bi, i: (bi * nq + i, 1)),
        pl.BlockSpec((tq, LANE), lambda bi, i: (bi * nq + i, 1)),
        pl.BlockSpec((t, HEAD_W), lambda bi, i: (bi, 1)),
        pl.BlockSpec((None, nq, HEAD_W, tq), lambda bi, i: (bi, 0, 0, 0)),
        pl.BlockSpec((t, LANE), lambda bi, i: (bi, 0)),
    ]
    args = [qx, qx, zf, yb, vnt, zb]
    scratch = [pltpu.VMEM((nq, tq, tqp), I32), pltpu.VMEM((nq, tq, tqp), F32)]
    if past is not None:
        pk, pvt, pki = past
        in_specs += [pl.BlockSpec((None, None, plen, HEAD_W), lambda bi, i: (layer, bi, 0, 0)),
                     pl.BlockSpec((None, None, ncp, HEAD_W, wp), lambda bi, i: (layer, bi, 0, 0, 0)),
                     pl.BlockSpec((None, None, plen, LANE), lambda bi, i: (layer, bi, 0, 0))]
        args += [pk, pvt, pki]
        scratch += [pltpu.VMEM((ncp, wp, tqp), I32), pltpu.VMEM((ncp, wp, tqp), F32)]
    return pl.pallas_call(
        functools.partial(_dsa_kernel, tq=tq, tqp=tqp, wp=wp, ncp=ncp, topk=topk),
        grid=(b, nq),
        in_specs=in_specs,
        out_specs=pl.BlockSpec((tq, HEAD_W), lambda bi, i: (bi * nq + i, 0)),
        out_shape=jax.ShapeDtypeStruct((n, HEAD_W), BF16),
        scratch_shapes=scratch,
        compiler_params=_cparams(("arbitrary", "arbitrary")),
        name="dsa_attn",
    )(*args)


def _merge_kernel(od_ref, os_ref, wa_ref, wb_ref, ga_ref, gb_ref, o_ref):
    a = jnp.dot(od_ref[...], wa_ref[...], preferred_element_type=F32)
    bq = jnp.dot(os_ref[...], wb_ref[...], preferred_element_type=F32)
    sig = lambda z: 1.0 / (1.0 + jnp.exp(-z.astype(F32)))
    o_ref[...] = (sig(ga_ref[...]) * a + sig(gb_ref[...]) * bq).astype(o_ref.dtype)


def _merge(od, os_, wa, wb, qx, d):
    n = od.shape[0]
    tm = min(n, 1024)
    tn = min(d, 1024)
    gate0 = (3 * HEAD_W) // tn
    nd = d // tn
    return pl.pallas_call(
        _merge_kernel,
        grid=(n // tm, nd),
        in_specs=[
            pl.BlockSpec((tm, HEAD_W), lambda i, j: (i, 0)),
            pl.BlockSpec((tm, HEAD_W), lambda i, j: (i, 0)),
            pl.BlockSpec((HEAD_W, tn), lambda i, j: (0, j)),
            pl.BlockSpec((HEAD_W, tn), lambda i, j: (0, j)),
            pl.BlockSpec((tm, tn), lambda i, j: (i, gate0 + j)),
            pl.BlockSpec((tm, tn), lambda i, j: (i, gate0 + nd + j)),
        ],
        out_specs=pl.BlockSpec((tm, tn), lambda i, j: (i, j)),
        out_shape=jax.ShapeDtypeStruct((n, d), BF16),
        compiler_params=_cparams(("arbitrary", "arbitrary")),
        name="merge",
    )(od, os_, wa, wb, qx, qx)


def _outproj_kernel(m_ref, w_ref, x_ref, g1_ref, n2_ref, sc_ref, sh_ref, wrh_ref, wrl_ref, br_ref,
                    x1_ref, h2_ref, lg_ref):
    y = jnp.dot(m_ref[...], w_ref[...], preferred_element_type=F32)
    x1 = x_ref[...] + g1_ref[...] * y
    x1_ref[...] = x1
    h = x1 * lax.rsqrt(jnp.mean(x1 * x1, axis=-1, keepdims=True) + RMS_EPS) * n2_ref[...]
    h = h * (1.0 + sc_ref[...]) + sh_ref[...]
    hi = h.astype(BF16)
    h2_ref[...] = hi
    lo = (h - hi.astype(F32)).astype(BF16)
    lg = (jnp.dot(hi, wrh_ref[...], preferred_element_type=F32)
          + jnp.dot(lo, wrh_ref[...], preferred_element_type=F32)
          + jnp.dot(hi, wrl_ref[...], preferred_element_type=F32))
    lg_ref[...] = lg + br_ref[...]


def _outproj(merged, w_out, x, mod5, layer, boff, n2, wr_hi, wr_lo, br):
    b, t, d = x.shape
    tm = min(t, 512)
    nt = t // tm
    row = lambda k: pl.BlockSpec((None, None, None, 1, d), lambda bi, i, k=k: (layer, boff + bi, k, 0, 0))
    return pl.pallas_call(
        _outproj_kernel,
        grid=(b, nt),
        in_specs=[
            pl.BlockSpec((tm, d), lambda bi, i: (bi * nt + i, 0)),
            pl.BlockSpec((d, d), lambda bi, i: (0, 0)),
            pl.BlockSpec((None, tm, d), lambda bi, i: (bi, i, 0)),
            row(2),
            pl.BlockSpec((1, d), lambda bi, i: (0, 0)),
            row(4), row(3),
            pl.BlockSpec((d, LANE), lambda bi, i: (0, 0)),
            pl.BlockSpec((d, LANE), lambda bi, i: (0, 0)),
            pl.BlockSpec((1, LANE), lambda bi, i: (0, 0)),
        ],
        out_specs=[
            pl.BlockSpec((None, tm, d), lambda bi, i: (bi, i, 0)),
            pl.BlockSpec((tm, d), lambda bi, i: (bi * nt + i, 0)),
            pl.BlockSpec((tm, LANE), lambda bi, i: (bi * nt + i, 0)),
        ],
        out_shape=[jax.ShapeDtypeStruct((b, t, d), F32),
                   jax.ShapeDtypeStruct((b * t, d), BF16),
                   jax.ShapeDtypeStruct((b * t, LANE), F32)],
        compiler_params=_cparams(("arbitrary", "arbitrary")),
        name="outproj",
    )(merged, w_out, x, mod5, n2.reshape(1, d), mod5, mod5, wr_hi, wr_lo, br)


def _route_kernel(lg_ref, o_ref):
    lt = lg_ref[...].T
    tm = lt.shape[1]
    e = EXPERTS_PER_GROUP
    rid = lax.broadcasted_iota(I32, (e, tm), 0)
    gl = jnp.where(rid < N_GROUPS, lt[0:e], -jnp.inf)
    gmax = jnp.max(gl, axis=0, keepdims=True)
    ge = jnp.exp(gl - gmax)
    pg = ge / jnp.sum(ge, axis=0, keepdims=True)
    gidx = jnp.min(jnp.where(gl == gmax, rid, e), axis=0, keepdims=True)
    gw = jnp.where(rid == gidx, pg, 0.0)
    parts = []
    for g in range(N_GROUPS):
        el = lt[e * (g + 1):e * (g + 2)]
        m1 = jnp.max(el, axis=0, keepdims=True)
        i1 = jnp.min(jnp.where(el == m1, rid, e), axis=0, keepdims=True)
        el2 = jnp.where(rid == i1, -jnp.inf, el)
        m2 = jnp.max(el2, axis=0, keepdims=True)
        i2 = jnp.min(jnp.where(el2 == m2, rid, e), axis=0, keepdims=True)
        e2 = jnp.exp(m2 - m1)
        p1 = 1.0 / (1.0 + e2)
        p2 = e2 / (1.0 + e2)
        within = jnp.where(rid == i1, p1, 0.0) + jnp.where(rid == i2, p2, 0.0)
        parts.append(within * gw[g:g + 1])
    parts.append(jnp.zeros((LANE - N_GROUPS * e, tm), F32))
    o_ref[...] = jnp.concatenate(parts, axis=0).T


def _route(logits):
    n = logits.shape[0]
    tm = min(n, 512)
    return pl.pallas_call(
        _route_kernel,
        grid=(n // tm,),
        in_specs=[pl.BlockSpec((tm, LANE), lambda i: (i, 0))],
        out_specs=pl.BlockSpec((tm, LANE), lambda i: (i, 0)),
        out_shape=jax.ShapeDtypeStruct((n, LANE), F32),
        compiler_params=_cparams(("arbitrary",)),
        name="route",
    )(logits)


def _moe_up_kernel(h_ref, w1_ref, w3_ref, comb_ref, o_ref):
    j = pl.program_id(1)
    h = h_ref[...]
    a = jnp.dot(h, w1_ref[...], preferred_element_type=F32)
    u = jnp.dot(h, w3_ref[...], preferred_element_type=F32)
    comb = comb_ref[...]
    lane = lax.broadcasted_iota(I32, comb.shape, 1)
    cw = jnp.sum(jnp.where(lane == j, comb, 0.0), axis=1, keepdims=True)
    o_ref[...] = (a * (1.0 / (1.0 + jnp.exp(-a))) * u * cw).astype(o_ref.dtype)


def _moe_up(h2, w1, w3, comb):
    n, d = h2.shape
    ne, _, f = w1.shape
    tm = min(n, 1024)
    return pl.pallas_call(
        _moe_up_kernel,
        grid=(n // tm, ne),
        in_specs=[
            pl.BlockSpec((tm, d), lambda i, j: (i, 0)),
            pl.BlockSpec((None, d, f), lambda i, j: (j, 0, 0)),
            pl.BlockSpec((None, d, f), lambda i, j: (j, 0, 0)),
            pl.BlockSpec((tm, LANE), lambda i, j: (i, 0)),
        ],
        out_specs=pl.BlockSpec((tm, f), lambda i, j: (i, j)),
        out_shape=jax.ShapeDtypeStruct((n, ne * f), BF16),
        compiler_params=_cparams(("arbitrary", "arbitrary")),
        name="moe_up",
    )(h2, w1, w3, comb)


def _moe_down_kernel(a_ref, w_ref, x_ref, g2_ref, gn_ref, sc_ref, sh_ref, x2_ref, hn_ref, acc_ref, *, final):
    k = pl.program_id(2)

    @pl.when(k == 0)
    def _():
        acc_ref[...] = jnp.zeros_like(acc_ref)

    acc_ref[...] += jnp.dot(a_ref[...], w_ref[...], preferred_element_type=F32)

    @pl.when(k == pl.num_programs(2) - 1)
    def _():
        x2 = x_ref[...] + g2_ref[...] * acc_ref[...]
        x2_ref[...] = x2
        y = x2 * lax.rsqrt(jnp.mean(x2 * x2, axis=-1, keepdims=True) + RMS_EPS) * gn_ref[...]
        if not final:
            y = y * (1.0 + sc_ref[...]) + sh_ref[...]
        hn_ref[...] = y.astype(hn_ref.dtype)


def _moe_down(act, w2, x1, mod5, layer, boff, g_next, next_layer, final):
    b, t, d = x1.shape
    kk = act.shape[1]
    tm = min(t, 512)
    tk = min(kk, 1024)
    nt = t // tm
    row = lambda l, k: pl.BlockSpec((None, None, None, 1, d), lambda bi, i, kq, l=l, k=k: (l, boff + bi, k, 0, 0))
    return pl.pallas_call(
        functools.partial(_moe_down_kernel, final=final),
        grid=(b, nt, kk // tk),
        in_specs=[
            pl.BlockSpec((tm, tk), lambda bi, i, kq: (bi * nt + i, kq)),
            pl.BlockSpec((tk, d), lambda bi, i, kq: (kq, 0)),
            pl.BlockSpec((None, tm, d), lambda bi, i, kq: (bi, i, 0)),
            row(layer, 5),
            pl.BlockSpec((1, d), lambda bi, i, kq: (0, 0)),
            row(next_layer, 1), row(next_layer, 0),
        ],
        out_specs=[
            pl.BlockSpec((None, tm, d), lambda bi, i, kq: (bi, i, 0)),
            pl.BlockSpec((None, tm, d), lambda bi, i, kq: (bi, i, 0)),
        ],
        out_shape=[jax.ShapeDtypeStruct((b, t, d), F32),
                   jax.ShapeDtypeStruct((b, t, d), F32 if final else BF16)],
        scratch_shapes=[pltpu.VMEM((tm, d), F32)],
        compiler_params=_cparams(("arbitrary", "arbitrary", "arbitrary")),
        name="moe_down",
    )(act, w2, x1, mod5, g_next.reshape(1, d), mod5, mod5)


def _pack_weights(w_in, w_br_diff, w_br_dsa, w_out, w_rg, b_rg, w_re, b_re, w_e1, w_e3, w_e2, d):
    hw = HEAD_W
    o_qd, o_kd, o_vd, o_qs, o_ks, o_vs, o_qi = (k * hw for k in range(7))
    o_ki = 7 * hw
    o_wi = o_ki + IDX_DIM
    o_gt = o_wi + N_IDX_HEADS
    sl = lambda o, n: w_in[:, :, o:o + n]
    wx = jnp.concatenate([sl(o_qd, hw), sl(o_qi, hw), sl(o_qs, hw), sl(o_gt, 2 * d)], axis=-1).astype(BF16)
    wy = jnp.concatenate([sl(o_kd, hw), sl(o_ks, hw), sl(o_vd, hw), sl(o_vs, hw)], axis=-1).astype(BF16)
    depth = w_in.shape[0]
    pad = jnp.zeros((depth, d, 2 * LANE - 2 * IDX_DIM - N_IDX_HEADS), w_in.dtype)
    wz = jnp.concatenate([sl(o_ki, IDX_DIM), sl(o_ki, IDX_DIM), sl(o_wi, N_IDX_HEADS), pad], axis=-1).astype(BF16)
    ne = N_GROUPS * EXPERTS_PER_GROUP
    rpad0 = jnp.zeros((depth, d, EXPERTS_PER_GROUP - N_GROUPS), F32)
    rpad1 = jnp.zeros((depth, d, LANE - EXPERTS_PER_GROUP - ne), F32)
    wr = jnp.concatenate([w_rg, rpad0, w_re, rpad1], axis=-1)
    wr_hi = wr.astype(BF16)
    wr_lo = (wr - wr_hi.astype(F32)).astype(BF16)
    br = jnp.concatenate([b_rg, jnp.zeros((depth, EXPERTS_PER_GROUP - N_GROUPS), F32), b_re,
                          jnp.zeros((depth, LANE - EXPERTS_PER_GROUP - ne), F32)], axis=-1).reshape(depth, 1, LANE)
    f = w_e1.shape[-1]
    return dict(
        wx=wx, wy=wy, wz=wz,
        wa=w_br_diff.astype(BF16), wb=w_br_dsa.astype(BF16), wo=w_out.astype(BF16),
        wr_hi=wr_hi, wr_lo=wr_lo, br=br,
        w1=w_e1.reshape(depth, ne, d, f).astype(BF16), w3=w_e3.reshape(depth, ne, d, f).astype(BF16),
        w2=w_e2.reshape(depth, ne * f, d).astype(BF16),
    )


def _trunk(x, pos, boff, mod5, past, pw, norm1, norm2, norm_f, lam_params, subln_g):
    b, t, d = x.shape
    n = b * t
    depth = norm1.shape[0]
    rows = n if t < 1024 else t

    def tables(head_dim, width, active):
        tab = _rope_tables(pos, head_dim, width, active)
        return jnp.tile(tab, (1, rows // t, 1)) if rows != t else tab

    tab64 = tables(DIFF_QK_DIM, LANE, LANE)
    tab128 = tables(DSA_HEAD_DIM, LANE, LANE)
    tabz = tables(IDX_DIM, 2 * LANE, LANE)
    halves = (IDX_DIM // 8, DSA_HEAD_DIM // 8, IDX_DIM // 8)
    tabs = (tab64, tab128, tabz)
    gate_tiles = (2 * d) // HEAD_W
    tq, _, nq, _, _ = _attn_geometry(t, 0)

    h = _normmod(x, norm1[0], mod5, 0, boff, 1, 0)
    new_rows = ([], [], [], [], [])
    y = None
    for l in range(depth):
        h2d = h.reshape(n, d)
        (qx,) = _proj(h2d, pw["wx"][l], tabs, halves, (0, 0, 1) + (-1,) * gate_tiles, HEAD_W, (BF16,), "proj_q")
        yf, yb = _proj(h2d, pw["wy"][l], tabs, halves, (0, 1, -1, -1), HEAD_W, (F32, BF16), "proj_kv")
        zf, zb = _proj(h2d, pw["wz"][l], tabs, halves, (2,), 2 * LANE, (F32, BF16), "proj_idx")
        for lst, r in zip(new_rows, (yf[:, 0:HEAD_W], yf[:, 2 * HEAD_W:3 * HEAD_W], yf[:, HEAD_W:2 * HEAD_W],
                                     yf[:, 3 * HEAD_W:4 * HEAD_W], zf[:, 0:IDX_DIM])):
            lst.append(r)
        vdt = _transposed_chunks(yb[:, 2 * HEAD_W:3 * HEAD_W], b, nq, tq)
        vst = _transposed_chunks(yb[:, 3 * HEAD_W:4 * HEAD_W], b, nq, tq)
        lam_init = 0.8 - 0.6 * math.exp(-0.3 * l)
        lp = tuple(p[l] for p in lam_params)
        past_d = None if past is None else (past["dk"], past["dvt"])
        past_s = None if past is None else (past["sk"], past["svt"], past["ik"])
        od = _diff_attention(qx, yb, vdt, lp, subln_g[l], past_d, l, b, t, lam_init)
        os_ = _dsa_attention(qx, yb, vst, zf, zb, past_s, l, b, t)
        merged = _merge(od, os_, pw["wa"][l], pw["wb"][l], qx, d)
        x1, h2, logits = _outproj(merged, pw["wo"][l], x, mod5, l, boff, norm2[l],
                                  pw["wr_hi"][l], pw["wr_lo"][l], pw["br"][l])
        comb = _route(logits)
        act = _moe_up(h2, pw["w1"][l], pw["w3"][l], comb)
        final = l == depth - 1
        g_next = norm_f if final else norm1[l + 1]
        x, h = _moe_down(act, pw["w2"][l], x1, mod5, l, boff, g_next, 0 if final else l + 1, final)
        if final:
            y = h
    return y, new_rows


def kernel(x_prompt, x_sample, cache_diff_k, cache_diff_v, cache_dsa_k, cache_dsa_v, cache_idx_k, c_prompt, c_sample, norm1, norm2, norm_f, w_ada, b_ada, w_in, lambda_q1, lambda_k1, lambda_q2, lambda_k2, subln_g, w_br_diff, w_br_dsa, w_out, w_router_group, b_router_group, w_router_expert, b_router_expert, w_expert_gate, w_expert_up, w_expert_down):
    bp, tp, d = x_prompt.shape
    bs, ts, _ = x_sample.shape
    depth = norm1.shape[0]
    plen = cache_diff_k.shape[2]
    assert plen % CHUNK == 0 and tp % CHUNK == 0

    pw = _pack_weights(w_in, w_br_diff, w_br_dsa, w_out, w_router_group, b_router_group, w_router_expert,
                       b_router_expert, w_expert_gate, w_expert_up, w_expert_down, d)
    mod = _ada(jnp.concatenate([c_prompt, c_sample], axis=0), w_ada, b_ada)
    mod5 = mod.reshape(depth, bp + bs, 6, 1, d)
    lam_params = (lambda_q1, lambda_k1, lambda_q2, lambda_k2)

    pos_p = jnp.arange(tp, dtype=jnp.int32)
    y_p, rows_p = _trunk(x_prompt, pos_p, 0, mod5, None, pw, norm1, norm2, norm_f, lam_params, subln_g)

    _, _, _, wp, ncp = _attn_geometry(ts, plen)
    rows2d = lambda c: c.reshape(depth, bs, plen, HEAD_W).astype(BF16)
    chunks_t = lambda c: jnp.swapaxes(c.reshape(depth, bs, ncp, wp, HEAD_W), 3, 4).astype(BF16)
    past = dict(dk=rows2d(cache_diff_k), dvt=chunks_t(cache_diff_v),
                sk=rows2d(cache_dsa_k), svt=chunks_t(cache_dsa_v),
                ik=jnp.concatenate([cache_idx_k, cache_idx_k], axis=-1).astype(BF16))
    pos_s = plen + jnp.arange(ts, dtype=jnp.int32)
    y_s, rows_s = _trunk(x_sample, pos_s, bp, mod5, past, pw, norm1, norm2, norm_f, lam_params, subln_g)

    def finish(rows, b, t):
        dk, dv, sk, sv, ik = (jnp.stack(r, axis=0) for r in rows)
        return (dk.reshape(depth, b, t, N_DIFF_HEADS, 2, DIFF_QK_DIM), dv.reshape(depth, b, t, N_DIFF_HEADS, DIFF_V_DIM),
                sk.reshape(depth, b, t, N_DSA_HEADS, DSA_HEAD_DIM), sv.reshape(depth, b, t, N_DSA_HEADS, DSA_HEAD_DIM),
                ik.reshape(depth, b, t, IDX_DIM))

    return (y_p, y_s) + finish(rows_p, bp, tp) + finish(rows_s, bs, ts)
```

```python
import functools
import math

import numpy as np
import jax
import jax.numpy as jnp
from jax import lax
from jax.experimental import pallas as pl
from jax.experimental.pallas import tpu as pltpu

F32 = jnp.float32
BF16 = jnp.bfloat16
I32 = jnp.int32

CHUNK = 64
ROPE_THETA = 500000.0
RMS_EPS = 1e-6
N_DIFF_HEADS = 8
DIFF_QK_DIM = 64
DIFF_V_DIM = 128
N_DSA_HEADS = 8
DSA_HEAD_DIM = 128
N_IDX_HEADS = 16
IDX_DIM = 64
TOPK_MAX = 256
N_GROUPS = 4
EXPERTS_PER_GROUP = 8
HEAD_W = 1024
LANE = 128
SUBLANE = 8
NEG = -1e30
VMEM_LIMIT = 56 * 1024 * 1024
HEADS_PER_LOOP = 4
PAST_CHUNK = 512
COUNT_CHAINS = 4

_NEG_INF_BITS = int(np.array(-np.inf, np.float32).view(np.int32))
KEY_NEG_INF = int(np.int32(_NEG_INF_BITS ^ 0x7FFFFFFF))
INT_MIN = -(2 ** 31)


def _cparams(sem):
    return pltpu.CompilerParams(dimension_semantics=sem, vmem_limit_bytes=VMEM_LIMIT)


def _nt_dot(a, b):
    return lax.dot_general(a, b, (((1,), (1,)), ((), ())), preferred_element_type=F32)


def _ada_kernel(c_ref, w_ref, b_ref, o_ref):
    c = c_ref[...]
    a = (c * (1.0 / (1.0 + jnp.exp(-c)))).astype(BF16)
    o_ref[...] = jnp.dot(a, w_ref[...].astype(BF16), preferred_element_type=F32) + b_ref[...]


def _ada(c_all, w_ada, b_ada):
    depth, d, n6 = w_ada.shape
    r = c_all.shape[0]
    tn = 1024
    return pl.pallas_call(
        _ada_kernel,
        grid=(depth, n6 // tn),
        in_specs=[
            pl.BlockSpec((r, d), lambda l, j: (0, 0)),
            pl.BlockSpec((None, d, tn), lambda l, j: (l, 0, j)),
            pl.BlockSpec((None, 1, tn), lambda l, j: (l, 0, j)),
        ],
        out_specs=pl.BlockSpec((None, r, tn), lambda l, j: (l, 0, j)),
        out_shape=jax.ShapeDtypeStruct((depth, r, n6), F32),
        compiler_params=_cparams(("arbitrary", "arbitrary")),
        name="ada",
    )(c_all, w_ada, b_ada.reshape(depth, 1, n6))


def _normmod_kernel(x_ref, g_ref, sc_ref, sh_ref, o_ref):
    x = x_ref[...]
    y = x * lax.rsqrt(jnp.mean(x * x, axis=-1, keepdims=True) + RMS_EPS) * g_ref[...]
    o_ref[...] = (y * (1.0 + sc_ref[...]) + sh_ref[...]).astype(o_ref.dtype)


def _normmod(x, g, mod5, layer, boff, k_scale, k_shift):
    b, t, d = x.shape
    tt = min(t, 512)
    return pl.pallas_call(
        _normmod_kernel,
        grid=(b, t // tt),
        in_specs=[
            pl.BlockSpec((None, tt, d), lambda bi, i: (bi, i, 0)),
            pl.BlockSpec((1, d), lambda bi, i: (0, 0)),
            pl.BlockSpec((None, None, None, 1, d), lambda bi, i: (layer, boff + bi, k_scale, 0, 0)),
            pl.BlockSpec((None, None, None, 1, d), lambda bi, i: (layer, boff + bi, k_shift, 0, 0)),
        ],
        out_specs=pl.BlockSpec((None, tt, d), lambda bi, i: (bi, i, 0)),
        out_shape=jax.ShapeDtypeStruct((b, t, d), BF16),
        compiler_params=_cparams(("arbitrary", "arbitrary")),
        name="normmod",
    )(x, g.reshape(1, d), mod5, mod5)


def _rope_tables(pos, head_dim, width, active):
    rot = head_dim // 4
    half = rot // 2
    inv = ROPE_THETA ** (-jnp.arange(half, dtype=F32) * (2.0 / rot))
    ang = pos.astype(F32)[:, None] * inv[None, :]
    cos, sin = jnp.cos(ang), jnp.sin(ang)
    col = np.arange(width)
    ch = col % head_dim
    first = (ch < half) & (col < active)
    second = (ch >= half) & (ch < rot) & (col < active)
    idx = np.where(ch < half, ch, np.where(ch < rot, ch - half, 0))
    c = jnp.where((first | second)[None, :], cos[:, idx], 1.0)
    s1 = jnp.where(first[None, :], -sin[:, idx], 0.0)
    s2 = jnp.where(second[None, :], sin[:, idx], 0.0)
    return jnp.stack([c, s1, s2]).astype(F32)


def _proj_kernel(*refs, kinds, halves, n_tab, n_out):
    h_ref, w_ref = refs[0], refs[1]
    tab_refs = refs[2:2 + n_tab]
    out_refs = refs[2 + n_tab:2 + n_tab + n_out]
    j = pl.program_id(1)
    acc = jnp.dot(h_ref[...], w_ref[...], preferred_element_type=F32)
    tn = acc.shape[1]

    def emit(kind):
        if kind < 0:
            for o in out_refs:
                o[...] = acc.astype(o.dtype)
            return
        tab = tab_refs[kind]
        half = halves[kind]
        tw = tab.shape[2]
        for g in range(tn // LANE):
            xg = acc[:, g * LANE:(g + 1) * LANE]
            t0 = (g * LANE) % tw
            c = tab[0, :, t0:t0 + LANE]
            s1 = tab[1, :, t0:t0 + LANE]
            s2 = tab[2, :, t0:t0 + LANE]
            og = xg * c + pltpu.roll(xg, LANE - half, 1) * s1 + pltpu.roll(xg, half, 1) * s2
            for o in out_refs:
                o[:, g * LANE:(g + 1) * LANE] = og.astype(o.dtype)

    distinct = sorted(set(kinds))
    if len(distinct) == 1:
        emit(distinct[0])
    else:
        for kind in distinct:
            pred = functools.reduce(jnp.logical_or, [j == jj for jj, kk in enumerate(kinds) if kk == kind])
            pl.when(pred)(functools.partial(emit, kind))


def _proj(h2d, w, tabs, halves, kinds, tn, out_dtypes, name):
    n, d = h2d.shape
    c = w.shape[1]
    tm = min(n, 1024)
    assert c == tn * len(kinds)
    in_specs = [
        pl.BlockSpec((tm, d), lambda i, j: (i, 0)),
        pl.BlockSpec((d, tn), lambda i, j: (0, j)),
    ]
    for tab in tabs:
        nblk = tab.shape[1] // tm
        in_specs.append(pl.BlockSpec((3, tm, tab.shape[2]), lambda i, j, nblk=nblk: (0, i % nblk, 0)))
    return pl.pallas_call(
        functools.partial(_proj_kernel, kinds=tuple(kinds), halves=tuple(halves), n_tab=len(tabs),
                          n_out=len(out_dtypes)),
        grid=(n // tm, c // tn),
        in_specs=in_specs,
        out_specs=[pl.BlockSpec((tm, tn), lambda i, j: (i, j)) for _ in out_dtypes],
        out_shape=[jax.ShapeDtypeStruct((n, c), dt) for dt in out_dtypes],
        compiler_params=_cparams(("arbitrary", "arbitrary")),
        name=name,
    )(h2d, w, *tabs)


def _pad_rows(a, rows):
    if a.shape[0] == rows:
        return a
    return jnp.concatenate([a, jnp.zeros((rows - a.shape[0], a.shape[1]), a.dtype)], axis=0)


def _softmax_steps(sts, vts, carries):
    m_news = [jnp.maximum(c[0], jnp.max(st, axis=0, keepdims=True)) for st, c in zip(sts, carries)]
    ps = [jnp.exp(st - m_new) for st, m_new in zip(sts, m_news)]
    pvs = [jnp.dot(vt, p.astype(BF16), preferred_element_type=F32) for vt, p in zip(vts, ps)]
    out = []
    for (m, l, acc), m_new, p, pv in zip(carries, m_news, ps, pvs):
        alpha = jnp.exp(m - m_new)
        out.append((m_new, alpha * l + jnp.sum(p, axis=0, keepdims=True), alpha * acc + pv))
    return tuple(out)


def _softmax_init(n):
    return (jnp.full((1, n), NEG, F32), jnp.zeros((1, n), F32), jnp.zeros((LANE, n), F32))


def _attn_geometry(t, plen):
    tq = min(t, 256)
    tqp = max(tq, LANE)
    wp = min(plen, PAST_CHUNK) if plen else 0
    ncp = plen // wp if plen else 0
    assert t % tq == 0 and tq % CHUNK == 0 and (plen == 0 or plen % wp == 0)
    return tq, tqp, t // tq, wp, ncp


def _transposed_chunks(v2d, b, nchunk, w):
    return jnp.swapaxes(v2d.reshape(b, nchunk, w, HEAD_W), 2, 3)


def _diff_kernel(*refs, tq, tqp, wp, ncp, lam_init):
    lq1, lk1, lq2, lk2, g_ref, q_ref, kn_ref, vnt_ref = refs[:8]
    if ncp:
        pk_ref, pvt_ref, o_ref = refs[8:]
    else:
        (o_ref,) = refs[8:]
    i = pl.program_id(1)
    wn = tq
    n2 = 2 * tqp
    lane = lax.broadcasted_iota(I32, (tqp, LANE), 1)
    lam = (jnp.exp(jnp.sum(lq1[...] * lk1[...], axis=1, keepdims=True))
           - jnp.exp(jnp.sum(lq2[...] * lk2[...], axis=1, keepdims=True)) + lam_init)
    krow = lax.broadcasted_iota(I32, (wn, n2), 0)
    qcol = lax.broadcasted_iota(I32, (wn, n2), 1)
    qcol = jnp.where(qcol >= tqp, qcol - tqp, qcol)
    diag_mask = (krow // CHUNK) <= (qcol // CHUNK)

    for h0 in range(0, N_DIFF_HEADS, HEADS_PER_LOOP):
        heads = list(range(h0, h0 + HEADS_PER_LOOP))
        qqs = []
        for h in heads:
            q = _pad_rows(q_ref[:, h * LANE:(h + 1) * LANE], tqp) * jnp.asarray(DIFF_QK_DIM ** -0.5, BF16)
            zero = jnp.zeros_like(q)
            qqs.append(jnp.concatenate([jnp.where(lane < DIFF_QK_DIM, q, zero),
                                        jnp.where(lane >= DIFF_QK_DIM, q, zero)], axis=0))

        def step_all(get_k, get_vt, carries, mask, heads=heads, qqs=qqs):
            sts = [_nt_dot(get_k(h), qq) for h, qq in zip(heads, qqs)]
            if mask is not None:
                sts = [jnp.where(mask, st, NEG) for st in sts]
            return _softmax_steps(sts, [get_vt(h) for h in heads], carries)

        carries = tuple(_softmax_init(n2) for _ in heads)
        if ncp:
            def past_body(c, carries, step_all=step_all):
                r0 = pl.multiple_of(c * wp, wp)
                return step_all(lambda h: pk_ref[pl.ds(r0, wp), h * LANE:(h + 1) * LANE],
                                lambda h: pvt_ref[c, h * LANE:(h + 1) * LANE, :], carries, None)
            carries = lax.fori_loop(0, ncp, past_body, carries)

        def new_step(jn, carries, mask, step_all=step_all):
            r0 = pl.multiple_of(jn * wn, wn)
            return step_all(lambda h: kn_ref[pl.ds(r0, wn), h * LANE:(h + 1) * LANE],
                            lambda h: vnt_ref[jn, h * LANE:(h + 1) * LANE, :], carries, mask)

        carries = lax.fori_loop(0, i, lambda jn, cs, new_step=new_step: new_step(jn, cs, None), carries)
        carries = new_step(i, carries, diag_mask)

        for h, (m, l, acc) in zip(heads, carries):
            o = acc / l
            od = o[:, :tqp] - lam * o[:, tqp:]
            od = od * lax.rsqrt(jnp.mean(od * od, axis=0, keepdims=True) + RMS_EPS)
            od = od * g_ref[...] * (1.0 - lam_init)
            o_ref[:, h * LANE:(h + 1) * LANE] = od.T[:tq].astype(o_ref.dtype)


def _diff_attention(qx, yb, vnt, lam_params, subln, past, layer, b, t, lam_init):
    n = b * t
    plen = 0 if past is None else past[0].shape[2]
    tq, tqp, nq, wp, ncp = _attn_geometry(t, plen)
    vec = lambda a: a.reshape(1, -1).astype(F32)
    small = pl.BlockSpec((1, DIFF_QK_DIM), lambda bi, i: (0, 0))
    in_specs = [small, small, small, small,
                pl.BlockSpec((DIFF_V_DIM, 1), lambda bi, i: (0, 0)),
                pl.BlockSpec((tq, HEAD_W), lambda bi, i: (bi * nq + i, 0)),
                pl.BlockSpec((t, HEAD_W), lambda bi, i: (bi, 0)),
                pl.BlockSpec((None, nq, HEAD_W, tq), lambda bi, i: (bi, 0, 0, 0))]
    args = [vec(p) for p in lam_params] + [subln.reshape(-1, 1).astype(F32), qx, yb, vnt]
    if past is not None:
        pk, pvt = past
        in_specs += [pl.BlockSpec((None, None, plen, HEAD_W), lambda bi, i: (layer, bi, 0, 0)),
                     pl.BlockSpec((None, None, ncp, HEAD_W, wp), lambda bi, i: (layer, bi, 0, 0, 0))]
        args += [pk, pvt]
    return pl.pallas_call(
        functools.partial(_diff_kernel, tq=tq, tqp=tqp, wp=wp, ncp=ncp, lam_init=lam_init),
        grid=(b, nq),
        in_specs=in_specs,
        out_specs=pl.BlockSpec((tq, HEAD_W), lambda bi, i: (bi * nq + i, 0)),
        out_shape=jax.ShapeDtypeStruct((n, HEAD_W), BF16),
        compiler_params=_cparams(("arbitrary", "arbitrary")),
        name="diff_attn",
    )(*args)


def _dsa_kernel(*refs, tq, tqp, wp, ncp, topk):
    qs_ref, qi_ref, wq_ref, kn_ref, vnt_ref, zk_ref = refs[:6]
    if ncp:
        pk_ref, pvt_ref, pki_ref, o_ref, keyn_sc, biasn_sc, keyp_sc, biasp_sc = refs[6:]
    else:
        o_ref, keyn_sc, biasn_sc = refs[6:]
    i = pl.program_id(1)
    wn = tq
    nnew = i + 1
    kf = float(topk)

    lane = lax.broadcasted_iota(I32, (tqp, LANE), 1)
    wit = _pad_rows(wq_ref[...], tqp).T * (N_IDX_HEADS ** -0.5 * IDX_DIM ** -0.5)
    qqs, wcats = [], []
    for p in range(N_IDX_HEADS // 2):
        qp = _pad_rows(qi_ref[:, p * LANE:(p + 1) * LANE], tqp)
        zero = jnp.zeros_like(qp)
        qqs.append(jnp.concatenate([jnp.where(lane < IDX_DIM, qp, zero),
                                    jnp.where(lane >= IDX_DIM, qp, zero)], axis=0))
        wcats.append(jnp.concatenate([wit[2 * p:2 * p + 1], wit[2 * p + 1:2 * p + 2]], axis=1))

    def scores_t(kdup):
        acc = jnp.zeros((kdup.shape[0], tqp), F32)
        for qq, wc in zip(qqs, wcats):
            r = jnp.maximum(_nt_dot(kdup, qq), 0.0) * wc
            acc = acc + r[:, :tqp] + r[:, tqp:]
        return acc

    def to_key(s, visible):
        s = jnp.where(s == 0.0, 0.0, s)
        if visible is not None:
            s = jnp.where(visible, s, -jnp.inf)
        bits = lax.bitcast_convert_type(s, I32)
        return bits ^ (jnp.right_shift(bits, 31) & 0x7FFFFFFF)

    if ncp:
        def past_keys(c, _):
            r0 = pl.multiple_of(c * wp, wp)
            keyp_sc[c] = to_key(scores_t(pki_ref[pl.ds(r0, wp), :]), None)
            return 0
        lax.fori_loop(0, ncp, past_keys, 0)

    def new_keys(jn, _):
        r0 = pl.multiple_of(jn * wn, wn)
        s = scores_t(zk_ref[pl.ds(r0, wn), :])
        kpos = jn * wn + lax.broadcasted_iota(I32, (wn, tqp), 0)
        qpos = i * tq + lax.broadcasted_iota(I32, (wn, tqp), 1)
        keyn_sc[jn] = to_key(s, (kpos // CHUNK) <= (qpos // CHUNK))
        return 0
    lax.fori_loop(0, nnew, new_keys, 0)

    def count(pred):
        fr = COUNT_CHAINS * SUBLANE

        def fold(kk):
            m = jnp.where(pred(kk), 1.0, 0.0)
            return jnp.sum(m.reshape(kk.shape[0] // fr, fr, tqp), axis=0)
        acc = jnp.zeros((fr, tqp), F32)
        if ncp:
            acc = lax.fori_loop(0, ncp, lambda c, a: a + fold(keyp_sc[c]), acc)
        acc = lax.fori_loop(0, nnew, lambda jn, a: a + fold(keyn_sc[jn]), acc)
        return jnp.sum(acc, axis=0, keepdims=True)

    def pass_body(t, kth):
        cand = kth + jnp.left_shift(jnp.int32(1), 31 - t)
        cnt = count(lambda kk: kk >= cand)
        return jnp.where(cnt >= kf, cand, kth)
    kth = lax.fori_loop(0, 32, pass_body, jnp.full((1, tqp), INT_MIN, I32))

    need = kf - count(lambda kk: kk > kth)

    def lower_tri(w):
        ra = lax.broadcasted_iota(I32, (w, w), 0)
        rb = lax.broadcasted_iota(I32, (w, w), 1)
        return jnp.where(rb <= ra, 1.0, 0.0).astype(BF16)

    def bias_chunk(kk, seen, tril):
        eq = kk == kth
        eqf = jnp.where(eq, 1.0, 0.0)
        rank = seen + jnp.dot(tril, eqf.astype(BF16), preferred_element_type=F32)
        sel = ((kk > kth) | (eq & (rank <= need))) & (kk != KEY_NEG_INF)
        return jnp.where(sel, 0.0, NEG), seen + jnp.sum(eqf, axis=0, keepdims=True)

    seen = jnp.zeros((1, tqp), F32)
    if ncp:
        tril_p = lower_tri(wp)

        def past_bias(c, seen):
            bias, seen = bias_chunk(keyp_sc[c], seen, tril_p)
            biasp_sc[c] = bias
            return seen
        seen = lax.fori_loop(0, ncp, past_bias, seen)
    tril_n = lower_tri(wn)

    def new_bias(jn, seen):
        bias, seen = bias_chunk(keyn_sc[jn], seen, tril_n)
        biasn_sc[jn] = bias
        return seen
    lax.fori_loop(0, nnew, new_bias, seen)

    scale = DSA_HEAD_DIM ** -0.5
    for h0 in range(0, N_DSA_HEADS, HEADS_PER_LOOP):
        heads = list(range(h0, h0 + HEADS_PER_LOOP))
        qs = [_pad_rows(qs_ref[:, h * LANE:(h + 1) * LANE], tqp) for h in heads]

        def step_all(get_k, get_vt, bias, carries, heads=heads, qs=qs):
            sts = [_nt_dot(get_k(h), q) * scale + bias for h, q in zip(heads, qs)]
            return _softmax_steps(sts, [get_vt(h) for h in heads], carries)

        carries = tuple(_softmax_init(tqp) for _ in heads)
        if ncp:
            def past_body(c, carries, step_all=step_all):
                r0 = pl.multiple_of(c * wp, wp)
                return step_all(lambda h: pk_ref[pl.ds(r0, wp), h * LANE:(h + 1) * LANE],
                                lambda h: pvt_ref[c, h * LANE:(h + 1) * LANE, :], biasp_sc[c], carries)
            carries = lax.fori_loop(0, ncp, past_body, carries)

        def new_body(jn, carries, step_all=step_all):
            r0 = pl.multiple_of(jn * wn, wn)
            return step_all(lambda h: kn_ref[pl.ds(r0, wn), h * LANE:(h + 1) * LANE],
                            lambda h: vnt_ref[jn, h * LANE:(h + 1) * LANE, :], biasn_sc[jn], carries)
        carries = lax.fori_loop(0, nnew, new_body, carries)

        for h, (m, l, acc) in zip(heads, carries):
            o_ref[:, h * LANE:(h + 1) * LANE] = (acc / l).T[:tq].astype(o_ref.dtype)


def _dsa_attention(qx, yb, vnt, zf, zb, past, layer, b, t):
    n = b * t
    plen = 0 if past is None else past[0].shape[2]
    topk = min(TOPK_MAX, (plen + t) // 4)
    tq, tqp, nq, wp, ncp = _attn_geometry(t, plen)
    in_specs = [
        pl.BlockSpec((tq, HEAD_W), lambda bi, i: (bi * nq + i, 2)),
        pl.BlockSpec((tq, HEAD_W), lambda bi, i: (bi * nq + i, 1)),
        pl.BlockSpec((tq, LANE), lambda bi, i: (bi * nq + i, 1)),
        pl.BlockSpec((t, HEAD_W), lambda bi, i: (bi, 1)),
        pl.BlockSpec((None, nq, HEAD_W, tq), lambda bi, i: (bi, 0, 0, 0)),
        pl.BlockSpec((t, LANE), lambda bi, i: (bi, 0)),
    ]
    args = [qx, qx, zf, yb, vnt, zb]
    scratch = [pltpu.VMEM((nq, tq, tqp), I32), pltpu.VMEM((nq, tq, tqp), F32)]
    if past is not None:
        pk, pvt, pki = past
        in_specs += [pl.BlockSpec((None, None, plen, HEAD_W), lambda bi, i: (layer, bi, 0, 0)),
                     pl.BlockSpec((None, None, ncp, HEAD_W, wp), lambda bi, i: (layer, bi, 0, 0, 0)),
                     pl.BlockSpec((None, None, plen, LANE), lambda bi, i: (layer, bi, 0, 0))]
        args += [pk, pvt, pki]
        scratch += [pltpu.VMEM((ncp, wp, tqp), I32), pltpu.VMEM((ncp, wp, tqp), F32)]
    return pl.pallas_call(
        functools.partial(_dsa_kernel, tq=tq, tqp=tqp, wp=wp, ncp=ncp, topk=topk),
        grid=(b, nq),
        in_specs=in_specs,
        out_specs=pl.BlockSpec((tq, HEAD_W), lambda bi, i: (bi * nq + i, 0)),
        out_shape=jax.ShapeDtypeStruct((n, HEAD_W), BF16),
        scratch_shapes=scratch,
        compiler_params=_cparams(("arbitrary", "arbitrary")),
        name="dsa_attn",
    )(*args)


def _merge_kernel(od_ref, os_ref, wa_ref, wb_ref, ga_ref, gb_ref, o_ref):
    a = jnp.dot(od_ref[...], wa_ref[...], preferred_element_type=F32)
    bq = jnp.dot(os_ref[...], wb_ref[...], preferred_element_type=F32)
    sig = lambda z: 1.0 / (1.0 + jnp.exp(-z.astype(F32)))
    o_ref[...] = (sig(ga_ref[...]) * a + sig(gb_ref[...]) * bq).astype(o_ref.dtype)


def _merge(od, os_, wa, wb, qx, d):
    n = od.shape[0]
    tm = min(n, 1024)
    tn = min(d, 1024)
    gate0 = (3 * HEAD_W) // tn
    nd = d // tn
    return pl.pallas_call(
        _merge_kernel,
        grid=(n // tm, nd),
        in_specs=[
            pl.BlockSpec((tm, HEAD_W), lambda i, j: (i, 0)),
            pl.BlockSpec((tm, HEAD_W), lambda i, j: (i, 0)),
            pl.BlockSpec((HEAD_W, tn), lambda i, j: (0, j)),
            pl.BlockSpec((HEAD_W, tn), lambda i, j: (0, j)),
            pl.BlockSpec((tm, tn), lambda i, j: (i, gate0 + j)),
            pl.BlockSpec((tm, tn), lambda i, j: (i, gate0 + nd + j)),
        ],
        out_specs=pl.BlockSpec((tm, tn), lambda i, j: (i, j)),
        out_shape=jax.ShapeDtypeStruct((n, d), BF16),
        compiler_params=_cparams(("arbitrary", "arbitrary")),
        name="merge",
    )(od, os_, wa, wb, qx, qx)


def _outproj_kernel(m_ref, w_ref, x_ref, g1_ref, n2_ref, sc_ref, sh_ref, wrh_ref, wrl_ref, br_ref,
                    x1_ref, hx_ref):
    d = x_ref.shape[-1]
    y = jnp.dot(m_ref[...], w_ref[...], preferred_element_type=F32)
    x1 = x_ref[...] + g1_ref[...] * y
    x1_ref[...] = x1
    h = x1 * lax.rsqrt(jnp.mean(x1 * x1, axis=-1, keepdims=True) + RMS_EPS) * n2_ref[...]
    h = h * (1.0 + sc_ref[...]) + sh_ref[...]
    hi = h.astype(BF16)
    lo = (h - hi.astype(F32)).astype(BF16)
    lg = (jnp.dot(hi, wrh_ref[...], preferred_element_type=F32)
          + jnp.dot(lo, wrh_ref[...], preferred_element_type=F32)
          + jnp.dot(hi, wrl_ref[...], preferred_element_type=F32))
    hx_ref[:, :d] = h
    hx_ref[:, d:] = lg + br_ref[...]


def _outproj(merged, w_out, x, mod5, layer, boff, n2, wr_hi, wr_lo, br):
    b, t, d = x.shape
    tm = min(t, 512)
    nt = t // tm
    row = lambda k: pl.BlockSpec((None, None, None, 1, d), lambda bi, i, k=k: (layer, boff + bi, k, 0, 0))
    return pl.pallas_call(
        _outproj_kernel,
        grid=(b, nt),
        in_specs=[
            pl.BlockSpec((tm, d), lambda bi, i: (bi * nt + i, 0)),
            pl.BlockSpec((d, d), lambda bi, i: (0, 0)),
            pl.BlockSpec((None, tm, d), lambda bi, i: (bi, i, 0)),
            row(2),
            pl.BlockSpec((1, d), lambda bi, i: (0, 0)),
            row(4), row(3),
            pl.BlockSpec((d, LANE), lambda bi, i: (0, 0)),
            pl.BlockSpec((d, LANE), lambda bi, i: (0, 0)),
            pl.BlockSpec((1, LANE), lambda bi, i: (0, 0)),
        ],
        out_specs=[
            pl.BlockSpec((None, tm, d), lambda bi, i: (bi, i, 0)),
            pl.BlockSpec((tm, d + LANE), lambda bi, i: (bi * nt + i, 0)),
        ],
        out_shape=[jax.ShapeDtypeStruct((b, t, d), F32),
                   jax.ShapeDtypeStruct((b * t, d + LANE), F32)],
        compiler_params=_cparams(("arbitrary", "arbitrary")),
        name="outproj",
    )(merged, w_out, x, mod5, n2.reshape(1, d), mod5, mod5, wr_hi, wr_lo, br)


def _route_kernel(lg_ref, o_ref):
    lt = lg_ref[...].T
    tm = lt.shape[1]
    e = EXPERTS_PER_GROUP
    rid = lax.broadcasted_iota(I32, (e, tm), 0)
    gl = jnp.where(rid < N_GROUPS, lt[0:e], -jnp.inf)
    gmax = jnp.max(gl, axis=0, keepdims=True)
    ge = jnp.exp(gl - gmax)
    pg = ge / jnp.sum(ge, axis=0, keepdims=True)
    gidx = jnp.min(jnp.where(gl == gmax, rid, e), axis=0, keepdims=True)
    gw = jnp.where(rid == gidx, pg, 0.0)
    parts = []
    for g in range(N_GROUPS):
        el = lt[e * (g + 1):e * (g + 2)]
        m1 = jnp.max(el, axis=0, keepdims=True)
        i1 = jnp.min(jnp.where(el == m1, rid, e), axis=0, keepdims=True)
        el2 = jnp.where(rid == i1, -jnp.inf, el)
        m2 = jnp.max(el2, axis=0, keepdims=True)
        i2 = jnp.min(jnp.where(el2 == m2, rid, e), axis=0, keepdims=True)
        e2 = jnp.exp(m2 - m1)
        p1 = 1.0 / (1.0 + e2)
        p2 = e2 / (1.0 + e2)
        within = jnp.where(rid == i1, p1, 0.0) + jnp.where(rid == i2, p2, 0.0)
        parts.append(within * gw[g:g + 1])
    parts.append(jnp.broadcast_to(gidx.astype(F32), (e, tm)))
    parts.append(jnp.zeros((LANE - (N_GROUPS + 1) * e, tm), F32))
    o_ref[...] = jnp.concatenate(parts, axis=0).T


GROUP_ID_COL = N_GROUPS * EXPERTS_PER_GROUP


def _route(x, col_block):
    n = x.shape[0]
    tm = min(n, 512)
    return pl.pallas_call(
        _route_kernel,
        grid=(n // tm,),
        in_specs=[pl.BlockSpec((tm, LANE), lambda i: (i, col_block))],
        out_specs=pl.BlockSpec((tm, LANE), lambda i: (i, 0)),
        out_shape=jax.ShapeDtypeStruct((n, LANE), F32),
        compiler_params=_cparams(("arbitrary",)),
        name="route",
    )(x)


def _moe_plan(gid, ts):
    n = gid.shape[0]
    ntile = n // ts + N_GROUPS
    onehot = (gid[:, None] == jnp.arange(N_GROUPS, dtype=I32)[None, :]).astype(I32)
    csum = jnp.cumsum(onehot, axis=0)
    rank = jnp.sum((csum - onehot) * onehot, axis=1)
    tiles = (csum[-1] + ts - 1) // ts
    tile_end = jnp.cumsum(tiles)
    pos = jnp.sum(onehot * ((tile_end - tiles) * ts)[None, :], axis=1) + rank
    src = jnp.zeros((ntile * ts,), I32).at[pos].set(jnp.arange(n, dtype=I32))
    tile_gid = jnp.sum((jnp.arange(ntile, dtype=I32)[:, None] >= tile_end[None, :]).astype(I32), axis=1)
    return pos.astype(I32), src, jnp.minimum(tile_gid, N_GROUPS - 1).astype(I32)


def _row_gather(idx_ref, base, x_hbm, buf, sem):
    rows = buf.shape[0]

    def copy(r, row):
        return pltpu.make_async_copy(x_hbm.at[pl.ds(row, 1)], buf.at[pl.ds(r, 1)], sem)

    def start(r, _):
        copy(r, idx_ref[base + r]).start()
        return 0
    lax.fori_loop(0, rows, start, 0)

    def wait(r, _):
        copy(r, 0).wait()
        return 0
    lax.fori_loop(0, rows, wait, 0)


def _dispatch_kernel(src_ref, x_hbm, h_ref, lg_ref, buf, sem):
    _row_gather(src_ref, pl.program_id(0) * buf.shape[0], x_hbm, buf, sem)
    d = h_ref.shape[1]
    h_ref[...] = buf[:, :d].astype(h_ref.dtype)
    lg_ref[...] = buf[:, d:]


def _dispatch(hx, src, ts):
    npad = src.shape[0]
    d = hx.shape[1] - LANE
    return pl.pallas_call(
        _dispatch_kernel,
        grid_spec=pltpu.PrefetchScalarGridSpec(
            num_scalar_prefetch=1,
            grid=(npad // ts,),
            in_specs=[pl.BlockSpec(memory_space=pl.ANY)],
            out_specs=[pl.BlockSpec((ts, d), lambda i, s: (i, 0)),
                       pl.BlockSpec((ts, LANE), lambda i, s: (i, 0))],
            scratch_shapes=[pltpu.VMEM((ts, d + LANE), F32), pltpu.SemaphoreType.DMA(())],
        ),
        out_shape=[jax.ShapeDtypeStruct((npad, d), BF16), jax.ShapeDtypeStruct((npad, LANE), F32)],
        compiler_params=_cparams(("arbitrary",)),
        name="moe_dispatch",
    )(src, hx)


def _moe_up_kernel(gid_ref, h_ref, w1_ref, w3_ref, comb_ref, o_ref):
    col = gid_ref[pl.program_id(0)] * EXPERTS_PER_GROUP + pl.program_id(1)
    h = h_ref[...]
    a = jnp.dot(h, w1_ref[...], preferred_element_type=F32)
    u = jnp.dot(h, w3_ref[...], preferred_element_type=F32)
    comb = comb_ref[...]
    lane = lax.broadcasted_iota(I32, comb.shape, 1)
    cw = jnp.sum(jnp.where(lane == col, comb, 0.0), axis=1, keepdims=True)
    o_ref[...] = (a * (1.0 / (1.0 + jnp.exp(-a))) * u * cw).astype(o_ref.dtype)


def _moe_up(hs, w1, w3, comb, tile_gid, ts):
    npad, d = hs.shape
    f = w1.shape[2]
    e = EXPERTS_PER_GROUP
    return pl.pallas_call(
        _moe_up_kernel,
        grid_spec=pltpu.PrefetchScalarGridSpec(
            num_scalar_prefetch=1,
            grid=(npad // ts, e),
            in_specs=[
                pl.BlockSpec((ts, d), lambda i, j, g: (i, 0)),
                pl.BlockSpec((None, d, f), lambda i, j, g: (g[i] * e + j, 0, 0)),
                pl.BlockSpec((None, d, f), lambda i, j, g: (g[i] * e + j, 0, 0)),
                pl.BlockSpec((ts, LANE), lambda i, j, g: (i, 0)),
            ],
            out_specs=pl.BlockSpec((ts, f), lambda i, j, g: (i, j)),
        ),
        out_shape=jax.ShapeDtypeStruct((npad, e * f), BF16),
        compiler_params=_cparams(("arbitrary", "arbitrary")),
        name="moe_up",
    )(tile_gid, hs, w1, w3, comb)


def _moe_down_kernel(gid_ref, a_ref, w_ref, o_ref):
    o_ref[...] = jnp.dot(a_ref[...], w_ref[...], preferred_element_type=F32)


def _moe_down(act, w2, tile_gid, ts):
    npad, kk = act.shape
    d = w2.shape[2]
    return pl.pallas_call(
        _moe_down_kernel,
        grid_spec=pltpu.PrefetchScalarGridSpec(
            num_scalar_prefetch=1,
            grid=(npad // ts,),
            in_specs=[pl.BlockSpec((ts, kk), lambda i, g: (i, 0)),
                      pl.BlockSpec((None, kk, d), lambda i, g: (g[i], 0, 0))],
            out_specs=pl.BlockSpec((ts, d), lambda i, g: (i, 0)),
        ),
        out_shape=jax.ShapeDtypeStruct((npad, d), F32),
        compiler_params=_cparams(("arbitrary",)),
        name="moe_down",
    )(tile_gid, act, w2)


def _combine_kernel(pos_ref, ys_hbm, x_ref, g2_ref, gn_ref, sc_ref, sh_ref, x2_ref, hn_ref, buf, sem, *, final):
    tm = buf.shape[0]
    _row_gather(pos_ref, (pl.program_id(0) * pl.num_programs(1) + pl.program_id(1)) * tm, ys_hbm, buf, sem)
    x2 = x_ref[...] + g2_ref[...] * buf[...]
    x2_ref[...] = x2
    y = x2 * lax.rsqrt(jnp.mean(x2 * x2, axis=-1, keepdims=True) + RMS_EPS) * gn_ref[...]
    if not final:
        y = y * (1.0 + sc_ref[...]) + sh_ref[...]
    hn_ref[...] = y.astype(hn_ref.dtype)


def _combine(ys, pos, x1, mod5, layer, boff, g_next, next_layer, final):
    b, t, d = x1.shape
    tm = min(t, 256)
    nt = t // tm
    row = lambda l, k: pl.BlockSpec((None, None, None, 1, d), lambda bi, i, p, l=l, k=k: (l, boff + bi, k, 0, 0))
    return pl.pallas_call(
        functools.partial(_combine_kernel, final=final),
        grid_spec=pltpu.PrefetchScalarGridSpec(
            num_scalar_prefetch=1,
            grid=(b, nt),
            in_specs=[
                pl.BlockSpec(memory_space=pl.ANY),
                pl.BlockSpec((None, tm, d), lambda bi, i, p: (bi, i, 0)),
                row(layer, 5),
                pl.BlockSpec((1, d), lambda bi, i, p: (0, 0)),
                row(next_layer, 1), row(next_layer, 0),
            ],
            out_specs=[
                pl.BlockSpec((None, tm, d), lambda bi, i, p: (bi, i, 0)),
                pl.BlockSpec((None, tm, d), lambda bi, i, p: (bi, i, 0)),
            ],
            scratch_shapes=[pltpu.VMEM((tm, d), F32), pltpu.SemaphoreType.DMA(())],
        ),
        out_shape=[jax.ShapeDtypeStruct((b, t, d), F32),
                   jax.ShapeDtypeStruct((b, t, d), F32 if final else BF16)],
        compiler_params=_cparams(("arbitrary", "arbitrary")),
        name="moe_combine",
    )(pos, ys, x1, mod5, g_next.reshape(1, d), mod5, mod5)


def _pack_weights(w_in, w_br_diff, w_br_dsa, w_out, w_rg, b_rg, w_re, b_re, w_e1, w_e3, w_e2, d):
    hw = HEAD_W
    o_qd, o_kd, o_vd, o_qs, o_ks, o_vs, o_qi = (k * hw for k in range(7))
    o_ki = 7 * hw
    o_wi = o_ki + IDX_DIM
    o_gt = o_wi + N_IDX_HEADS
    sl = lambda o, n: w_in[:, :, o:o + n]
    wx = jnp.concatenate([sl(o_qd, hw), sl(o_qi, hw), sl(o_qs, hw), sl(o_gt, 2 * d)], axis=-1).astype(BF16)
    wy = jnp.concatenate([sl(o_kd, hw), sl(o_ks, hw), sl(o_vd, hw), sl(o_vs, hw)], axis=-1).astype(BF16)
    depth = w_in.shape[0]
    pad = jnp.zeros((depth, d, 2 * LANE - 2 * IDX_DIM - N_IDX_HEADS), w_in.dtype)
    wz = jnp.concatenate([sl(o_ki, IDX_DIM), sl(o_ki, IDX_DIM), sl(o_wi, N_IDX_HEADS), pad], axis=-1).astype(BF16)
    ne = N_GROUPS * EXPERTS_PER_GROUP
    rpad0 = jnp.zeros((depth, d, EXPERTS_PER_GROUP - N_GROUPS), F32)
    rpad1 = jnp.zeros((depth, d, LANE - EXPERTS_PER_GROUP - ne), F32)
    wr = jnp.concatenate([w_rg, rpad0, w_re, rpad1], axis=-1)
    wr_hi = wr.astype(BF16)
    wr_lo = (wr - wr_hi.astype(F32)).astype(BF16)
    br = jnp.concatenate([b_rg, jnp.zeros((depth, EXPERTS_PER_GROUP - N_GROUPS), F32), b_re,
                          jnp.zeros((depth, LANE - EXPERTS_PER_GROUP - ne), F32)], axis=-1).reshape(depth, 1, LANE)
    f = w_e1.shape[-1]
    return dict(
        wx=wx, wy=wy, wz=wz,
        wa=w_br_diff.astype(BF16), wb=w_br_dsa.astype(BF16), wo=w_out.astype(BF16),
        wr_hi=wr_hi, wr_lo=wr_lo, br=br,
        w1=w_e1.reshape(depth, ne, d, f).astype(BF16), w3=w_e3.reshape(depth, ne, d, f).astype(BF16),
        w2=w_e2.reshape(depth, N_GROUPS, EXPERTS_PER_GROUP * f, d).astype(BF16),
    )


def _trunk(x, pos, boff, mod5, past, pw, norm1, norm2, norm_f, lam_params, subln_g):
    b, t, d = x.shape
    n = b * t
    depth = norm1.shape[0]
    rows = n if t < 1024 else t

    def tables(head_dim, width, active):
        tab = _rope_tables(pos, head_dim, width, active)
        return jnp.tile(tab, (1, rows // t, 1)) if rows != t else tab

    tab64 = tables(DIFF_QK_DIM, LANE, LANE)
    tab128 = tables(DSA_HEAD_DIM, LANE, LANE)
    tabz = tables(IDX_DIM, 2 * LANE, LANE)
    halves = (IDX_DIM // 8, DSA_HEAD_DIM // 8, IDX_DIM // 8)
    tabs = (tab64, tab128, tabz)
    gate_tiles = (2 * d) // HEAD_W
    tq, _, nq, _, _ = _attn_geometry(t, 0)
    ts = 512 if n >= 8192 else 128

    h = _normmod(x, norm1[0], mod5, 0, boff, 1, 0)
    new_rows = ([], [], [], [], [])
    y = None
    for l in range(depth):
        h2d = h.reshape(n, d)
        (qx,) = _proj(h2d, pw["wx"][l], tabs, halves, (0, 0, 1) + (-1,) * gate_tiles, HEAD_W, (BF16,), "proj_q")
        yf, yb = _proj(h2d, pw["wy"][l], tabs, halves, (0, 1, -1, -1), HEAD_W, (F32, BF16), "proj_kv")
        zf, zb = _proj(h2d, pw["wz"][l], tabs, halves, (2,), 2 * LANE, (F32, BF16), "proj_idx")
        for lst, r in zip(new_rows, (yf[:, 0:HEAD_W], yf[:, 2 * HEAD_W:3 * HEAD_W], yf[:, HEAD_W:2 * HEAD_W],
                                     yf[:, 3 * HEAD_W:4 * HEAD_W], zf[:, 0:IDX_DIM])):
            lst.append(r)
        vdt = _transposed_chunks(yb[:, 2 * HEAD_W:3 * HEAD_W], b, nq, tq)
        vst = _transposed_chunks(yb[:, 3 * HEAD_W:4 * HEAD_W], b, nq, tq)
        lam_init = 0.8 - 0.6 * math.exp(-0.3 * l)
        lp = tuple(p[l] for p in lam_params)
        past_d = None if past is None else (past["dk"], past["dvt"])
        past_s = None if past is None else (past["sk"], past["svt"], past["ik"])
        od = _diff_attention(qx, yb, vdt, lp, subln_g[l], past_d, l, b, t, lam_init)
        os_ = _dsa_attention(qx, yb, vst, zf, zb, past_s, l, b, t)
        merged = _merge(od, os_, pw["wa"][l], pw["wb"][l], qx, d)
        x1, hx = _outproj(merged, pw["wo"][l], x, mod5, l, boff, norm2[l],
                          pw["wr_hi"][l], pw["wr_lo"][l], pw["br"][l])
        gid = _route(hx, d // LANE)[:, GROUP_ID_COL].astype(I32)
        pos, src, tile_gid = _moe_plan(gid, ts)
        hs, lgs = _dispatch(hx, src, ts)
        act = _moe_up(hs, pw["w1"][l], pw["w3"][l], _route(lgs, 0), tile_gid, ts)
        ys = _moe_down(act, pw["w2"][l], tile_gid, ts)
        final = l == depth - 1
        g_next = norm_f if final else norm1[l + 1]
        x, h = _combine(ys, pos, x1, mod5, l, boff, g_next, 0 if final else l + 1, final)
        if final:
            y = h
    return y, new_rows


def kernel(x_prompt, x_sample, cache_diff_k, cache_diff_v, cache_dsa_k, cache_dsa_v, cache_idx_k, c_prompt, c_sample, norm1, norm2, norm_f, w_ada, b_ada, w_in, lambda_q1, lambda_k1, lambda_q2, lambda_k2, subln_g, w_br_diff, w_br_dsa, w_out, w_router_group, b_router_group, w_router_expert, b_router_expert, w_expert_gate, w_expert_up, w_expert_down):
    bp, tp, d = x_prompt.shape
    bs, ts, _ = x_sample.shape
    depth = norm1.shape[0]
    plen = cache_diff_k.shape[2]
    assert plen % CHUNK == 0 and tp % CHUNK == 0

    pw = _pack_weights(w_in, w_br_diff, w_br_dsa, w_out, w_router_group, b_router_group, w_router_expert,
                       b_router_expert, w_expert_gate, w_expert_up, w_expert_down, d)
    mod = _ada(jnp.concatenate([c_prompt, c_sample], axis=0), w_ada, b_ada)
    mod5 = mod.reshape(depth, bp + bs, 6, 1, d)
    lam_params = (lambda_q1, lambda_k1, lambda_q2, lambda_k2)

    pos_p = jnp.arange(tp, dtype=jnp.int32)
    y_p, rows_p = _trunk(x_prompt, pos_p, 0, mod5, None, pw, norm1, norm2, norm_f, lam_params, subln_g)

    _, _, _, wp, ncp = _attn_geometry(ts, plen)
    rows2d = lambda c: c.reshape(depth, bs, plen, HEAD_W).astype(BF16)
    chunks_t = lambda c: jnp.swapaxes(c.reshape(depth, bs, ncp, wp, HEAD_W), 3, 4).astype(BF16)
    past = dict(dk=rows2d(cache_diff_k), dvt=chunks_t(cache_diff_v),
                sk=rows2d(cache_dsa_k), svt=chunks_t(cache_dsa_v),
                ik=jnp.concatenate([cache_idx_k, cache_idx_k], axis=-1).astype(BF16))
    pos_s = plen + jnp.arange(ts, dtype=jnp.int32)
    y_s, rows_s = _trunk(x_sample, pos_s, bp, mod5, past, pw, norm1, norm2, norm_f, lam_params, subln_g)

    def finish(rows, b, t):
        dk, dv, sk, sv, ik = (jnp.stack(r, axis=0) for r in rows)
        return (dk.reshape(depth, b, t, N_DIFF_HEADS, 2, DIFF_QK_DIM), dv.reshape(depth, b, t, N_DIFF_HEADS, DIFF_V_DIM),
                sk.reshape(depth, b, t, N_DSA_HEADS, DSA_HEAD_DIM), sv.reshape(depth, b, t, N_DSA_HEADS, DSA_HEAD_DIM),
                ik.reshape(depth, b, t, IDX_DIM))

    return (y_p, y_s) + finish(rows_p, bp, tp) + finish(rows_s, bs, ts)
```

```python
import functools
import math

import numpy as np
import jax
import jax.numpy as jnp
from jax import lax
from jax.experimental import pallas as pl
from jax.experimental.pallas import tpu as pltpu

F32 = jnp.float32
BF16 = jnp.bfloat16
I32 = jnp.int32

CHUNK = 64
ROPE_THETA = 500000.0
RMS_EPS = 1e-6
N_DIFF_HEADS = 8
DIFF_QK_DIM = 64
DIFF_V_DIM = 128
N_DSA_HEADS = 8
DSA_HEAD_DIM = 128
N_IDX_HEADS = 16
IDX_DIM = 64
TOPK_MAX = 256
N_GROUPS = 4
EXPERTS_PER_GROUP = 8
HEAD_W = 1024
LANE = 128
SUBLANE = 8
NEG = -1e30
VMEM_LIMIT = 56 * 1024 * 1024
HEADS_PER_LOOP = 8
PAST_CHUNK = 512
COUNT_CHAINS = 4

_NEG_INF_BITS = int(np.array(-np.inf, np.float32).view(np.int32))
KEY_NEG_INF = int(np.int32(_NEG_INF_BITS ^ 0x7FFFFFFF))
INT_MIN = -(2 ** 31)


def _cparams(sem):
    return pltpu.CompilerParams(dimension_semantics=sem, vmem_limit_bytes=VMEM_LIMIT)


def _nt_dot(a, b):
    return lax.dot_general(a, b, (((1,), (1,)), ((), ())), preferred_element_type=F32)


def _ada_kernel(c_ref, w_ref, b_ref, o_ref):
    c = c_ref[...]
    a = (c * (1.0 / (1.0 + jnp.exp(-c)))).astype(BF16)
    o_ref[...] = jnp.dot(a, w_ref[...].astype(BF16), preferred_element_type=F32) + b_ref[...]


def _ada(c_all, w_ada, b_ada):
    depth, d, n6 = w_ada.shape
    r = c_all.shape[0]
    tn = 1024
    return pl.pallas_call(
        _ada_kernel,
        grid=(depth, n6 // tn),
        in_specs=[
            pl.BlockSpec((r, d), lambda l, j: (0, 0)),
            pl.BlockSpec((None, d, tn), lambda l, j: (l, 0, j)),
            pl.BlockSpec((None, 1, tn), lambda l, j: (l, 0, j)),
        ],
        out_specs=pl.BlockSpec((None, r, tn), lambda l, j: (l, 0, j)),
        out_shape=jax.ShapeDtypeStruct((depth, r, n6), F32),
        compiler_params=_cparams(("arbitrary", "arbitrary")),
        name="ada",
    )(c_all, w_ada, b_ada.reshape(depth, 1, n6))


def _normmod_kernel(x_ref, g_ref, sc_ref, sh_ref, o_ref):
    x = x_ref[...]
    y = x * lax.rsqrt(jnp.mean(x * x, axis=-1, keepdims=True) + RMS_EPS) * g_ref[...]
    o_ref[...] = (y * (1.0 + sc_ref[...]) + sh_ref[...]).astype(o_ref.dtype)


def _normmod(x, g, mod5, layer, boff, k_scale, k_shift):
    b, t, d = x.shape
    tt = min(t, 512)
    return pl.pallas_call(
        _normmod_kernel,
        grid=(b, t // tt),
        in_specs=[
            pl.BlockSpec((None, tt, d), lambda bi, i: (bi, i, 0)),
            pl.BlockSpec((1, d), lambda bi, i: (0, 0)),
            pl.BlockSpec((None, None, None, 1, d), lambda bi, i: (layer, boff + bi, k_scale, 0, 0)),
            pl.BlockSpec((None, None, None, 1, d), lambda bi, i: (layer, boff + bi, k_shift, 0, 0)),
        ],
        out_specs=pl.BlockSpec((None, tt, d), lambda bi, i: (bi, i, 0)),
        out_shape=jax.ShapeDtypeStruct((b, t, d), BF16),
        compiler_params=_cparams(("arbitrary", "arbitrary")),
        name="normmod",
    )(x, g.reshape(1, d), mod5, mod5)


def _rope_tables(pos, head_dim, width, active):
    rot = head_dim // 4
    half = rot // 2
    inv = ROPE_THETA ** (-jnp.arange(half, dtype=F32) * (2.0 / rot))
    ang = pos.astype(F32)[:, None] * inv[None, :]
    cos, sin = jnp.cos(ang), jnp.sin(ang)
    col = np.arange(width)
    ch = col % head_dim
    first = (ch < half) & (col < active)
    second = (ch >= half) & (ch < rot) & (col < active)
    idx = np.where(ch < half, ch, np.where(ch < rot, ch - half, 0))
    c = jnp.where((first | second)[None, :], cos[:, idx], 1.0)
    s1 = jnp.where(first[None, :], -sin[:, idx], 0.0)
    s2 = jnp.where(second[None, :], sin[:, idx], 0.0)
    return jnp.stack([c, s1, s2]).astype(F32)


def _proj_kernel(*refs, kinds, halves, n_tab, n_out):
    h_ref, w_ref = refs[0], refs[1]
    tab_refs = refs[2:2 + n_tab]
    out_refs = refs[2 + n_tab:2 + n_tab + n_out]
    j = pl.program_id(1)
    acc = jnp.dot(h_ref[...], w_ref[...], preferred_element_type=F32)
    tn = acc.shape[1]

    def emit(kind):
        if kind < 0:
            for o in out_refs:
                o[...] = acc.astype(o.dtype)
            return
        tab = tab_refs[kind]
        half = halves[kind]
        tw = tab.shape[2]
        for g in range(tn // LANE):
            xg = acc[:, g * LANE:(g + 1) * LANE]
            t0 = (g * LANE) % tw
            c = tab[0, :, t0:t0 + LANE]
            s1 = tab[1, :, t0:t0 + LANE]
            s2 = tab[2, :, t0:t0 + LANE]
            og = xg * c + pltpu.roll(xg, LANE - half, 1) * s1 + pltpu.roll(xg, half, 1) * s2
            for o in out_refs:
                o[:, g * LANE:(g + 1) * LANE] = og.astype(o.dtype)

    distinct = sorted(set(kinds))
    if len(distinct) == 1:
        emit(distinct[0])
    else:
        for kind in distinct:
            pred = functools.reduce(jnp.logical_or, [j == jj for jj, kk in enumerate(kinds) if kk == kind])
            pl.when(pred)(functools.partial(emit, kind))


def _proj(h2d, w, tabs, halves, kinds, tn, out_dtypes, name):
    n, d = h2d.shape
    c = w.shape[1]
    tm = min(n, 1024)
    assert c == tn * len(kinds)
    in_specs = [
        pl.BlockSpec((tm, d), lambda i, j: (i, 0)),
        pl.BlockSpec((d, tn), lambda i, j: (0, j)),
    ]
    for tab in tabs:
        nblk = tab.shape[1] // tm
        in_specs.append(pl.BlockSpec((3, tm, tab.shape[2]), lambda i, j, nblk=nblk: (0, i % nblk, 0)))
    return pl.pallas_call(
        functools.partial(_proj_kernel, kinds=tuple(kinds), halves=tuple(halves), n_tab=len(tabs),
                          n_out=len(out_dtypes)),
        grid=(n // tm, c // tn),
        in_specs=in_specs,
        out_specs=[pl.BlockSpec((tm, tn), lambda i, j: (i, j)) for _ in out_dtypes],
        out_shape=[jax.ShapeDtypeStruct((n, c), dt) for dt in out_dtypes],
        compiler_params=_cparams(("arbitrary", "arbitrary")),
        name=name,
    )(h2d, w, *tabs)


def _pad_rows(a, rows):
    if a.shape[0] == rows:
        return a
    return jnp.concatenate([a, jnp.zeros((rows - a.shape[0], a.shape[1]), a.dtype)], axis=0)


def _tn_dot(a, b):
    return lax.dot_general(a, b, (((0,), (0,)), ((), ())), preferred_element_type=F32)


def _softmax_steps(sts, pv_fns, carries):
    m_news = [jnp.maximum(c[0], jnp.max(st, axis=0, keepdims=True)) for st, c in zip(sts, carries)]
    ps = [jnp.exp(st - m_new) for st, m_new in zip(sts, m_news)]
    pvs = [fn(p.astype(BF16)) for fn, p in zip(pv_fns, ps)]
    out = []
    for (m, l, acc), m_new, p, pv in zip(carries, m_news, ps, pvs):
        alpha = jnp.exp(m - m_new)
        out.append((m_new, alpha * l + jnp.sum(p, axis=0, keepdims=True), alpha * acc + pv))
    return tuple(out)


def _softmax_init(n):
    return (jnp.full((1, n), NEG, F32), jnp.zeros((1, n), F32), jnp.zeros((LANE, n), F32))


def _attn_geometry(t, plen):
    tq = min(t, 256)
    tqp = max(tq, LANE)
    wp = min(plen, PAST_CHUNK) if plen else 0
    ncp = plen // wp if plen else 0
    assert t % tq == 0 and tq % CHUNK == 0 and (plen == 0 or plen % wp == 0)
    return tq, tqp, t // tq, wp, ncp


def _transposed_chunks(v2d, b, nchunk, w):
    return jnp.swapaxes(v2d.reshape(b, nchunk, w, HEAD_W), 2, 3)


def _carry_load(scs, heads):
    m_sc, l_sc, acc_sc = scs
    return tuple((m_sc[h], l_sc[h], acc_sc[h]) for h in heads)


def _carry_store(scs, heads, carries):
    m_sc, l_sc, acc_sc = scs
    for h, (m, l, acc) in zip(heads, carries):
        m_sc[h] = m
        l_sc[h] = l
        acc_sc[h] = acc


def _diff_kernel(*refs, tq, tqp, ncp, lam_init):
    lq1, lk1, lq2, lk2, g_ref, q_ref, kn_ref, vnt_ref = refs[:8]
    if ncp:
        pk_ref, pv_ref, o_ref = refs[8:11]
        scs = refs[11:]
        c = pl.program_id(2)
    else:
        (o_ref,) = refs[8:]
    i = pl.program_id(1)
    wn = tq
    n2 = 2 * tqp
    lane = lax.broadcasted_iota(I32, (tqp, LANE), 1)
    lam = (jnp.exp(jnp.sum(lq1[...] * lk1[...], axis=1, keepdims=True))
           - jnp.exp(jnp.sum(lq2[...] * lk2[...], axis=1, keepdims=True)) + lam_init)
    krow = lax.broadcasted_iota(I32, (wn, n2), 0)
    qcol = lax.broadcasted_iota(I32, (wn, n2), 1)
    qcol = jnp.where(qcol >= tqp, qcol - tqp, qcol)
    diag_mask = (krow // CHUNK) <= (qcol // CHUNK)

    for h0 in range(0, N_DIFF_HEADS, HEADS_PER_LOOP):
        heads = list(range(h0, h0 + HEADS_PER_LOOP))
        qqs = []
        for h in heads:
            q = _pad_rows(q_ref[:, h * LANE:(h + 1) * LANE], tqp) * jnp.asarray(DIFF_QK_DIM ** -0.5, BF16)
            zero = jnp.zeros_like(q)
            qqs.append(jnp.concatenate([jnp.where(lane < DIFF_QK_DIM, q, zero),
                                        jnp.where(lane >= DIFF_QK_DIM, q, zero)], axis=0))

        def step_all(get_k, pv_fn, carries, mask, heads=heads, qqs=qqs):
            sts = [_nt_dot(get_k(h), qq) for h, qq in zip(heads, qqs)]
            if mask is not None:
                sts = [jnp.where(mask, st, NEG) for st in sts]
            return _softmax_steps(sts, [pv_fn(h) for h in heads], carries)

        def new_step(jn, carries, mask, step_all=step_all):
            r0 = pl.multiple_of(jn * wn, wn)
            return step_all(lambda h: kn_ref[pl.ds(r0, wn), h * LANE:(h + 1) * LANE],
                            lambda h: functools.partial(jnp.dot, vnt_ref[jn, h * LANE:(h + 1) * LANE, :],
                                                        preferred_element_type=F32), carries, mask)

        def new_rows_and_finish(carries, heads=heads, new_step=new_step):
            carries = lax.fori_loop(0, i, lambda jn, cs: new_step(jn, cs, None), carries)
            carries = new_step(i, carries, diag_mask)
            for h, (m, l, acc) in zip(heads, carries):
                o = acc / l
                od = o[:, :tqp] - lam * o[:, tqp:]
                od = od * lax.rsqrt(jnp.mean(od * od, axis=0, keepdims=True) + RMS_EPS)
                od = od * g_ref[...] * (1.0 - lam_init)
                o_ref[:, h * LANE:(h + 1) * LANE] = od.T[:tq].astype(o_ref.dtype)

        if ncp:
            @pl.when(c == 0)
            def _(heads=heads):
                _carry_store(scs, heads, tuple(_softmax_init(n2) for _ in heads))

            carries = step_all(lambda h: pk_ref[:, h * LANE:(h + 1) * LANE].astype(BF16),
                               lambda h: functools.partial(_tn_dot, pv_ref[:, h * LANE:(h + 1) * LANE].astype(BF16)),
                               _carry_load(scs, heads), None)
            _carry_store(scs, heads, carries)
            pl.when(c == ncp - 1)(lambda heads=heads, fin=new_rows_and_finish: fin(_carry_load(scs, heads)))
        else:
            new_rows_and_finish(tuple(_softmax_init(n2) for _ in heads))


def _diff_attention(qx, yb, vnt, lam_params, subln, past, layer, b, t, lam_init):
    n = b * t
    plen = 0 if past is None else past[0].shape[2]
    tq, tqp, nq, wp, ncp = _attn_geometry(t, plen)
    vec = lambda a: a.reshape(1, -1).astype(F32)
    small = pl.BlockSpec((1, DIFF_QK_DIM), lambda bi, i, *_: (0, 0))
    in_specs = [small, small, small, small,
                pl.BlockSpec((DIFF_V_DIM, 1), lambda bi, i, *_: (0, 0)),
                pl.BlockSpec((tq, HEAD_W), lambda bi, i, *_: (bi * nq + i, 0)),
                pl.BlockSpec((t, HEAD_W), lambda bi, i, *_: (bi, 0)),
                pl.BlockSpec((None, nq, HEAD_W, tq), lambda bi, i, *_: (bi, 0, 0, 0))]
    args = [vec(p) for p in lam_params] + [subln.reshape(-1, 1).astype(F32), qx, yb, vnt]
    grid, scratch = (b, nq), []
    if past is not None:
        chunk = pl.BlockSpec((None, None, wp, HEAD_W), lambda bi, i, c: (layer, bi, c, 0))
        in_specs += [chunk, chunk]
        args += list(past)
        grid = (b, nq, ncp)
        scratch = [pltpu.VMEM((N_DIFF_HEADS, 1, 2 * tqp), F32), pltpu.VMEM((N_DIFF_HEADS, 1, 2 * tqp), F32),
                   pltpu.VMEM((N_DIFF_HEADS, LANE, 2 * tqp), F32)]
    return pl.pallas_call(
        functools.partial(_diff_kernel, tq=tq, tqp=tqp, ncp=ncp, lam_init=lam_init),
        grid=grid,
        in_specs=in_specs,
        out_specs=pl.BlockSpec((tq, HEAD_W), lambda bi, i, *_: (bi * nq + i, 0)),
        out_shape=jax.ShapeDtypeStruct((n, HEAD_W), BF16),
        scratch_shapes=scratch,
        compiler_params=_cparams(("arbitrary",) * len(grid)),
        name="diff_attn",
    )(*args)


def _dsa_indexer(qi_ref, wq_ref, zk_ref, pki_ref, keyn_sc, biasn_sc, keyp_sc, biasp_sc, *, tq, tqp, wp, ncp, topk):
    i = pl.program_id(1)
    wn = tq
    nnew = i + 1
    kf = float(topk)

    lane = lax.broadcasted_iota(I32, (tqp, LANE), 1)
    wit = _pad_rows(wq_ref[...], tqp).T * (N_IDX_HEADS ** -0.5 * IDX_DIM ** -0.5)
    qqs, wcats = [], []
    for p in range(N_IDX_HEADS // 2):
        qp = _pad_rows(qi_ref[:, p * LANE:(p + 1) * LANE], tqp)
        zero = jnp.zeros_like(qp)
        qqs.append(jnp.concatenate([jnp.where(lane < IDX_DIM, qp, zero),
                                    jnp.where(lane >= IDX_DIM, qp, zero)], axis=0))
        wcats.append(jnp.concatenate([wit[2 * p:2 * p + 1], wit[2 * p + 1:2 * p + 2]], axis=1))

    def scores_t(kdup):
        acc = jnp.zeros((kdup.shape[0], tqp), F32)
        for qq, wc in zip(qqs, wcats):
            r = jnp.maximum(_nt_dot(kdup, qq), 0.0) * wc
            acc = acc + r[:, :tqp] + r[:, tqp:]
        return acc

    def to_key(s, visible):
        s = jnp.where(s == 0.0, 0.0, s)
        if visible is not None:
            s = jnp.where(visible, s, -jnp.inf)
        bits = lax.bitcast_convert_type(s, I32)
        return bits ^ (jnp.right_shift(bits, 31) & 0x7FFFFFFF)

    if ncp:
        def past_keys(c, _):
            r0 = pl.multiple_of(c * wp, wp)
            keyp_sc[c] = to_key(scores_t(pki_ref[pl.ds(r0, wp), :]), None)
            return 0
        lax.fori_loop(0, ncp, past_keys, 0)

    def new_keys(jn, _):
        r0 = pl.multiple_of(jn * wn, wn)
        s = scores_t(zk_ref[pl.ds(r0, wn), :])
        kpos = jn * wn + lax.broadcasted_iota(I32, (wn, tqp), 0)
        qpos = i * tq + lax.broadcasted_iota(I32, (wn, tqp), 1)
        keyn_sc[jn] = to_key(s, (kpos // CHUNK) <= (qpos // CHUNK))
        return 0
    lax.fori_loop(0, nnew, new_keys, 0)

    def count(pred):
        fr = COUNT_CHAINS * SUBLANE

        def fold(kk):
            m = jnp.where(pred(kk), 1.0, 0.0)
            return jnp.sum(m.reshape(kk.shape[0] // fr, fr, tqp), axis=0)
        acc = jnp.zeros((fr, tqp), F32)
        if ncp:
            acc = lax.fori_loop(0, ncp, lambda c, a: a + fold(keyp_sc[c]), acc)
        acc = lax.fori_loop(0, nnew, lambda jn, a: a + fold(keyn_sc[jn]), acc)
        return jnp.sum(acc, axis=0, keepdims=True)

    def pass_body(t, kth):
        cand = kth + jnp.left_shift(jnp.int32(1), 31 - t)
        cnt = count(lambda kk: kk >= cand)
        return jnp.where(cnt >= kf, cand, kth)
    kth = lax.fori_loop(0, 32, pass_body, jnp.full((1, tqp), INT_MIN, I32))

    need = kf - count(lambda kk: kk > kth)

    def lower_tri(w):
        ra = lax.broadcasted_iota(I32, (w, w), 0)
        rb = lax.broadcasted_iota(I32, (w, w), 1)
        return jnp.where(rb <= ra, 1.0, 0.0).astype(BF16)

    def bias_chunk(kk, seen, tril):
        eq = kk == kth
        eqf = jnp.where(eq, 1.0, 0.0)
        rank = seen + jnp.dot(tril, eqf.astype(BF16), preferred_element_type=F32)
        sel = ((kk > kth) | (eq & (rank <= need))) & (kk != KEY_NEG_INF)
        return jnp.where(sel, 0.0, NEG), seen + jnp.sum(eqf, axis=0, keepdims=True)

    seen = jnp.zeros((1, tqp), F32)
    if ncp:
        tril_p = lower_tri(wp)

        def past_bias(c, seen):
            bias, seen = bias_chunk(keyp_sc[c], seen, tril_p)
            biasp_sc[c] = bias
            return seen
        seen = lax.fori_loop(0, ncp, past_bias, seen)
    tril_n = lower_tri(wn)

    def new_bias(jn, seen):
        bias, seen = bias_chunk(keyn_sc[jn], seen, tril_n)
        biasn_sc[jn] = bias
        return seen
    lax.fori_loop(0, nnew, new_bias, seen)

def _dsa_kernel(*refs, tq, tqp, wp, ncp, topk):
    qs_ref, qi_ref, wq_ref, kn_ref, vnt_ref, zk_ref = refs[:6]
    idx = functools.partial(_dsa_indexer, tq=tq, tqp=tqp, wp=wp, ncp=ncp, topk=topk)
    if ncp:
        pk_ref, pv_ref, pki_ref, o_ref, keyn_sc, biasn_sc, keyp_sc, biasp_sc = refs[6:14]
        scs = refs[14:]
        c = pl.program_id(2)
        pl.when(c == 0)(lambda: idx(qi_ref, wq_ref, zk_ref, pki_ref, keyn_sc, biasn_sc, keyp_sc, biasp_sc))
    else:
        o_ref, keyn_sc, biasn_sc = refs[6:]
        idx(qi_ref, wq_ref, zk_ref, None, keyn_sc, biasn_sc, None, None)
    wn = tq
    nnew = pl.program_id(1) + 1

    scale = DSA_HEAD_DIM ** -0.5
    for h0 in range(0, N_DSA_HEADS, HEADS_PER_LOOP):
        heads = list(range(h0, h0 + HEADS_PER_LOOP))
        qs = [_pad_rows(qs_ref[:, h * LANE:(h + 1) * LANE], tqp) for h in heads]

        def step_all(get_k, pv_fn, bias, carries, heads=heads, qs=qs):
            sts = [_nt_dot(get_k(h), q) * scale + bias for h, q in zip(heads, qs)]
            return _softmax_steps(sts, [pv_fn(h) for h in heads], carries)

        def new_body(jn, carries, step_all=step_all):
            r0 = pl.multiple_of(jn * wn, wn)
            return step_all(lambda h: kn_ref[pl.ds(r0, wn), h * LANE:(h + 1) * LANE],
                            lambda h: functools.partial(jnp.dot, vnt_ref[jn, h * LANE:(h + 1) * LANE, :],
                                                        preferred_element_type=F32), biasn_sc[jn], carries)

        def new_rows_and_finish(carries, heads=heads, new_body=new_body):
            carries = lax.fori_loop(0, nnew, new_body, carries)
            for h, (m, l, acc) in zip(heads, carries):
                o_ref[:, h * LANE:(h + 1) * LANE] = (acc / l).T[:tq].astype(o_ref.dtype)

        if ncp:
            @pl.when(c == 0)
            def _(heads=heads):
                _carry_store(scs, heads, tuple(_softmax_init(tqp) for _ in heads))

            carries = step_all(lambda h: pk_ref[:, h * LANE:(h + 1) * LANE].astype(BF16),
                               lambda h: functools.partial(_tn_dot, pv_ref[:, h * LANE:(h + 1) * LANE].astype(BF16)),
                               biasp_sc[c], _carry_load(scs, heads))
            _carry_store(scs, heads, carries)
            pl.when(c == ncp - 1)(lambda heads=heads, fin=new_rows_and_finish: fin(_carry_load(scs, heads)))
        else:
            new_rows_and_finish(tuple(_softmax_init(tqp) for _ in heads))


def _dsa_attention(qx, yb, vnt, zf, zb, past, layer, b, t):
    n = b * t
    plen = 0 if past is None else past[0].shape[2]
    topk = min(TOPK_MAX, (plen + t) // 4)
    tq, tqp, nq, wp, ncp = _attn_geometry(t, plen)
    in_specs = [
        pl.BlockSpec((tq, HEAD_W), lambda bi, i, *_: (bi * nq + i, 2)),
        pl.BlockSpec((tq, HEAD_W), lambda bi, i, *_: (bi * nq + i, 1)),
        pl.BlockSpec((tq, LANE), lambda bi, i, *_: (bi * nq + i, 1)),
        pl.BlockSpec((t, HEAD_W), lambda bi, i, *_: (bi, 1)),
        pl.BlockSpec((None, nq, HEAD_W, tq), lambda bi, i, *_: (bi, 0, 0, 0)),
        pl.BlockSpec((t, LANE), lambda bi, i, *_: (bi, 0)),
    ]
    args = [qx, qx, zf, yb, vnt, zb]
    scratch = [pltpu.VMEM((nq, tq, tqp), I32), pltpu.VMEM((nq, tq, tqp), F32)]
    grid = (b, nq)
    if past is not None:
        chunk = pl.BlockSpec((None, None, wp, HEAD_W), lambda bi, i, c: (layer, bi, c, 0))
        in_specs += [chunk, chunk, pl.BlockSpec((None, None, plen, LANE), lambda bi, i, c: (layer, bi, 0, 0))]
        args += list(past)
        grid = (b, nq, ncp)
        scratch += [pltpu.VMEM((ncp, wp, tqp), I32), pltpu.VMEM((ncp, wp, tqp), F32),
                    pltpu.VMEM((N_DSA_HEADS, 1, tqp), F32), pltpu.VMEM((N_DSA_HEADS, 1, tqp), F32),
                    pltpu.VMEM((N_DSA_HEADS, LANE, tqp), F32)]
    return pl.pallas_call(
        functools.partial(_dsa_kernel, tq=tq, tqp=tqp, wp=wp, ncp=ncp, topk=topk),
        grid=grid,
        in_specs=in_specs,
        out_specs=pl.BlockSpec((tq, HEAD_W), lambda bi, i, *_: (bi * nq + i, 0)),
        out_shape=jax.ShapeDtypeStruct((n, HEAD_W), BF16),
        scratch_shapes=scratch,
        compiler_params=_cparams(("arbitrary",) * len(grid)),
        name="dsa_attn",
    )(*args)


def _merge_kernel(od_ref, os_ref, wa_ref, wb_ref, ga_ref, gb_ref, o_ref):
    a = jnp.dot(od_ref[...], wa_ref[...], preferred_element_type=F32)
    bq = jnp.dot(os_ref[...], wb_ref[...], preferred_element_type=F32)
    sig = lambda z: 1.0 / (1.0 + jnp.exp(-z.astype(F32)))
    o_ref[...] = (sig(ga_ref[...]) * a + sig(gb_ref[...]) * bq).astype(o_ref.dtype)


def _merge(od, os_, wa, wb, qx, d):
    n = od.shape[0]
    tm = min(n, 1024)
    tn = min(d, 1024)
    gate0 = (3 * HEAD_W) // tn
    nd = d // tn
    return pl.pallas_call(
        _merge_kernel,
        grid=(n // tm, nd),
        in_specs=[
            pl.BlockSpec((tm, HEAD_W), lambda i, j: (i, 0)),
            pl.BlockSpec((tm, HEAD_W), lambda i, j: (i, 0)),
            pl.BlockSpec((HEAD_W, tn), lambda i, j: (0, j)),
            pl.BlockSpec((HEAD_W, tn), lambda i, j: (0, j)),
            pl.BlockSpec((tm, tn), lambda i, j: (i, gate0 + j)),
            pl.BlockSpec((tm, tn), lambda i, j: (i, gate0 + nd + j)),
        ],
        out_specs=pl.BlockSpec((tm, tn), lambda i, j: (i, j)),
        out_shape=jax.ShapeDtypeStruct((n, d), BF16),
        compiler_params=_cparams(("arbitrary", "arbitrary")),
        name="merge",
    )(od, os_, wa, wb, qx, qx)


def _outproj_kernel(m_ref, w_ref, x_ref, g1_ref, n2_ref, sc_ref, sh_ref, wrh_ref, wrl_ref, br_ref,
                    x1_ref, hx_ref):
    d = x_ref.shape[-1]
    y = jnp.dot(m_ref[...], w_ref[...], preferred_element_type=F32)
    x1 = x_ref[...] + g1_ref[...] * y
    x1_ref[...] = x1
    h = x1 * lax.rsqrt(jnp.mean(x1 * x1, axis=-1, keepdims=True) + RMS_EPS) * n2_ref[...]
    h = h * (1.0 + sc_ref[...]) + sh_ref[...]
    hi = h.astype(BF16)
    lo = (h - hi.astype(F32)).astype(BF16)
    lg = (jnp.dot(hi, wrh_ref[...], preferred_element_type=F32)
          + jnp.dot(lo, wrh_ref[...], preferred_element_type=F32)
          + jnp.dot(hi, wrl_ref[...], preferred_element_type=F32))
    hx_ref[:, :d] = h
    hx_ref[:, d:] = lg + br_ref[...]


def _outproj(merged, w_out, x, mod5, layer, boff, n2, wr_hi, wr_lo, br):
    b, t, d = x.shape
    tm = min(t, 512)
    nt = t // tm
    row = lambda k: pl.BlockSpec((None, None, None, 1, d), lambda bi, i, k=k: (layer, boff + bi, k, 0, 0))
    return pl.pallas_call(
        _outproj_kernel,
        grid=(b, nt),
        in_specs=[
            pl.BlockSpec((tm, d), lambda bi, i: (bi * nt + i, 0)),
            pl.BlockSpec((d, d), lambda bi, i: (0, 0)),
            pl.BlockSpec((None, tm, d), lambda bi, i: (bi, i, 0)),
            row(2),
            pl.BlockSpec((1, d), lambda bi, i: (0, 0)),
            row(4), row(3),
            pl.BlockSpec((d, LANE), lambda bi, i: (0, 0)),
            pl.BlockSpec((d, LANE), lambda bi, i: (0, 0)),
            pl.BlockSpec((1, LANE), lambda bi, i: (0, 0)),
        ],
        out_specs=[
            pl.BlockSpec((None, tm, d), lambda bi, i: (bi, i, 0)),
            pl.BlockSpec((tm, d + LANE), lambda bi, i: (bi * nt + i, 0)),
        ],
        out_shape=[jax.ShapeDtypeStruct((b, t, d), F32),
                   jax.ShapeDtypeStruct((b * t, d + LANE), F32)],
        compiler_params=_cparams(("arbitrary", "arbitrary")),
        name="outproj",
    )(merged, w_out, x, mod5, n2.reshape(1, d), mod5, mod5, wr_hi, wr_lo, br)


def _route_kernel(lg_ref, o_ref):
    lt = lg_ref[...].T
    tm = lt.shape[1]
    e = EXPERTS_PER_GROUP
    rid = lax.broadcasted_iota(I32, (e, tm), 0)
    gl = jnp.where(rid < N_GROUPS, lt[0:e], -jnp.inf)
    gmax = jnp.max(gl, axis=0, keepdims=True)
    ge = jnp.exp(gl - gmax)
    pg = ge / jnp.sum(ge, axis=0, keepdims=True)
    gidx = jnp.min(jnp.where(gl == gmax, rid, e), axis=0, keepdims=True)
    gw = jnp.where(rid == gidx, pg, 0.0)
    parts = []
    for g in range(N_GROUPS):
        el = lt[e * (g + 1):e * (g + 2)]
        m1 = jnp.max(el, axis=0, keepdims=True)
        i1 = jnp.min(jnp.where(el == m1, rid, e), axis=0, keepdims=True)
        el2 = jnp.where(rid == i1, -jnp.inf, el)
        m2 = jnp.max(el2, axis=0, keepdims=True)
        i2 = jnp.min(jnp.where(el2 == m2, rid, e), axis=0, keepdims=True)
        e2 = jnp.exp(m2 - m1)
        p1 = 1.0 / (1.0 + e2)
        p2 = e2 / (1.0 + e2)
        within = jnp.where(rid == i1, p1, 0.0) + jnp.where(rid == i2, p2, 0.0)
        parts.append(within * gw[g:g + 1])
    parts.append(jnp.broadcast_to(gidx.astype(F32), (e, tm)))
    parts.append(jnp.zeros((LANE - (N_GROUPS + 1) * e, tm), F32))
    o_ref[...] = jnp.concatenate(parts, axis=0).T


GROUP_ID_COL = N_GROUPS * EXPERTS_PER_GROUP


def _route(x, col_block):
    n = x.shape[0]
    tm = min(n, 512)
    return pl.pallas_call(
        _route_kernel,
        grid=(n // tm,),
        in_specs=[pl.BlockSpec((tm, LANE), lambda i: (i, col_block))],
        out_specs=pl.BlockSpec((tm, LANE), lambda i: (i, 0)),
        out_shape=jax.ShapeDtypeStruct((n, LANE), F32),
        compiler_params=_cparams(("arbitrary",)),
        name="route",
    )(x)


def _moe_plan(gid, ts):
    n = gid.shape[0]
    ntile = n // ts + N_GROUPS
    onehot = (gid[:, None] == jnp.arange(N_GROUPS, dtype=I32)[None, :]).astype(I32)
    csum = jnp.cumsum(onehot, axis=0)
    rank = jnp.sum((csum - onehot) * onehot, axis=1)
    tiles = (csum[-1] + ts - 1) // ts
    tile_end = jnp.cumsum(tiles)
    pos = jnp.sum(onehot * ((tile_end - tiles) * ts)[None, :], axis=1) + rank
    src = jnp.zeros((ntile * ts,), I32).at[pos].set(jnp.arange(n, dtype=I32))
    tile_gid = jnp.sum((jnp.arange(ntile, dtype=I32)[:, None] >= tile_end[None, :]).astype(I32), axis=1)
    return pos.astype(I32), src, jnp.minimum(tile_gid, N_GROUPS - 1).astype(I32)


def _row_copy(x_hbm, row, buf, r, sem):
    return pltpu.make_async_copy(x_hbm.at[pl.ds(row, 1)], buf.at[pl.ds(r, 1)], sem)


def _row_gather_pipelined(idx_ref, step, nsteps, x_hbm, bufs, sems):
    rows = bufs.shape[1]

    def start_all(s):
        slot = s % 2

        def start(r, carry):
            _row_copy(x_hbm, idx_ref[s * rows + r], bufs.at[slot], r, sems.at[slot]).start()
            return carry
        lax.fori_loop(0, rows, start, 0)

    pl.when(step == 0)(lambda: start_all(step))
    pl.when(step + 1 < nsteps)(lambda: start_all(step + 1))
    slot = step % 2

    def wait(r, carry):
        _row_copy(x_hbm, 0, bufs.at[slot], r, sems.at[slot]).wait()
        return carry
    lax.fori_loop(0, rows, wait, 0)
    return slot


def _dispatch_kernel(src_ref, x_hbm, h_ref, lg_ref, bufs, sems):
    slot = _row_gather_pipelined(src_ref, pl.program_id(0), pl.num_programs(0), x_hbm, bufs, sems)
    d = h_ref.shape[1]
    h_ref[...] = bufs[slot, :, :d].astype(h_ref.dtype)
    lg_ref[...] = bufs[slot, :, d:]


def _dispatch(hx, src, ts):
    npad = src.shape[0]
    d = hx.shape[1] - LANE
    return pl.pallas_call(
        _dispatch_kernel,
        grid_spec=pltpu.PrefetchScalarGridSpec(
            num_scalar_prefetch=1,
            grid=(npad // ts,),
            in_specs=[pl.BlockSpec(memory_space=pl.ANY)],
            out_specs=[pl.BlockSpec((ts, d), lambda i, s: (i, 0)),
                       pl.BlockSpec((ts, LANE), lambda i, s: (i, 0))],
            scratch_shapes=[pltpu.VMEM((2, ts, d + LANE), F32), pltpu.SemaphoreType.DMA((2,))],
        ),
        out_shape=[jax.ShapeDtypeStruct((npad, d), BF16), jax.ShapeDtypeStruct((npad, LANE), F32)],
        compiler_params=_cparams(("arbitrary",)),
        name="moe_dispatch",
    )(src, hx)


def _moe_up_kernel(gid_ref, h_ref, w1_ref, w3_ref, comb_ref, o_ref):
    col = gid_ref[pl.program_id(0)] * EXPERTS_PER_GROUP + pl.program_id(1)
    h = h_ref[...]
    a = jnp.dot(h, w1_ref[...], preferred_element_type=F32)
    u = jnp.dot(h, w3_ref[...], preferred_element_type=F32)
    comb = comb_ref[...]
    lane = lax.broadcasted_iota(I32, comb.shape, 1)
    cw = jnp.sum(jnp.where(lane == col, comb, 0.0), axis=1, keepdims=True)
    o_ref[...] = (a * (1.0 / (1.0 + jnp.exp(-a))) * u * cw).astype(o_ref.dtype)


def _moe_up(hs, w1, w3, comb, tile_gid, ts):
    npad, d = hs.shape
    f = w1.shape[2]
    e = EXPERTS_PER_GROUP
    return pl.pallas_call(
        _moe_up_kernel,
        grid_spec=pltpu.PrefetchScalarGridSpec(
            num_scalar_prefetch=1,
            grid=(npad // ts, e),
            in_specs=[
                pl.BlockSpec((ts, d), lambda i, j, g: (i, 0)),
                pl.BlockSpec((None, d, f), lambda i, j, g: (g[i] * e + j, 0, 0)),
                pl.BlockSpec((None, d, f), lambda i, j, g: (g[i] * e + j, 0, 0)),
                pl.BlockSpec((ts, LANE), lambda i, j, g: (i, 0)),
            ],
            out_specs=pl.BlockSpec((ts, f), lambda i, j, g: (i, j)),
        ),
        out_shape=jax.ShapeDtypeStruct((npad, e * f), BF16),
        compiler_params=_cparams(("arbitrary", "arbitrary")),
        name="moe_up",
    )(tile_gid, hs, w1, w3, comb)


def _moe_down_kernel(gid_ref, a_ref, w_ref, o_ref):
    o_ref[...] = jnp.dot(a_ref[...], w_ref[...], preferred_element_type=F32)


def _moe_down(act, w2, tile_gid, ts):
    npad, kk = act.shape
    d = w2.shape[2]
    return pl.pallas_call(
        _moe_down_kernel,
        grid_spec=pltpu.PrefetchScalarGridSpec(
            num_scalar_prefetch=1,
            grid=(npad // ts,),
            in_specs=[pl.BlockSpec((ts, kk), lambda i, g: (i, 0)),
                      pl.BlockSpec((None, kk, d), lambda i, g: (g[i], 0, 0))],
            out_specs=pl.BlockSpec((ts, d), lambda i, g: (i, 0)),
        ),
        out_shape=jax.ShapeDtypeStruct((npad, d), F32),
        compiler_params=_cparams(("arbitrary",)),
        name="moe_down",
    )(tile_gid, act, w2)


def _combine_kernel(pos_ref, ys_hbm, x_ref, g2_ref, gn_ref, sc_ref, sh_ref, x2_ref, hn_ref, bufs, sems, *, final):
    step = pl.program_id(0) * pl.num_programs(1) + pl.program_id(1)
    slot = _row_gather_pipelined(pos_ref, step, pl.num_programs(0) * pl.num_programs(1), ys_hbm, bufs, sems)
    x2 = x_ref[...] + g2_ref[...] * bufs[slot]
    x2_ref[...] = x2
    y = x2 * lax.rsqrt(jnp.mean(x2 * x2, axis=-1, keepdims=True) + RMS_EPS) * gn_ref[...]
    if not final:
        y = y * (1.0 + sc_ref[...]) + sh_ref[...]
    hn_ref[...] = y.astype(hn_ref.dtype)


def _combine(ys, pos, x1, mod5, layer, boff, g_next, next_layer, final):
    b, t, d = x1.shape
    tm = min(t, 256)
    nt = t // tm
    row = lambda l, k: pl.BlockSpec((None, None, None, 1, d), lambda bi, i, p, l=l, k=k: (l, boff + bi, k, 0, 0))
    return pl.pallas_call(
        functools.partial(_combine_kernel, final=final),
        grid_spec=pltpu.PrefetchScalarGridSpec(
            num_scalar_prefetch=1,
            grid=(b, nt),
            in_specs=[
                pl.BlockSpec(memory_space=pl.ANY),
                pl.BlockSpec((None, tm, d), lambda bi, i, p: (bi, i, 0)),
                row(layer, 5),
                pl.BlockSpec((1, d), lambda bi, i, p: (0, 0)),
                row(next_layer, 1), row(next_layer, 0),
            ],
            out_specs=[
                pl.BlockSpec((None, tm, d), lambda bi, i, p: (bi, i, 0)),
                pl.BlockSpec((None, tm, d), lambda bi, i, p: (bi, i, 0)),
            ],
            scratch_shapes=[pltpu.VMEM((2, tm, d), F32), pltpu.SemaphoreType.DMA((2,))],
        ),
        out_shape=[jax.ShapeDtypeStruct((b, t, d), F32),
                   jax.ShapeDtypeStruct((b, t, d), F32 if final else BF16)],
        compiler_params=_cparams(("arbitrary", "arbitrary")),
        name="moe_combine",
    )(pos, ys, x1, mod5, g_next.reshape(1, d), mod5, mod5)


def _pack_weights(w_in, w_br_diff, w_br_dsa, w_out, w_rg, b_rg, w_re, b_re, w_e1, w_e3, w_e2, d):
    hw = HEAD_W
    o_qd, o_kd, o_vd, o_qs, o_ks, o_vs, o_qi = (k * hw for k in range(7))
    o_ki = 7 * hw
    o_wi = o_ki + IDX_DIM
    o_gt = o_wi + N_IDX_HEADS
    sl = lambda o, n: w_in[:, :, o:o + n]
    wx = jnp.concatenate([sl(o_qd, hw), sl(o_qi, hw), sl(o_qs, hw), sl(o_gt, 2 * d)], axis=-1).astype(BF16)
    wy = jnp.concatenate([sl(o_kd, hw), sl(o_ks, hw), sl(o_vd, hw), sl(o_vs, hw)], axis=-1).astype(BF16)
    depth = w_in.shape[0]
    pad = jnp.zeros((depth, d, 2 * LANE - 2 * IDX_DIM - N_IDX_HEADS), w_in.dtype)
    wz = jnp.concatenate([sl(o_ki, IDX_DIM), sl(o_ki, IDX_DIM), sl(o_wi, N_IDX_HEADS), pad], axis=-1).astype(BF16)
    ne = N_GROUPS * EXPERTS_PER_GROUP
    rpad0 = jnp.zeros((depth, d, EXPERTS_PER_GROUP - N_GROUPS), F32)
    rpad1 = jnp.zeros((depth, d, LANE - EXPERTS_PER_GROUP - ne), F32)
    wr = jnp.concatenate([w_rg, rpad0, w_re, rpad1], axis=-1)
    wr_hi = wr.astype(BF16)
    wr_lo = (wr - wr_hi.astype(F32)).astype(BF16)
    br = jnp.concatenate([b_rg, jnp.zeros((depth, EXPERTS_PER_GROUP - N_GROUPS), F32), b_re,
                          jnp.zeros((depth, LANE - EXPERTS_PER_GROUP - ne), F32)], axis=-1).reshape(depth, 1, LANE)
    f = w_e1.shape[-1]
    return dict(
        wx=wx, wy=wy, wz=wz,
        wa=w_br_diff.astype(BF16), wb=w_br_dsa.astype(BF16), wo=w_out.astype(BF16),
        wr_hi=wr_hi, wr_lo=wr_lo, br=br,
        w1=w_e1.reshape(depth, ne, d, f).astype(BF16), w3=w_e3.reshape(depth, ne, d, f).astype(BF16),
        w2=w_e2.reshape(depth, N_GROUPS, EXPERTS_PER_GROUP * f, d).astype(BF16),
    )


def _trunk(x, pos, boff, mod5, past, pw, norm1, norm2, norm_f, lam_params, subln_g):
    b, t, d = x.shape
    n = b * t
    depth = norm1.shape[0]
    rows = n if t < 1024 else t

    def tables(head_dim, width, active):
        tab = _rope_tables(pos, head_dim, width, active)
        return jnp.tile(tab, (1, rows // t, 1)) if rows != t else tab

    tab64 = tables(DIFF_QK_DIM, LANE, LANE)
    tab128 = tables(DSA_HEAD_DIM, LANE, LANE)
    tabz = tables(IDX_DIM, 2 * LANE, LANE)
    halves = (IDX_DIM // 8, DSA_HEAD_DIM // 8, IDX_DIM // 8)
    tabs = (tab64, tab128, tabz)
    gate_tiles = (2 * d) // HEAD_W
    tq, _, nq, _, _ = _attn_geometry(t, 0)
    ts = 512 if n >= 8192 else 128

    h = _normmod(x, norm1[0], mod5, 0, boff, 1, 0)
    new_rows = ([], [], [], [], [])
    y = None
    for l in range(depth):
        h2d = h.reshape(n, d)
        (qx,) = _proj(h2d, pw["wx"][l], tabs, halves, (0, 0, 1) + (-1,) * gate_tiles, HEAD_W, (BF16,), "proj_q")
        yf, yb = _proj(h2d, pw["wy"][l], tabs, halves, (0, 1, -1, -1), HEAD_W, (F32, BF16), "proj_kv")
        zf, zb = _proj(h2d, pw["wz"][l], tabs, halves, (2,), 2 * LANE, (F32, BF16), "proj_idx")
        for lst, r in zip(new_rows, (yf[:, 0:HEAD_W], yf[:, 2 * HEAD_W:3 * HEAD_W], yf[:, HEAD_W:2 * HEAD_W],
                                     yf[:, 3 * HEAD_W:4 * HEAD_W], zf[:, 0:IDX_DIM])):
            lst.append(r)
        vdt = _transposed_chunks(yb[:, 2 * HEAD_W:3 * HEAD_W], b, nq, tq)
        vst = _transposed_chunks(yb[:, 3 * HEAD_W:4 * HEAD_W], b, nq, tq)
        lam_init = 0.8 - 0.6 * math.exp(-0.3 * l)
        lp = tuple(p[l] for p in lam_params)
        past_d = None if past is None else (past["dk"], past["dv"])
        past_s = None if past is None else (past["sk"], past["sv"], past["ik"])
        od = _diff_attention(qx, yb, vdt, lp, subln_g[l], past_d, l, b, t, lam_init)
        os_ = _dsa_attention(qx, yb, vst, zf, zb, past_s, l, b, t)
        merged = _merge(od, os_, pw["wa"][l], pw["wb"][l], qx, d)
        x1, hx = _outproj(merged, pw["wo"][l], x, mod5, l, boff, norm2[l],
                          pw["wr_hi"][l], pw["wr_lo"][l], pw["br"][l])
        gid = _route(hx, d // LANE)[:, GROUP_ID_COL].astype(I32)
        pos, src, tile_gid = _moe_plan(gid, ts)
        hs, lgs = _dispatch(hx, src, ts)
        act = _moe_up(hs, pw["w1"][l], pw["w3"][l], _route(lgs, 0), tile_gid, ts)
        ys = _moe_down(act, pw["w2"][l], tile_gid, ts)
        final = l == depth - 1
        g_next = norm_f if final else norm1[l + 1]
        x, h = _combine(ys, pos, x1, mod5, l, boff, g_next, 0 if final else l + 1, final)
        if final:
            y = h
    return y, new_rows


def kernel(x_prompt, x_sample, cache_diff_k, cache_diff_v, cache_dsa_k, cache_dsa_v, cache_idx_k, c_prompt, c_sample, norm1, norm2, norm_f, w_ada, b_ada, w_in, lambda_q1, lambda_k1, lambda_q2, lambda_k2, subln_g, w_br_diff, w_br_dsa, w_out, w_router_group, b_router_group, w_router_expert, b_router_expert, w_expert_gate, w_expert_up, w_expert_down):
    bp, tp, d = x_prompt.shape
    bs, ts, _ = x_sample.shape
    depth = norm1.shape[0]
    plen = cache_diff_k.shape[2]
    assert plen % CHUNK == 0 and tp % CHUNK == 0

    pw = _pack_weights(w_in, w_br_diff, w_br_dsa, w_out, w_router_group, b_router_group, w_router_expert,
                       b_router_expert, w_expert_gate, w_expert_up, w_expert_down, d)
    mod = _ada(jnp.concatenate([c_prompt, c_sample], axis=0), w_ada, b_ada)
    mod5 = mod.reshape(depth, bp + bs, 6, 1, d)
    lam_params = (lambda_q1, lambda_k1, lambda_q2, lambda_k2)

    pos_p = jnp.arange(tp, dtype=jnp.int32)
    y_p, rows_p = _trunk(x_prompt, pos_p, 0, mod5, None, pw, norm1, norm2, norm_f, lam_params, subln_g)

    rows2d = lambda c: c.reshape(depth, bs, plen, HEAD_W)
    past = dict(dk=rows2d(cache_diff_k), dv=rows2d(cache_diff_v), sk=rows2d(cache_dsa_k), sv=rows2d(cache_dsa_v),
                ik=jnp.concatenate([cache_idx_k, cache_idx_k], axis=-1).astype(BF16))
    pos_s = plen + jnp.arange(ts, dtype=jnp.int32)
    y_s, rows_s = _trunk(x_sample, pos_s, bp, mod5, past, pw, norm1, norm2, norm_f, lam_params, subln_g)

    def finish(rows, b, t):
        dk, dv, sk, sv, ik = (jnp.stack(r, axis=0) for r in rows)
        return (dk.reshape(depth, b, t, N_DIFF_HEADS, 2, DIFF_QK_DIM), dv.reshape(depth, b, t, N_DIFF_HEADS, DIFF_V_DIM),
                sk.reshape(depth, b, t, N_DSA_HEADS, DSA_HEAD_DIM), sv.reshape(depth, b, t, N_DSA_HEADS, DSA_HEAD_DIM),
                ik.reshape(depth, b, t, IDX_DIM))

    return (y_p, y_s) + finish(rows_p, bp, tp) + finish(rows_s, bs, ts)
```

```python
import functools
import math

import numpy as np
import jax
import jax.numpy as jnp
from jax import lax
from jax.experimental import pallas as pl
from jax.experimental.pallas import tpu as pltpu

F32 = jnp.float32
BF16 = jnp.bfloat16
I32 = jnp.int32

CHUNK = 64
ROPE_THETA = 500000.0
RMS_EPS = 1e-6
N_DIFF_HEADS = 8
DIFF_QK_DIM = 64
DIFF_V_DIM = 128
N_DSA_HEADS = 8
DSA_HEAD_DIM = 128
N_IDX_HEADS = 16
IDX_DIM = 64
TOPK_MAX = 256
N_GROUPS = 4
EXPERTS_PER_GROUP = 8
HEAD_W = 1024
LANE = 128
SUBLANE = 8
NEG = -1e30
VMEM_LIMIT = 56 * 1024 * 1024
HEADS_PER_LOOP = 8
PAST_CHUNK = 512
COUNT_CHAINS = 4
PROJ_SUB = 256

_NEG_INF_BITS = int(np.array(-np.inf, np.float32).view(np.int32))
KEY_NEG_INF = int(np.int32(_NEG_INF_BITS ^ 0x7FFFFFFF))
INT_MIN = -(2 ** 31)


def _cparams(sem):
    return pltpu.CompilerParams(dimension_semantics=sem, vmem_limit_bytes=VMEM_LIMIT)


def _nt_dot(a, b):
    return lax.dot_general(a, b, (((1,), (1,)), ((), ())), preferred_element_type=F32)


def _ada_kernel(c_ref, w_ref, b_ref, o_ref):
    c = c_ref[...]
    a = (c * (1.0 / (1.0 + jnp.exp(-c)))).astype(BF16)
    o_ref[...] = jnp.dot(a, w_ref[...].astype(BF16), preferred_element_type=F32) + b_ref[...]


def _ada(c_all, w_ada, b_ada):
    depth, d, n6 = w_ada.shape
    r = c_all.shape[0]
    tn = 1024
    return pl.pallas_call(
        _ada_kernel,
        grid=(depth, n6 // tn),
        in_specs=[
            pl.BlockSpec((r, d), lambda l, j: (0, 0)),
            pl.BlockSpec((None, d, tn), lambda l, j: (l, 0, j)),
            pl.BlockSpec((None, 1, tn), lambda l, j: (l, 0, j)),
        ],
        out_specs=pl.BlockSpec((None, r, tn), lambda l, j: (l, 0, j)),
        out_shape=jax.ShapeDtypeStruct((depth, r, n6), F32),
        compiler_params=_cparams(("arbitrary", "arbitrary")),
        name="ada",
    )(c_all, w_ada, b_ada.reshape(depth, 1, n6))


def _normmod_kernel(x_ref, g_ref, sc_ref, sh_ref, o_ref):
    x = x_ref[...]
    y = x * lax.rsqrt(jnp.mean(x * x, axis=-1, keepdims=True) + RMS_EPS) * g_ref[...]
    o_ref[...] = (y * (1.0 + sc_ref[...]) + sh_ref[...]).astype(o_ref.dtype)


def _normmod(x, g, mod5, layer, boff, k_scale, k_shift):
    b, t, d = x.shape
    tt = min(t, 512)
    return pl.pallas_call(
        _normmod_kernel,
        grid=(b, t // tt),
        in_specs=[
            pl.BlockSpec((None, tt, d), lambda bi, i: (bi, i, 0)),
            pl.BlockSpec((1, d), lambda bi, i: (0, 0)),
            pl.BlockSpec((None, None, None, 1, d), lambda bi, i: (layer, boff + bi, k_scale, 0, 0)),
            pl.BlockSpec((None, None, None, 1, d), lambda bi, i: (layer, boff + bi, k_shift, 0, 0)),
        ],
        out_specs=pl.BlockSpec((None, tt, d), lambda bi, i: (bi, i, 0)),
        out_shape=jax.ShapeDtypeStruct((b, t, d), BF16),
        compiler_params=_cparams(("arbitrary", "arbitrary")),
        name="normmod",
    )(x, g.reshape(1, d), mod5, mod5)


def _rope_tables(pos, head_dim, width, active, scale=1.0):
    rot = head_dim // 4
    half = rot // 2
    inv = ROPE_THETA ** (-jnp.arange(half, dtype=F32) * (2.0 / rot))
    ang = pos.astype(F32)[:, None] * inv[None, :]
    cos, sin = jnp.cos(ang), jnp.sin(ang)
    col = np.arange(width)
    ch = col % head_dim
    first = (ch < half) & (col < active)
    second = (ch >= half) & (ch < rot) & (col < active)
    idx = np.where(ch < half, ch, np.where(ch < rot, ch - half, 0))
    c = jnp.where((first | second)[None, :], cos[:, idx], 1.0)
    s1 = jnp.where(first[None, :], -sin[:, idx], 0.0)
    s2 = jnp.where(second[None, :], sin[:, idx], 0.0)
    return (jnp.stack([c, s1, s2]) * scale).astype(F32)


def _proj_kernel(kind_ref, half_ref, h_ref, w_ref, tab_ref, *out_refs):
    half = half_ref[kind_ref[pl.program_id(1)]]
    h = h_ref[...]
    tn = w_ref.shape[1]
    tw = tab_ref.shape[2]
    sub = min(tn, PROJ_SUB)
    for s0 in range(0, tn, sub):
        acc = jnp.dot(h, w_ref[:, s0:s0 + sub], preferred_element_type=F32)
        for g0 in range(s0, s0 + sub, LANE):
            xg = acc[:, g0 - s0:g0 - s0 + LANE]
            t0 = g0 % tw
            c = tab_ref[0, :, t0:t0 + LANE]
            s1 = tab_ref[1, :, t0:t0 + LANE]
            s2 = tab_ref[2, :, t0:t0 + LANE]
            og = xg * c + pltpu.roll(xg, LANE - half, 1) * s1 + pltpu.roll(xg, half, 1) * s2
            for o in out_refs:
                o[:, g0:g0 + LANE] = og.astype(o.dtype)


def _proj(h2d, w, tabs, halves, kinds, tn, out_dtypes, name):
    n, d = h2d.shape
    c = w.shape[1]
    tm = min(n, 1024)
    assert c == tn * len(kinds)
    nblk = tabs.shape[2] // tm
    tw = tabs.shape[3]
    return pl.pallas_call(
        _proj_kernel,
        grid_spec=pltpu.PrefetchScalarGridSpec(
            num_scalar_prefetch=2,
            grid=(n // tm, c // tn),
            in_specs=[
                pl.BlockSpec((tm, d), lambda i, j, k, hf: (i, 0)),
                pl.BlockSpec((d, tn), lambda i, j, k, hf: (0, j)),
                pl.BlockSpec((None, 3, tm, tw), lambda i, j, k, hf: (k[j], 0, i % nblk, 0)),
            ],
            out_specs=[pl.BlockSpec((tm, tn), lambda i, j, k, hf: (i, j)) for _ in out_dtypes],
        ),
        out_shape=[jax.ShapeDtypeStruct((n, c), dt) for dt in out_dtypes],
        compiler_params=_cparams(("arbitrary", "arbitrary")),
        name=name,
    )(jnp.asarray(kinds, I32), jnp.asarray(halves, I32), h2d, w, tabs)


def _pad_rows(a, rows):
    if a.shape[0] == rows:
        return a
    return jnp.concatenate([a, jnp.zeros((rows - a.shape[0], a.shape[1]), a.dtype)], axis=0)


def _tn_dot(a, b):
    return lax.dot_general(a, b, (((0,), (0,)), ((), ())), preferred_element_type=F32)


def _softmax_steps(sts, pv_fns, carries):
    m_news = [jnp.maximum(c[0], jnp.max(st, axis=0, keepdims=True)) for st, c in zip(sts, carries)]
    ps = [jnp.exp2(st - m_new) for st, m_new in zip(sts, m_news)]
    pvs = [fn(p.astype(BF16)) for fn, p in zip(pv_fns, ps)]
    out = []
    for (m, l, acc), m_new, p, pv in zip(carries, m_news, ps, pvs):
        alpha = jnp.exp2(m - m_new)
        out.append((m_new, alpha * l + jnp.sum(p, axis=0, keepdims=True), alpha * acc + pv))
    return tuple(out)


def _softmax_init(n):
    return (jnp.full((1, n), NEG, F32), jnp.zeros((1, n), F32), jnp.zeros((LANE, n), F32))


def _attn_geometry(t, plen):
    tq = min(t, 256)
    tqp = max(tq, LANE)
    wp = min(plen, PAST_CHUNK) if plen else 0
    ncp = plen // wp if plen else 0
    assert t % tq == 0 and tq % CHUNK == 0 and (plen == 0 or plen % wp == 0)
    return tq, tqp, t // tq, wp, ncp


def _transposed_chunks(v2d, b, nchunk, w):
    return jnp.swapaxes(v2d.reshape(b, nchunk, w, HEAD_W), 2, 3)


def _carry_load(scs, heads):
    m_sc, l_sc, acc_sc = scs
    return tuple((m_sc[h], l_sc[h], acc_sc[h]) for h in heads)


def _carry_store(scs, heads, carries):
    m_sc, l_sc, acc_sc = scs
    for h, (m, l, acc) in zip(heads, carries):
        m_sc[h] = m
        l_sc[h] = l
        acc_sc[h] = acc


def _diff_kernel(*refs, tq, tqp, ncp, lam_init):
    lq1, lk1, lq2, lk2, g_ref, q_ref, kn_ref, vnt_ref = refs[:8]
    if ncp:
        pk_ref, pv_ref, o_ref = refs[8:11]
        scs = refs[11:]
        c = pl.program_id(2)
    else:
        (o_ref,) = refs[8:]
    i = pl.program_id(1)
    wn = tq
    n2 = 2 * tqp
    lane = lax.broadcasted_iota(I32, (tqp, LANE), 1)
    lam = (jnp.exp(jnp.sum(lq1[...] * lk1[...], axis=1, keepdims=True))
           - jnp.exp(jnp.sum(lq2[...] * lk2[...], axis=1, keepdims=True)) + lam_init)
    krow = lax.broadcasted_iota(I32, (wn, n2), 0)
    qcol = lax.broadcasted_iota(I32, (wn, n2), 1)
    qcol = jnp.where(qcol >= tqp, qcol - tqp, qcol)
    diag_mask = (krow // CHUNK) <= (qcol // CHUNK)

    for h0 in range(0, N_DIFF_HEADS, HEADS_PER_LOOP):
        heads = list(range(h0, h0 + HEADS_PER_LOOP))
        qqs = []
        for h in heads:
            q = _pad_rows(q_ref[:, h * LANE:(h + 1) * LANE], tqp)
            zero = jnp.zeros_like(q)
            qqs.append(jnp.concatenate([jnp.where(lane < DIFF_QK_DIM, q, zero),
                                        jnp.where(lane >= DIFF_QK_DIM, q, zero)], axis=0))

        def step_all(get_k, pv_fn, carries, mask, heads=heads, qqs=qqs):
            sts = [_nt_dot(get_k(h), qq) for h, qq in zip(heads, qqs)]
            if mask is not None:
                sts = [jnp.where(mask, st, NEG) for st in sts]
            return _softmax_steps(sts, [pv_fn(h) for h in heads], carries)

        def new_step(jn, carries, mask, step_all=step_all):
            r0 = pl.multiple_of(jn * wn, wn)
            return step_all(lambda h: kn_ref[pl.ds(r0, wn), h * LANE:(h + 1) * LANE],
                            lambda h: functools.partial(jnp.dot, vnt_ref[jn, h * LANE:(h + 1) * LANE, :],
                                                        preferred_element_type=F32), carries, mask)

        def new_rows_and_finish(carries, heads=heads, new_step=new_step):
            carries = lax.fori_loop(0, i, lambda jn, cs: new_step(jn, cs, None), carries)
            carries = new_step(i, carries, diag_mask)
            for h, (m, l, acc) in zip(heads, carries):
                o = acc / l
                od = o[:, :tqp] - lam * o[:, tqp:]
                od = od * lax.rsqrt(jnp.mean(od * od, axis=0, keepdims=True) + RMS_EPS)
                od = od * g_ref[...] * (1.0 - lam_init)
                o_ref[:, h * LANE:(h + 1) * LANE] = od.T[:tq].astype(o_ref.dtype)

        if ncp:
            @pl.when(c == 0)
            def _(heads=heads):
                _carry_store(scs, heads, tuple(_softmax_init(n2) for _ in heads))

            carries = step_all(lambda h: pk_ref[:, h * LANE:(h + 1) * LANE].astype(BF16),
                               lambda h: functools.partial(_tn_dot, pv_ref[:, h * LANE:(h + 1) * LANE].astype(BF16)),
                               _carry_load(scs, heads), None)
            _carry_store(scs, heads, carries)
            pl.when(c == ncp - 1)(lambda heads=heads, fin=new_rows_and_finish: fin(_carry_load(scs, heads)))
        else:
            new_rows_and_finish(tuple(_softmax_init(n2) for _ in heads))


def _diff_attention(qx, yb, vnt, lam_params, subln, past, layer, b, t, lam_init):
    n = b * t
    plen = 0 if past is None else past[0].shape[2]
    tq, tqp, nq, wp, ncp = _attn_geometry(t, plen)
    vec = lambda a: a.reshape(1, -1).astype(F32)
    small = pl.BlockSpec((1, DIFF_QK_DIM), lambda bi, i, *_: (0, 0))
    in_specs = [small, small, small, small,
                pl.BlockSpec((DIFF_V_DIM, 1), lambda bi, i, *_: (0, 0)),
                pl.BlockSpec((tq, HEAD_W), lambda bi, i, *_: (bi * nq + i, 0)),
                pl.BlockSpec((t, HEAD_W), lambda bi, i, *_: (bi, 0)),
                pl.BlockSpec((None, nq, HEAD_W, tq), lambda bi, i, *_: (bi, 0, 0, 0))]
    args = [vec(p) for p in lam_params] + [subln.reshape(-1, 1).astype(F32), qx, yb, vnt]
    grid, scratch = (b, nq), []
    if past is not None:
        chunk = pl.BlockSpec((None, None, wp, HEAD_W), lambda bi, i, c: (layer, bi, c, 0))
        in_specs += [chunk, chunk]
        args += list(past)
        grid = (b, nq, ncp)
        scratch = [pltpu.VMEM((N_DIFF_HEADS, 1, 2 * tqp), F32), pltpu.VMEM((N_DIFF_HEADS, 1, 2 * tqp), F32),
                   pltpu.VMEM((N_DIFF_HEADS, LANE, 2 * tqp), F32)]
    return pl.pallas_call(
        functools.partial(_diff_kernel, tq=tq, tqp=tqp, ncp=ncp, lam_init=lam_init),
        grid=grid,
        in_specs=in_specs,
        out_specs=pl.BlockSpec((tq, HEAD_W), lambda bi, i, *_: (bi * nq + i, 0)),
        out_shape=jax.ShapeDtypeStruct((n, HEAD_W), BF16),
        scratch_shapes=scratch,
        compiler_params=_cparams(("arbitrary",) * len(grid)),
        name="diff_attn",
    )(*args)


def _dsa_indexer(qi_ref, wq_ref, zk_ref, pki_ref, keyn_sc, biasn_sc, keyp_sc, biasp_sc, *, tq, tqp, wp, ncp, topk):
    i = pl.program_id(1)
    wn = tq
    nnew = i + 1
    kf = float(topk)

    lane = lax.broadcasted_iota(I32, (tqp, LANE), 1)
    wit = _pad_rows(wq_ref[...], tqp).T * (N_IDX_HEADS ** -0.5 * IDX_DIM ** -0.5)
    qqs, wcats = [], []
    for p in range(N_IDX_HEADS // 2):
        qp = _pad_rows(qi_ref[:, p * LANE:(p + 1) * LANE], tqp)
        zero = jnp.zeros_like(qp)
        qqs.append(jnp.concatenate([jnp.where(lane < IDX_DIM, qp, zero),
                                    jnp.where(lane >= IDX_DIM, qp, zero)], axis=0))
        wcats.append(jnp.concatenate([wit[2 * p:2 * p + 1], wit[2 * p + 1:2 * p + 2]], axis=1))

    def scores_t(kdup):
        acc = jnp.zeros((kdup.shape[0], tqp), F32)
        for qq, wc in zip(qqs, wcats):
            r = jnp.maximum(_nt_dot(kdup, qq), 0.0) * wc
            acc = acc + r[:, :tqp] + r[:, tqp:]
        return acc

    def to_key(s, visible):
        s = jnp.where(s == 0.0, 0.0, s)
        if visible is not None:
            s = jnp.where(visible, s, -jnp.inf)
        bits = lax.bitcast_convert_type(s, I32)
        return bits ^ (jnp.right_shift(bits, 31) & 0x7FFFFFFF)

    if ncp:
        def past_keys(c, _):
            r0 = pl.multiple_of(c * wp, wp)
            keyp_sc[c] = to_key(scores_t(pki_ref[pl.ds(r0, wp), :]), None)
            return 0
        lax.fori_loop(0, ncp, past_keys, 0)

    def new_keys(jn, _):
        r0 = pl.multiple_of(jn * wn, wn)
        s = scores_t(zk_ref[pl.ds(r0, wn), :])
        kpos = jn * wn + lax.broadcasted_iota(I32, (wn, tqp), 0)
        qpos = i * tq + lax.broadcasted_iota(I32, (wn, tqp), 1)
        keyn_sc[jn] = to_key(s, (kpos // CHUNK) <= (qpos // CHUNK))
        return 0
    lax.fori_loop(0, nnew, new_keys, 0)

    def count(pred):
        fr = COUNT_CHAINS * SUBLANE

        def fold(kk):
            m = jnp.where(pred(kk), 1.0, 0.0)
            return jnp.sum(m.reshape(kk.shape[0] // fr, fr, tqp), axis=0)
        acc = jnp.zeros((fr, tqp), F32)
        if ncp:
            acc = lax.fori_loop(0, ncp, lambda c, a: a + fold(keyp_sc[c]), acc)
        acc = lax.fori_loop(0, nnew, lambda jn, a: a + fold(keyn_sc[jn]), acc)
        return jnp.sum(acc, axis=0, keepdims=True)

    def pass_body(t, kth):
        cand = kth + jnp.left_shift(jnp.int32(1), 31 - t)
        cnt = count(lambda kk: kk >= cand)
        return jnp.where(cnt >= kf, cand, kth)
    kth = lax.fori_loop(0, 32, pass_body, jnp.full((1, tqp), INT_MIN, I32))

    need = kf - count(lambda kk: kk > kth)

    def lower_tri(w):
        ra = lax.broadcasted_iota(I32, (w, w), 0)
        rb = lax.broadcasted_iota(I32, (w, w), 1)
        return jnp.where(rb <= ra, 1.0, 0.0).astype(BF16)

    def bias_chunk(kk, seen, tril):
        eq = kk == kth
        eqf = jnp.where(eq, 1.0, 0.0)
        rank = seen + jnp.dot(tril, eqf.astype(BF16), preferred_element_type=F32)
        sel = ((kk > kth) | (eq & (rank <= need))) & (kk != KEY_NEG_INF)
        return jnp.where(sel, 0.0, NEG), seen + jnp.sum(eqf, axis=0, keepdims=True)

    seen = jnp.zeros((1, tqp), F32)
    if ncp:
        tril_p = lower_tri(wp)

        def past_bias(c, seen):
            bias, seen = bias_chunk(keyp_sc[c], seen, tril_p)
            biasp_sc[c] = bias
            return seen
        seen = lax.fori_loop(0, ncp, past_bias, seen)
    tril_n = lower_tri(wn)

    def new_bias(jn, seen):
        bias, seen = bias_chunk(keyn_sc[jn], seen, tril_n)
        biasn_sc[jn] = bias
        return seen
    lax.fori_loop(0, nnew, new_bias, seen)

def _dsa_kernel(*refs, tq, tqp, wp, ncp, topk):
    qs_ref, qi_ref, wq_ref, kn_ref, vnt_ref, zk_ref = refs[:6]
    idx = functools.partial(_dsa_indexer, tq=tq, tqp=tqp, wp=wp, ncp=ncp, topk=topk)
    if ncp:
        pk_ref, pv_ref, pki_ref, o_ref, keyn_sc, biasn_sc, keyp_sc, biasp_sc = refs[6:14]
        scs = refs[14:]
        c = pl.program_id(2)
        pl.when(c == 0)(lambda: idx(qi_ref, wq_ref, zk_ref, pki_ref, keyn_sc, biasn_sc, keyp_sc, biasp_sc))
    else:
        o_ref, keyn_sc, biasn_sc = refs[6:]
        idx(qi_ref, wq_ref, zk_ref, None, keyn_sc, biasn_sc, None, None)
    wn = tq
    nnew = pl.program_id(1) + 1

    for h0 in range(0, N_DSA_HEADS, HEADS_PER_LOOP):
        heads = list(range(h0, h0 + HEADS_PER_LOOP))
        qs = [_pad_rows(qs_ref[:, h * LANE:(h + 1) * LANE], tqp) for h in heads]

        def step_all(get_k, pv_fn, bias, carries, heads=heads, qs=qs):
            sts = [_nt_dot(get_k(h), q) + bias for h, q in zip(heads, qs)]
            return _softmax_steps(sts, [pv_fn(h) for h in heads], carries)

        def new_body(jn, carries, step_all=step_all):
            r0 = pl.multiple_of(jn * wn, wn)
            return step_all(lambda h: kn_ref[pl.ds(r0, wn), h * LANE:(h + 1) * LANE],
                            lambda h: functools.partial(jnp.dot, vnt_ref[jn, h * LANE:(h + 1) * LANE, :],
                                                        preferred_element_type=F32), biasn_sc[jn], carries)

        def new_rows_and_finish(carries, heads=heads, new_body=new_body):
            carries = lax.fori_loop(0, nnew, new_body, carries)
            for h, (m, l, acc) in zip(heads, carries):
                o_ref[:, h * LANE:(h + 1) * LANE] = (acc / l).T[:tq].astype(o_ref.dtype)

        if ncp:
            @pl.when(c == 0)
            def _(heads=heads):
                _carry_store(scs, heads, tuple(_softmax_init(tqp) for _ in heads))

            carries = step_all(lambda h: pk_ref[:, h * LANE:(h + 1) * LANE].astype(BF16),
                               lambda h: functools.partial(_tn_dot, pv_ref[:, h * LANE:(h + 1) * LANE].astype(BF16)),
                               biasp_sc[c], _carry_load(scs, heads))
            _carry_store(scs, heads, carries)
            pl.when(c == ncp - 1)(lambda heads=heads, fin=new_rows_and_finish: fin(_carry_load(scs, heads)))
        else:
            new_rows_and_finish(tuple(_softmax_init(tqp) for _ in heads))


def _dsa_attention(qx, yb, vnt, zf, zb, past, layer, b, t):
    n = b * t
    plen = 0 if past is None else past[0].shape[2]
    topk = min(TOPK_MAX, (plen + t) // 4)
    tq, tqp, nq, wp, ncp = _attn_geometry(t, plen)
    in_specs = [
        pl.BlockSpec((tq, HEAD_W), lambda bi, i, *_: (bi * nq + i, 2)),
        pl.BlockSpec((tq, HEAD_W), lambda bi, i, *_: (bi * nq + i, 1)),
        pl.BlockSpec((tq, LANE), lambda bi, i, *_: (bi * nq + i, 1)),
        pl.BlockSpec((t, HEAD_W), lambda bi, i, *_: (bi, 1)),
        pl.BlockSpec((None, nq, HEAD_W, tq), lambda bi, i, *_: (bi, 0, 0, 0)),
        pl.BlockSpec((t, LANE), lambda bi, i, *_: (bi, 0)),
    ]
    args = [qx, qx, zf, yb, vnt, zb]
    scratch = [pltpu.VMEM((nq, tq, tqp), I32), pltpu.VMEM((nq, tq, tqp), F32)]
    grid = (b, nq)
    if past is not None:
        chunk = pl.BlockSpec((None, None, wp, HEAD_W), lambda bi, i, c: (layer, bi, c, 0))
        in_specs += [chunk, chunk, pl.BlockSpec((None, None, plen, LANE), lambda bi, i, c: (layer, bi, 0, 0))]
        args += list(past)
        grid = (b, nq, ncp)
        scratch += [pltpu.VMEM((ncp, wp, tqp), I32), pltpu.VMEM((ncp, wp, tqp), F32),
                    pltpu.VMEM((N_DSA_HEADS, 1, tqp), F32), pltpu.VMEM((N_DSA_HEADS, 1, tqp), F32),
                    pltpu.VMEM((N_DSA_HEADS, LANE, tqp), F32)]
    return pl.pallas_call(
        functools.partial(_dsa_kernel, tq=tq, tqp=tqp, wp=wp, ncp=ncp, topk=topk),
        grid=grid,
        in_specs=in_specs,
        out_specs=pl.BlockSpec((tq, HEAD_W), lambda bi, i, *_: (bi * nq + i, 0)),
        out_shape=jax.ShapeDtypeStruct((n, HEAD_W), BF16),
        scratch_shapes=scratch,
        compiler_params=_cparams(("arbitrary",) * len(grid)),
        name="dsa_attn",
    )(*args)


def _merge_kernel(od_ref, os_ref, wa_ref, wb_ref, ga_ref, gb_ref, o_ref):
    a = jnp.dot(od_ref[...], wa_ref[...], preferred_element_type=F32)
    bq = jnp.dot(os_ref[...], wb_ref[...], preferred_element_type=F32)
    sig = lambda z: 1.0 / (1.0 + jnp.exp(-z.astype(F32)))
    o_ref[...] = (sig(ga_ref[...]) * a + sig(gb_ref[...]) * bq).astype(o_ref.dtype)


def _merge(od, os_, wa, wb, qx, d):
    n = od.shape[0]
    tm = min(n, 1024)
    tn = min(d, 1024)
    gate0 = (3 * HEAD_W) // tn
    nd = d // tn
    return pl.pallas_call(
        _merge_kernel,
        grid=(n // tm, nd),
        in_specs=[
            pl.BlockSpec((tm, HEAD_W), lambda i, j: (i, 0)),
            pl.BlockSpec((tm, HEAD_W), lambda i, j: (i, 0)),
            pl.BlockSpec((HEAD_W, tn), lambda i, j: (0, j)),
            pl.BlockSpec((HEAD_W, tn), lambda i, j: (0, j)),
            pl.BlockSpec((tm, tn), lambda i, j: (i, gate0 + j)),
            pl.BlockSpec((tm, tn), lambda i, j: (i, gate0 + nd + j)),
        ],
        out_specs=pl.BlockSpec((tm, tn), lambda i, j: (i, j)),
        out_shape=jax.ShapeDtypeStruct((n, d), BF16),
        compiler_params=_cparams(("arbitrary", "arbitrary")),
        name="merge",
    )(od, os_, wa, wb, qx, qx)


def _outproj_kernel(m_ref, w_ref, x_ref, g1_ref, n2_ref, sc_ref, sh_ref, wrh_ref, wrl_ref, br_ref,
                    x1_ref, hx_ref):
    d = x_ref.shape[-1]
    y = jnp.dot(m_ref[...], w_ref[...], preferred_element_type=F32)
    x1 = x_ref[...] + g1_ref[...] * y
    x1_ref[...] = x1
    h = x1 * lax.rsqrt(jnp.mean(x1 * x1, axis=-1, keepdims=True) + RMS_EPS) * n2_ref[...]
    h = h * (1.0 + sc_ref[...]) + sh_ref[...]
    hi = h.astype(BF16)
    lo = (h - hi.astype(F32)).astype(BF16)
    lg = (jnp.dot(hi, wrh_ref[...], preferred_element_type=F32)
          + jnp.dot(lo, wrh_ref[...], preferred_element_type=F32)
          + jnp.dot(hi, wrl_ref[...], preferred_element_type=F32))
    hx_ref[:, :d] = h
    hx_ref[:, d:] = lg + br_ref[...]


def _outproj(merged, w_out, x, mod5, layer, boff, n2, wr_hi, wr_lo, br):
    b, t, d = x.shape
    tm = min(t, 512)
    nt = t // tm
    row = lambda k: pl.BlockSpec((None, None, None, 1, d), lambda bi, i, k=k: (layer, boff + bi, k, 0, 0))
    return pl.pallas_call(
        _outproj_kernel,
        grid=(b, nt),
        in_specs=[
            pl.BlockSpec((tm, d), lambda bi, i: (bi * nt + i, 0)),
            pl.BlockSpec((d, d), lambda bi, i: (0, 0)),
            pl.BlockSpec((None, tm, d), lambda bi, i: (bi, i, 0)),
            row(2),
            pl.BlockSpec((1, d), lambda bi, i: (0, 0)),
            row(4), row(3),
            pl.BlockSpec((d, LANE), lambda bi, i: (0, 0)),
            pl.BlockSpec((d, LANE), lambda bi, i: (0, 0)),
            pl.BlockSpec((1, LANE), lambda bi, i: (0, 0)),
        ],
        out_specs=[
            pl.BlockSpec((None, tm, d), lambda bi, i: (bi, i, 0)),
            pl.BlockSpec((tm, d + LANE), lambda bi, i: (bi * nt + i, 0)),
        ],
        out_shape=[jax.ShapeDtypeStruct((b, t, d), F32),
                   jax.ShapeDtypeStruct((b * t, d + LANE), F32)],
        compiler_params=_cparams(("arbitrary", "arbitrary")),
        name="outproj",
    )(merged, w_out, x, mod5, n2.reshape(1, d), mod5, mod5, wr_hi, wr_lo, br)


def _route_kernel(lg_ref, o_ref):
    lt = lg_ref[...].T
    tm = lt.shape[1]
    e = EXPERTS_PER_GROUP
    rid = lax.broadcasted_iota(I32, (e, tm), 0)
    gl = jnp.where(rid < N_GROUPS, lt[0:e], -jnp.inf)
    gmax = jnp.max(gl, axis=0, keepdims=True)
    ge = jnp.exp(gl - gmax)
    pg = ge / jnp.sum(ge, axis=0, keepdims=True)
    gidx = jnp.min(jnp.where(gl == gmax, rid, e), axis=0, keepdims=True)
    gw = jnp.where(rid == gidx, pg, 0.0)
    parts = []
    for g in range(N_GROUPS):
        el = lt[e * (g + 1):e * (g + 2)]
        m1 = jnp.max(el, axis=0, keepdims=True)
        i1 = jnp.min(jnp.where(el == m1, rid, e), axis=0, keepdims=True)
        el2 = jnp.where(rid == i1, -jnp.inf, el)
        m2 = jnp.max(el2, axis=0, keepdims=True)
        i2 = jnp.min(jnp.where(el2 == m2, rid, e), axis=0, keepdims=True)
        e2 = jnp.exp(m2 - m1)
        p1 = 1.0 / (1.0 + e2)
        p2 = e2 / (1.0 + e2)
        within = jnp.where(rid == i1, p1, 0.0) + jnp.where(rid == i2, p2, 0.0)
        parts.append(within * gw[g:g + 1])
    parts.append(jnp.broadcast_to(gidx.astype(F32), (e, tm)))
    parts.append(jnp.zeros((LANE - (N_GROUPS + 1) * e, tm), F32))
    o_ref[...] = jnp.concatenate(parts, axis=0).T


GROUP_ID_COL = N_GROUPS * EXPERTS_PER_GROUP


def _route(x, col_block):
    n = x.shape[0]
    tm = min(n, 512)
    return pl.pallas_call(
        _route_kernel,
        grid=(n // tm,),
        in_specs=[pl.BlockSpec((tm, LANE), lambda i: (i, col_block))],
        out_specs=pl.BlockSpec((tm, LANE), lambda i: (i, 0)),
        out_shape=jax.ShapeDtypeStruct((n, LANE), F32),
        compiler_params=_cparams(("arbitrary",)),
        name="route",
    )(x)


def _moe_plan(gid, ts):
    n = gid.shape[0]
    ntile = n // ts + N_GROUPS
    onehot = (gid[:, None] == jnp.arange(N_GROUPS, dtype=I32)[None, :]).astype(I32)
    csum = jnp.cumsum(onehot, axis=0)
    rank = jnp.sum((csum - onehot) * onehot, axis=1)
    tiles = (csum[-1] + ts - 1) // ts
    tile_end = jnp.cumsum(tiles)
    pos = jnp.sum(onehot * ((tile_end - tiles) * ts)[None, :], axis=1) + rank
    src = jnp.zeros((ntile * ts,), I32).at[pos].set(jnp.arange(n, dtype=I32))
    tile_gid = jnp.sum((jnp.arange(ntile, dtype=I32)[:, None] >= tile_end[None, :]).astype(I32), axis=1)
    return pos.astype(I32), src, jnp.minimum(tile_gid, N_GROUPS - 1).astype(I32)


def _row_copy(x_hbm, row, buf, r, sem):
    return pltpu.make_async_copy(x_hbm.at[pl.ds(row, 1)], buf.at[pl.ds(r, 1)], sem)


def _row_gather_pipelined(idx_ref, step, nsteps, x_hbm, bufs, sems):
    rows = bufs.shape[1]

    def start_all(s):
        slot = s % 2

        def start(k, carry):
            r0 = pl.multiple_of(k * SUBLANE, SUBLANE)
            for j in range(SUBLANE):
                _row_copy(x_hbm, idx_ref[s * rows + r0 + j], bufs.at[slot], r0 + j, sems.at[slot]).start()
            return carry
        lax.fori_loop(0, rows // SUBLANE, start, 0)

    pl.when(step == 0)(lambda: start_all(step))
    pl.when(step + 1 < nsteps)(lambda: start_all(step + 1))
    slot = step % 2

    def wait(k, carry):
        r0 = pl.multiple_of(k * SUBLANE, SUBLANE)
        for j in range(SUBLANE):
            _row_copy(x_hbm, 0, bufs.at[slot], r0 + j, sems.at[slot]).wait()
        return carry
    lax.fori_loop(0, rows // SUBLANE, wait, 0)
    return slot


def _dispatch_kernel(src_ref, x_hbm, h_ref, lg_ref, bufs, sems):
    slot = _row_gather_pipelined(src_ref, pl.program_id(0), pl.num_programs(0), x_hbm, bufs, sems)
    d = h_ref.shape[1]
    h_ref[...] = bufs[slot, :, :d].astype(h_ref.dtype)
    lg_ref[...] = bufs[slot, :, d:]


def _dispatch(hx, src, ts):
    npad = src.shape[0]
    d = hx.shape[1] - LANE
    return pl.pallas_call(
        _dispatch_kernel,
        grid_spec=pltpu.PrefetchScalarGridSpec(
            num_scalar_prefetch=1,
            grid=(npad // ts,),
            in_specs=[pl.BlockSpec(memory_space=pl.ANY)],
            out_specs=[pl.BlockSpec((ts, d), lambda i, s: (i, 0)),
                       pl.BlockSpec((ts, LANE), lambda i, s: (i, 0))],
            scratch_shapes=[pltpu.VMEM((2, ts, d + LANE), F32), pltpu.SemaphoreType.DMA((2,))],
        ),
        out_shape=[jax.ShapeDtypeStruct((npad, d), BF16), jax.ShapeDtypeStruct((npad, LANE), F32)],
        compiler_params=_cparams(("arbitrary",)),
        name="moe_dispatch",
    )(src, hx)


def _moe_up_kernel(gid_ref, h_ref, w1_ref, w3_ref, comb_ref, o_ref):
    col = gid_ref[pl.program_id(0)] * EXPERTS_PER_GROUP + pl.program_id(1)
    h = h_ref[...]
    a = jnp.dot(h, w1_ref[...], preferred_element_type=F32)
    u = jnp.dot(h, w3_ref[...], preferred_element_type=F32)
    comb = comb_ref[...]
    lane = lax.broadcasted_iota(I32, comb.shape, 1)
    cw = jnp.sum(jnp.where(lane == col, comb, 0.0), axis=1, keepdims=True)
    o_ref[...] = (a * (1.0 / (1.0 + jnp.exp(-a))) * u * cw).astype(o_ref.dtype)


def _moe_up(hs, w1, w3, comb, tile_gid, ts):
    npad, d = hs.shape
    f = w1.shape[2]
    e = EXPERTS_PER_GROUP
    return pl.pallas_call(
        _moe_up_kernel,
        grid_spec=pltpu.PrefetchScalarGridSpec(
            num_scalar_prefetch=1,
            grid=(npad // ts, e),
            in_specs=[
                pl.BlockSpec((ts, d), lambda i, j, g: (i, 0)),
                pl.BlockSpec((None, d, f), lambda i, j, g: (g[i] * e + j, 0, 0)),
                pl.BlockSpec((None, d, f), lambda i, j, g: (g[i] * e + j, 0, 0)),
                pl.BlockSpec((ts, LANE), lambda i, j, g: (i, 0)),
            ],
            out_specs=pl.BlockSpec((ts, f), lambda i, j, g: (i, j)),
        ),
        out_shape=jax.ShapeDtypeStruct((npad, e * f), BF16),
        compiler_params=_cparams(("arbitrary", "arbitrary")),
        name="moe_up",
    )(tile_gid, hs, w1, w3, comb)


def _moe_down_kernel(gid_ref, a_ref, w_ref, o_ref):
    o_ref[...] = jnp.dot(a_ref[...], w_ref[...], preferred_element_type=F32)


def _moe_down(act, w2, tile_gid, ts):
    npad, kk = act.shape
    d = w2.shape[2]
    return pl.pallas_call(
        _moe_down_kernel,
        grid_spec=pltpu.PrefetchScalarGridSpec(
            num_scalar_prefetch=1,
            grid=(npad // ts,),
            in_specs=[pl.BlockSpec((ts, kk), lambda i, g: (i, 0)),
                      pl.BlockSpec((None, kk, d), lambda i, g: (g[i], 0, 0))],
            out_specs=pl.BlockSpec((ts, d), lambda i, g: (i, 0)),
        ),
        out_shape=jax.ShapeDtypeStruct((npad, d), F32),
        compiler_params=_cparams(("arbitrary",)),
        name="moe_down",
    )(tile_gid, act, w2)


def _combine_kernel(pos_ref, ys_hbm, x_ref, g2_ref, gn_ref, sc_ref, sh_ref, x2_ref, hn_ref, bufs, sems, *, final):
    step = pl.program_id(0) * pl.num_programs(1) + pl.program_id(1)
    slot = _row_gather_pipelined(pos_ref, step, pl.num_programs(0) * pl.num_programs(1), ys_hbm, bufs, sems)
    x2 = x_ref[...] + g2_ref[...] * bufs[slot]
    x2_ref[...] = x2
    y = x2 * lax.rsqrt(jnp.mean(x2 * x2, axis=-1, keepdims=True) + RMS_EPS) * gn_ref[...]
    if not final:
        y = y * (1.0 + sc_ref[...]) + sh_ref[...]
    hn_ref[...] = y.astype(hn_ref.dtype)


def _combine(ys, pos, x1, mod5, layer, boff, g_next, next_layer, final):
    b, t, d = x1.shape
    tm = min(t, 256)
    nt = t // tm
    row = lambda l, k: pl.BlockSpec((None, None, None, 1, d), lambda bi, i, p, l=l, k=k: (l, boff + bi, k, 0, 0))
    return pl.pallas_call(
        functools.partial(_combine_kernel, final=final),
        grid_spec=pltpu.PrefetchScalarGridSpec(
            num_scalar_prefetch=1,
            grid=(b, nt),
            in_specs=[
                pl.BlockSpec(memory_space=pl.ANY),
                pl.BlockSpec((None, tm, d), lambda bi, i, p: (bi, i, 0)),
                row(layer, 5),
                pl.BlockSpec((1, d), lambda bi, i, p: (0, 0)),
                row(next_layer, 1), row(next_layer, 0),
            ],
            out_specs=[
                pl.BlockSpec((None, tm, d), lambda bi, i, p: (bi, i, 0)),
                pl.BlockSpec((None, tm, d), lambda bi, i, p: (bi, i, 0)),
            ],
            scratch_shapes=[pltpu.VMEM((2, tm, d), F32), pltpu.SemaphoreType.DMA((2,))],
        ),
        out_shape=[jax.ShapeDtypeStruct((b, t, d), F32),
                   jax.ShapeDtypeStruct((b, t, d), F32 if final else BF16)],
        compiler_params=_cparams(("arbitrary", "arbitrary")),
        name="moe_combine",
    )(pos, ys, x1, mod5, g_next.reshape(1, d), mod5, mod5)


def _pack_weights(w_in, w_br_diff, w_br_dsa, w_out, w_rg, b_rg, w_re, b_re, w_e1, w_e3, w_e2, d):
    hw = HEAD_W
    o_qd, o_kd, o_vd, o_qs, o_ks, o_vs, o_qi = (k * hw for k in range(7))
    o_ki = 7 * hw
    o_wi = o_ki + IDX_DIM
    o_gt = o_wi + N_IDX_HEADS
    sl = lambda o, n: w_in[:, :, o:o + n]
    wx = jnp.concatenate([sl(o_qd, hw), sl(o_qi, hw), sl(o_qs, hw), sl(o_gt, 2 * d)], axis=-1).astype(BF16)
    wy = jnp.concatenate([sl(o_kd, hw), sl(o_ks, hw), sl(o_vd, hw), sl(o_vs, hw)], axis=-1).astype(BF16)
    depth = w_in.shape[0]
    pad = jnp.zeros((depth, d, 2 * LANE - 2 * IDX_DIM - N_IDX_HEADS), w_in.dtype)
    wz = jnp.concatenate([sl(o_ki, IDX_DIM), sl(o_ki, IDX_DIM), sl(o_wi, N_IDX_HEADS), pad], axis=-1).astype(BF16)
    ne = N_GROUPS * EXPERTS_PER_GROUP
    rpad0 = jnp.zeros((depth, d, EXPERTS_PER_GROUP - N_GROUPS), F32)
    rpad1 = jnp.zeros((depth, d, LANE - EXPERTS_PER_GROUP - ne), F32)
    wr = jnp.concatenate([w_rg, rpad0, w_re, rpad1], axis=-1)
    wr_hi = wr.astype(BF16)
    wr_lo = (wr - wr_hi.astype(F32)).astype(BF16)
    br = jnp.concatenate([b_rg, jnp.zeros((depth, EXPERTS_PER_GROUP - N_GROUPS), F32), b_re,
                          jnp.zeros((depth, LANE - EXPERTS_PER_GROUP - ne), F32)], axis=-1).reshape(depth, 1, LANE)
    f = w_e1.shape[-1]
    return dict(
        wx=wx, wy=wy, wz=wz,
        wa=w_br_diff.astype(BF16), wb=w_br_dsa.astype(BF16), wo=w_out.astype(BF16),
        wr_hi=wr_hi, wr_lo=wr_lo, br=br,
        w1=w_e1.reshape(depth, ne, d, f).astype(BF16), w3=w_e3.reshape(depth, ne, d, f).astype(BF16),
        w2=w_e2.reshape(depth, N_GROUPS, EXPERTS_PER_GROUP * f, d).astype(BF16),
    )


def _trunk(x, pos, boff, mod5, past, pw, norm1, norm2, norm_f, lam_params, subln_g):
    b, t, d = x.shape
    n = b * t
    depth = norm1.shape[0]
    rows = n if t < 1024 else t

    def tables(head_dim, width, active, scale=1.0):
        tab = _rope_tables(pos, head_dim, width, active, scale)
        return jnp.tile(tab, (1, rows // t, 1)) if rows != t else tab

    log2e = math.log2(math.e)
    tabs = jnp.stack([tables(DIFF_QK_DIM, LANE, LANE), tables(DSA_HEAD_DIM, LANE, LANE),
                      tables(DIFF_QK_DIM, LANE, 0),
                      tables(DIFF_QK_DIM, LANE, LANE, DIFF_QK_DIM ** -0.5 * log2e),
                      tables(DSA_HEAD_DIM, LANE, LANE, DSA_HEAD_DIM ** -0.5 * log2e)])
    halves = (DIFF_QK_DIM // 8, DSA_HEAD_DIM // 8, DIFF_QK_DIM // 8, DIFF_QK_DIM // 8, DSA_HEAD_DIM // 8)
    R64, R128, PLAIN, R64_Q, R128_Q = range(5)
    tabz = tables(IDX_DIM, 2 * LANE, LANE)[None]
    gate_tiles = (2 * d) // HEAD_W
    tq, _, nq, _, _ = _attn_geometry(t, 0)
    ts = 512 if n >= 8192 else 128

    h = _normmod(x, norm1[0], mod5, 0, boff, 1, 0)
    new_rows = ([], [], [], [], [])
    y = None
    for l in range(depth):
        h2d = h.reshape(n, d)
        (qx,) = _proj(h2d, pw["wx"][l], tabs, halves, (R64_Q, R64, R128_Q) + (PLAIN,) * gate_tiles, HEAD_W,
                      (BF16,), "proj_q")
        yf, yb = _proj(h2d, pw["wy"][l], tabs, halves, (R64, R128, PLAIN, PLAIN), HEAD_W, (F32, BF16), "proj_kv")
        zf, zb = _proj(h2d, pw["wz"][l], tabz, (IDX_DIM // 8,), (0,), 2 * LANE, (F32, BF16), "proj_idx")
        for lst, r in zip(new_rows, (yf[:, 0:HEAD_W], yf[:, 2 * HEAD_W:3 * HEAD_W], yf[:, HEAD_W:2 * HEAD_W],
                                     yf[:, 3 * HEAD_W:4 * HEAD_W], zf[:, 0:IDX_DIM])):
            lst.append(r)
        vdt = _transposed_chunks(yb[:, 2 * HEAD_W:3 * HEAD_W], b, nq, tq)
        vst = _transposed_chunks(yb[:, 3 * HEAD_W:4 * HEAD_W], b, nq, tq)
        lam_init = 0.8 - 0.6 * math.exp(-0.3 * l)
        lp = tuple(p[l] for p in lam_params)
        past_d = None if past is None else (past["dk"], past["dv"])
        past_s = None if past is None else (past["sk"], past["sv"], past["ik"])
        od = _diff_attention(qx, yb, vdt, lp, subln_g[l], past_d, l, b, t, lam_init)
        os_ = _dsa_attention(qx, yb, vst, zf, zb, past_s, l, b, t)
        merged = _merge(od, os_, pw["wa"][l], pw["wb"][l], qx, d)
        x1, hx = _outproj(merged, pw["wo"][l], x, mod5, l, boff, norm2[l],
                          pw["wr_hi"][l], pw["wr_lo"][l], pw["br"][l])
        gid = _route(hx, d // LANE)[:, GROUP_ID_COL].astype(I32)
        pos, src, tile_gid = _moe_plan(gid, ts)
        hs, lgs = _dispatch(hx, src, ts)
        act = _moe_up(hs, pw["w1"][l], pw["w3"][l], _route(lgs, 0), tile_gid, ts)
        ys = _moe_down(act, pw["w2"][l], tile_gid, ts)
        final = l == depth - 1
        g_next = norm_f if final else norm1[l + 1]
        x, h = _combine(ys, pos, x1, mod5, l, boff, g_next, 0 if final else l + 1, final)
        if final:
            y = h
    return y, new_rows


def kernel(x_prompt, x_sample, cache_diff_k, cache_diff_v, cache_dsa_k, cache_dsa_v, cache_idx_k, c_prompt, c_sample, norm1, norm2, norm_f, w_ada, b_ada, w_in, lambda_q1, lambda_k1, lambda_q2, lambda_k2, subln_g, w_br_diff, w_br_dsa, w_out, w_router_group, b_router_group, w_router_expert, b_router_expert, w_expert_gate, w_expert_up, w_expert_down):
    bp, tp, d = x_prompt.shape
    bs, ts, _ = x_sample.shape
    depth = norm1.shape[0]
    plen = cache_diff_k.shape[2]
    assert plen % CHUNK == 0 and tp % CHUNK == 0

    pw = _pack_weights(w_in, w_br_diff, w_br_dsa, w_out, w_router_group, b_router_group, w_router_expert,
                       b_router_expert, w_expert_gate, w_expert_up, w_expert_down, d)
    mod = _ada(jnp.concatenate([c_prompt, c_sample], axis=0), w_ada, b_ada)
    mod5 = mod.reshape(depth, bp + bs, 6, 1, d)
    lam_params = (lambda_q1, lambda_k1, lambda_q2, lambda_k2)

    pos_p = jnp.arange(tp, dtype=jnp.int32)
    y_p, rows_p = _trunk(x_prompt, pos_p, 0, mod5, None, pw, norm1, norm2, norm_f, lam_params, subln_g)

    rows2d = lambda c: c.reshape(depth, bs, plen, HEAD_W)
    past = dict(dk=rows2d(cache_diff_k), dv=rows2d(cache_diff_v), sk=rows2d(cache_dsa_k), sv=rows2d(cache_dsa_v),
                ik=jnp.concatenate([cache_idx_k, cache_idx_k], axis=-1).astype(BF16))
    pos_s = plen + jnp.arange(ts, dtype=jnp.int32)
    y_s, rows_s = _trunk(x_sample, pos_s, bp, mod5, past, pw, norm1, norm2, norm_f, lam_params, subln_g)

    def finish(rows, b, t):
        dk, dv, sk, sv, ik = (jnp.stack(r, axis=0) for r in rows)
        return (dk.reshape(depth, b, t, N_DIFF_HEADS, 2, DIFF_QK_DIM), dv.reshape(depth, b, t, N_DIFF_HEADS, DIFF_V_DIM),
                sk.reshape(depth, b, t, N_DSA_HEADS, DSA_HEAD_DIM), sv.reshape(depth, b, t, N_DSA_HEADS, DSA_HEAD_DIM),
                ik.reshape(depth, b, t, IDX_DIM))

    return (y_p, y_s) + finish(rows_p, bp, tp) + finish(rows_s, bs, ts)
```

```python
import functools
import math

import numpy as np
import jax
import jax.numpy as jnp
from jax import lax
from jax.experimental import pallas as pl
from jax.experimental.pallas import tpu as pltpu

F32 = jnp.float32
BF16 = jnp.bfloat16
I32 = jnp.int32

CHUNK = 64
ROPE_THETA = 500000.0
RMS_EPS = 1e-6
N_DIFF_HEADS = 8
DIFF_QK_DIM = 64
DIFF_V_DIM = 128
N_DSA_HEADS = 8
DSA_HEAD_DIM = 128
N_IDX_HEADS = 16
IDX_DIM = 64
TOPK_MAX = 256
N_GROUPS = 4
EXPERTS_PER_GROUP = 8
HEAD_W = 1024
LANE = 128
SUBLANE = 8
NEG = -1e30
VMEM_LIMIT = 56 * 1024 * 1024
HEADS_PER_LOOP = 8
PAST_CHUNK = 512
COUNT_CHAINS = 4
PROJ_SUB = 256

_NEG_INF_BITS = int(np.array(-np.inf, np.float32).view(np.int32))
KEY_NEG_INF = int(np.int32(_NEG_INF_BITS ^ 0x7FFFFFFF))
INT_MIN = -(2 ** 31)


def _cparams(sem):
    return pltpu.CompilerParams(dimension_semantics=sem, vmem_limit_bytes=VMEM_LIMIT)


def _nt_dot(a, b):
    return lax.dot_general(a, b, (((1,), (1,)), ((), ())), preferred_element_type=F32)


def _ada_kernel(c_ref, w_ref, b_ref, o_ref):
    c = c_ref[...]
    a = (c * (1.0 / (1.0 + jnp.exp(-c)))).astype(BF16)
    o_ref[...] = jnp.dot(a, w_ref[...].astype(BF16), preferred_element_type=F32) + b_ref[...]


def _ada(c_all, w_ada, b_ada):
    depth, d, n6 = w_ada.shape
    r = c_all.shape[0]
    tn = 1024
    return pl.pallas_call(
        _ada_kernel,
        grid=(depth, n6 // tn),
        in_specs=[
            pl.BlockSpec((r, d), lambda l, j: (0, 0)),
            pl.BlockSpec((None, d, tn), lambda l, j: (l, 0, j)),
            pl.BlockSpec((None, 1, tn), lambda l, j: (l, 0, j)),
        ],
        out_specs=pl.BlockSpec((None, r, tn), lambda l, j: (l, 0, j)),
        out_shape=jax.ShapeDtypeStruct((depth, r, n6), F32),
        compiler_params=_cparams(("arbitrary", "arbitrary")),
        name="ada",
    )(c_all, w_ada, b_ada.reshape(depth, 1, n6))


def _normmod_kernel(x_ref, g_ref, sc_ref, sh_ref, o_ref):
    x = x_ref[...]
    y = x * lax.rsqrt(jnp.mean(x * x, axis=-1, keepdims=True) + RMS_EPS) * g_ref[...]
    o_ref[...] = (y * (1.0 + sc_ref[...]) + sh_ref[...]).astype(o_ref.dtype)


def _normmod(x, g, mod5, layer, boff, k_scale, k_shift):
    b, t, d = x.shape
    tt = min(t, 512)
    return pl.pallas_call(
        _normmod_kernel,
        grid=(b, t // tt),
        in_specs=[
            pl.BlockSpec((None, tt, d), lambda bi, i: (bi, i, 0)),
            pl.BlockSpec((1, d), lambda bi, i: (0, 0)),
            pl.BlockSpec((None, None, None, 1, d), lambda bi, i: (layer, boff + bi, k_scale, 0, 0)),
            pl.BlockSpec((None, None, None, 1, d), lambda bi, i: (layer, boff + bi, k_shift, 0, 0)),
        ],
        out_specs=pl.BlockSpec((None, tt, d), lambda bi, i: (bi, i, 0)),
        out_shape=jax.ShapeDtypeStruct((b, t, d), BF16),
        compiler_params=_cparams(("arbitrary", "arbitrary")),
        name="normmod",
    )(x, g.reshape(1, d), mod5, mod5)


def _rope_tables(pos, head_dim, width, active, scale=1.0):
    rot = head_dim // 4
    half = rot // 2
    inv = ROPE_THETA ** (-jnp.arange(half, dtype=F32) * (2.0 / rot))
    ang = pos.astype(F32)[:, None] * inv[None, :]
    cos, sin = jnp.cos(ang), jnp.sin(ang)
    col = np.arange(width)
    ch = col % head_dim
    first = (ch < half) & (col < active)
    second = (ch >= half) & (ch < rot) & (col < active)
    idx = np.where(ch < half, ch, np.where(ch < rot, ch - half, 0))
    c = jnp.where((first | second)[None, :], cos[:, idx], 1.0)
    s1 = jnp.where(first[None, :], -sin[:, idx], 0.0)
    s2 = jnp.where(second[None, :], sin[:, idx], 0.0)
    return (jnp.stack([c, s1, s2]) * scale).astype(F32)


def _proj_kernel(kind_ref, half_ref, h_ref, w_ref, tab_ref, *out_refs):
    half = half_ref[kind_ref[pl.program_id(1)]]
    h = h_ref[...]
    tn = w_ref.shape[1]
    tw = tab_ref.shape[2]
    sub = min(tn, PROJ_SUB)
    for s0 in range(0, tn, sub):
        acc = jnp.dot(h, w_ref[:, s0:s0 + sub], preferred_element_type=F32)
        for g0 in range(s0, s0 + sub, LANE):
            xg = acc[:, g0 - s0:g0 - s0 + LANE]
            t0 = g0 % tw
            c = tab_ref[0, :, t0:t0 + LANE]
            s1 = tab_ref[1, :, t0:t0 + LANE]
            s2 = tab_ref[2, :, t0:t0 + LANE]
            og = xg * c + pltpu.roll(xg, LANE - half, 1) * s1 + pltpu.roll(xg, half, 1) * s2
            for o in out_refs:
                o[:, g0:g0 + LANE] = og.astype(o.dtype)


def _proj(h2d, w, tabs, halves, kinds, tn, out_dtypes, name):
    n, d = h2d.shape
    c = w.shape[1]
    tm = min(n, 1024)
    assert c == tn * len(kinds)
    nblk = tabs.shape[2] // tm
    tw = tabs.shape[3]
    return pl.pallas_call(
        _proj_kernel,
        grid_spec=pltpu.PrefetchScalarGridSpec(
            num_scalar_prefetch=2,
            grid=(n // tm, c // tn),
            in_specs=[
                pl.BlockSpec((tm, d), lambda i, j, k, hf: (i, 0)),
                pl.BlockSpec((d, tn), lambda i, j, k, hf: (0, j)),
                pl.BlockSpec((None, 3, tm, tw), lambda i, j, k, hf: (k[j], 0, i % nblk, 0)),
            ],
            out_specs=[pl.BlockSpec((tm, tn), lambda i, j, k, hf: (i, j)) for _ in out_dtypes],
        ),
        out_shape=[jax.ShapeDtypeStruct((n, c), dt) for dt in out_dtypes],
        compiler_params=_cparams(("arbitrary", "arbitrary")),
        name=name,
    )(jnp.asarray(kinds, I32), jnp.asarray(halves, I32), h2d, w, tabs)


def _pad_rows(a, rows):
    if a.shape[0] == rows:
        return a
    return jnp.concatenate([a, jnp.zeros((rows - a.shape[0], a.shape[1]), a.dtype)], axis=0)


def _tn_dot(a, b):
    return lax.dot_general(a, b, (((0,), (0,)), ((), ())), preferred_element_type=F32)


def _softmax_steps(sts, pv_fns, carries):
    m_news = [jnp.maximum(c[0], jnp.max(st, axis=0, keepdims=True)) for st, c in zip(sts, carries)]
    ps = [jnp.exp2(st - m_new) for st, m_new in zip(sts, m_news)]
    pvs = [fn(p.astype(BF16)) for fn, p in zip(pv_fns, ps)]
    out = []
    for (m, l, acc), m_new, p, pv in zip(carries, m_news, ps, pvs):
        alpha = jnp.exp2(m - m_new)
        out.append((m_new, alpha * l + jnp.sum(p, axis=0, keepdims=True), alpha * acc + pv))
    return tuple(out)


def _softmax_init(n):
    return (jnp.full((1, n), NEG, F32), jnp.zeros((1, n), F32), jnp.zeros((LANE, n), F32))


def _attn_geometry(t, plen):
    tq = min(t, 256)
    tqp = max(tq, LANE)
    wp = min(plen, PAST_CHUNK) if plen else 0
    ncp = plen // wp if plen else 0
    assert t % tq == 0 and tq % CHUNK == 0 and (plen == 0 or plen % wp == 0)
    return tq, tqp, t // tq, wp, ncp


def _carry_load(scs, heads):
    m_sc, l_sc, acc_sc = scs
    return tuple((m_sc[h], l_sc[h], acc_sc[h]) for h in heads)


def _carry_store(scs, heads, carries):
    m_sc, l_sc, acc_sc = scs
    for h, (m, l, acc) in zip(heads, carries):
        m_sc[h] = m
        l_sc[h] = l
        acc_sc[h] = acc


def _diff_kernel(*refs, tq, tqp, ncp, lam_init):
    lq1, lk1, lq2, lk2, g_ref, q_ref, kn_ref, vn_ref = refs[:8]
    if ncp:
        pk_ref, pv_ref, o_ref = refs[8:11]
        scs = refs[11:]
        c = pl.program_id(2)
    else:
        (o_ref,) = refs[8:]
    i = pl.program_id(1)
    wn = tq
    n2 = 2 * tqp
    lane = lax.broadcasted_iota(I32, (tqp, LANE), 1)
    lam = (jnp.exp(jnp.sum(lq1[...] * lk1[...], axis=1, keepdims=True))
           - jnp.exp(jnp.sum(lq2[...] * lk2[...], axis=1, keepdims=True)) + lam_init)
    krow = lax.broadcasted_iota(I32, (wn, n2), 0)
    qcol = lax.broadcasted_iota(I32, (wn, n2), 1)
    qcol = jnp.where(qcol >= tqp, qcol - tqp, qcol)
    diag_mask = (krow // CHUNK) <= (qcol // CHUNK)

    for h0 in range(0, N_DIFF_HEADS, HEADS_PER_LOOP):
        heads = list(range(h0, h0 + HEADS_PER_LOOP))
        qqs = []
        for h in heads:
            q = _pad_rows(q_ref[:, h * LANE:(h + 1) * LANE], tqp)
            zero = jnp.zeros_like(q)
            qqs.append(jnp.concatenate([jnp.where(lane < DIFF_QK_DIM, q, zero),
                                        jnp.where(lane >= DIFF_QK_DIM, q, zero)], axis=0))

        def step_all(get_k, pv_fn, carries, mask, heads=heads, qqs=qqs):
            sts = [_nt_dot(get_k(h), qq) for h, qq in zip(heads, qqs)]
            if mask is not None:
                sts = [jnp.where(mask, st, NEG) for st in sts]
            return _softmax_steps(sts, [pv_fn(h) for h in heads], carries)

        def new_step(jn, carries, mask, step_all=step_all):
            r0 = pl.multiple_of(jn * wn, wn)
            return step_all(lambda h: kn_ref[pl.ds(r0, wn), h * LANE:(h + 1) * LANE],
                            lambda h: functools.partial(_tn_dot, vn_ref[pl.ds(r0, wn), h * LANE:(h + 1) * LANE]),
                            carries, mask)

        def new_rows_and_finish(carries, heads=heads, new_step=new_step):
            carries = lax.fori_loop(0, i, lambda jn, cs: new_step(jn, cs, None), carries)
            carries = new_step(i, carries, diag_mask)
            for h, (m, l, acc) in zip(heads, carries):
                o = acc / l
                od = o[:, :tqp] - lam * o[:, tqp:]
                od = od * lax.rsqrt(jnp.mean(od * od, axis=0, keepdims=True) + RMS_EPS)
                od = od * g_ref[...] * (1.0 - lam_init)
                o_ref[:, h * LANE:(h + 1) * LANE] = od.T[:tq].astype(o_ref.dtype)

        if ncp:
            @pl.when(c == 0)
            def _(heads=heads):
                _carry_store(scs, heads, tuple(_softmax_init(n2) for _ in heads))

            carries = step_all(lambda h: pk_ref[:, h * LANE:(h + 1) * LANE].astype(BF16),
                               lambda h: functools.partial(_tn_dot, pv_ref[:, h * LANE:(h + 1) * LANE].astype(BF16)),
                               _carry_load(scs, heads), None)
            _carry_store(scs, heads, carries)
            pl.when(c == ncp - 1)(lambda heads=heads, fin=new_rows_and_finish: fin(_carry_load(scs, heads)))
        else:
            new_rows_and_finish(tuple(_softmax_init(n2) for _ in heads))


def _diff_attention(qx, yb, lam_params, subln, past, layer, b, t, lam_init):
    n = b * t
    plen = 0 if past is None else past[0].shape[2]
    tq, tqp, nq, wp, ncp = _attn_geometry(t, plen)
    vec = lambda a: a.reshape(1, -1).astype(F32)
    small = pl.BlockSpec((1, DIFF_QK_DIM), lambda bi, i, *_: (0, 0))
    in_specs = [small, small, small, small,
                pl.BlockSpec((DIFF_V_DIM, 1), lambda bi, i, *_: (0, 0)),
                pl.BlockSpec((tq, HEAD_W), lambda bi, i, *_: (bi * nq + i, 0)),
                pl.BlockSpec((t, HEAD_W), lambda bi, i, *_: (bi, 0)),
                pl.BlockSpec((t, HEAD_W), lambda bi, i, *_: (bi, 2))]
    args = [vec(p) for p in lam_params] + [subln.reshape(-1, 1).astype(F32), qx, yb, yb]
    grid, scratch = (b, nq), []
    if past is not None:
        chunk = pl.BlockSpec((None, None, wp, HEAD_W), lambda bi, i, c: (layer, bi, c, 0))
        in_specs += [chunk, chunk]
        args += list(past)
        grid = (b, nq, ncp)
        scratch = [pltpu.VMEM((N_DIFF_HEADS, 1, 2 * tqp), F32), pltpu.VMEM((N_DIFF_HEADS, 1, 2 * tqp), F32),
                   pltpu.VMEM((N_DIFF_HEADS, LANE, 2 * tqp), F32)]
    return pl.pallas_call(
        functools.partial(_diff_kernel, tq=tq, tqp=tqp, ncp=ncp, lam_init=lam_init),
        grid=grid,
        in_specs=in_specs,
        out_specs=pl.BlockSpec((tq, HEAD_W), lambda bi, i, *_: (bi * nq + i, 0)),
        out_shape=jax.ShapeDtypeStruct((n, HEAD_W), BF16),
        scratch_shapes=scratch,
        compiler_params=_cparams(("arbitrary",) * len(grid)),
        name="diff_attn",
    )(*args)


def _dsa_indexer(qi_ref, wq_ref, zk_ref, pki_ref, keyn_sc, biasn_sc, keyp_sc, biasp_sc, *, tq, tqp, wp, ncp, topk):
    i = pl.program_id(1)
    wn = tq
    nnew = i + 1
    kf = float(topk)

    lane = lax.broadcasted_iota(I32, (tqp, LANE), 1)
    wit = _pad_rows(wq_ref[...], tqp).T * (N_IDX_HEADS ** -0.5 * IDX_DIM ** -0.5)
    qqs, wcats = [], []
    for p in range(N_IDX_HEADS // 2):
        qp = _pad_rows(qi_ref[:, p * LANE:(p + 1) * LANE], tqp)
        zero = jnp.zeros_like(qp)
        qqs.append(jnp.concatenate([jnp.where(lane < IDX_DIM, qp, zero),
                                    jnp.where(lane >= IDX_DIM, qp, zero)], axis=0))
        wcats.append(jnp.concatenate([wit[2 * p:2 * p + 1], wit[2 * p + 1:2 * p + 2]], axis=1))

    def scores_t(kdup):
        acc = jnp.zeros((kdup.shape[0], tqp), F32)
        for qq, wc in zip(qqs, wcats):
            r = jnp.maximum(_nt_dot(kdup, qq), 0.0) * wc
            acc = acc + r[:, :tqp] + r[:, tqp:]
        return acc

    def to_key(s, visible):
        s = jnp.where(s == 0.0, 0.0, s)
        if visible is not None:
            s = jnp.where(visible, s, -jnp.inf)
        bits = lax.bitcast_convert_type(s, I32)
        return bits ^ (jnp.right_shift(bits, 31) & 0x7FFFFFFF)

    if ncp:
        def past_keys(c, _):
            r0 = pl.multiple_of(c * wp, wp)
            keyp_sc[c] = to_key(scores_t(pki_ref[pl.ds(r0, wp), :]), None)
            return 0
        lax.fori_loop(0, ncp, past_keys, 0)

    def new_keys(jn, _):
        r0 = pl.multiple_of(jn * wn, wn)
        s = scores_t(zk_ref[pl.ds(r0, wn), :])
        kpos = jn * wn + lax.broadcasted_iota(I32, (wn, tqp), 0)
        qpos = i * tq + lax.broadcasted_iota(I32, (wn, tqp), 1)
        keyn_sc[jn] = to_key(s, (kpos // CHUNK) <= (qpos // CHUNK))
        return 0
    lax.fori_loop(0, nnew, new_keys, 0)

    def count(pred):
        fr = COUNT_CHAINS * SUBLANE

        def fold(kk):
            m = jnp.where(pred(kk), 1.0, 0.0)
            return jnp.sum(m.reshape(kk.shape[0] // fr, fr, tqp), axis=0)
        acc = jnp.zeros((fr, tqp), F32)
        if ncp:
            acc = lax.fori_loop(0, ncp, lambda c, a: a + fold(keyp_sc[c]), acc)
        acc = lax.fori_loop(0, nnew, lambda jn, a: a + fold(keyn_sc[jn]), acc)
        return jnp.sum(acc, axis=0, keepdims=True)

    def pass_body(t, kth):
        cand = kth + jnp.left_shift(jnp.int32(1), 31 - t)
        cnt = count(lambda kk: kk >= cand)
        return jnp.where(cnt >= kf, cand, kth)
    kth = lax.fori_loop(0, 32, pass_body, jnp.full((1, tqp), INT_MIN, I32))

    need = kf - count(lambda kk: kk > kth)

    def lower_tri(w):
        ra = lax.broadcasted_iota(I32, (w, w), 0)
        rb = lax.broadcasted_iota(I32, (w, w), 1)
        return jnp.where(rb <= ra, 1.0, 0.0).astype(BF16)

    def bias_chunk(kk, seen, tril):
        eq = kk == kth
        eqf = jnp.where(eq, 1.0, 0.0)
        rank = seen + jnp.dot(tril, eqf.astype(BF16), preferred_element_type=F32)
        sel = ((kk > kth) | (eq & (rank <= need))) & (kk != KEY_NEG_INF)
        return jnp.where(sel, 0.0, NEG), seen + jnp.sum(eqf, axis=0, keepdims=True)

    seen = jnp.zeros((1, tqp), F32)
    if ncp:
        tril_p = lower_tri(wp)

        def past_bias(c, seen):
            bias, seen = bias_chunk(keyp_sc[c], seen, tril_p)
            biasp_sc[c] = bias
            return seen
        seen = lax.fori_loop(0, ncp, past_bias, seen)
    tril_n = lower_tri(wn)

    def new_bias(jn, seen):
        bias, seen = bias_chunk(keyn_sc[jn], seen, tril_n)
        biasn_sc[jn] = bias
        return seen
    lax.fori_loop(0, nnew, new_bias, seen)

def _dsa_kernel(*refs, tq, tqp, wp, ncp, topk):
    qs_ref, qi_ref, wq_ref, kn_ref, vn_ref, zk_ref = refs[:6]
    idx = functools.partial(_dsa_indexer, tq=tq, tqp=tqp, wp=wp, ncp=ncp, topk=topk)
    if ncp:
        pk_ref, pv_ref, pki_ref, o_ref, keyn_sc, biasn_sc, keyp_sc, biasp_sc = refs[6:14]
        scs = refs[14:]
        c = pl.program_id(2)
        pl.when(c == 0)(lambda: idx(qi_ref, wq_ref, zk_ref, pki_ref, keyn_sc, biasn_sc, keyp_sc, biasp_sc))
    else:
        o_ref, keyn_sc, biasn_sc = refs[6:]
        idx(qi_ref, wq_ref, zk_ref, None, keyn_sc, biasn_sc, None, None)
    wn = tq
    nnew = pl.program_id(1) + 1

    for h0 in range(0, N_DSA_HEADS, HEADS_PER_LOOP):
        heads = list(range(h0, h0 + HEADS_PER_LOOP))
        qs = [_pad_rows(qs_ref[:, h * LANE:(h + 1) * LANE], tqp) for h in heads]

        def step_all(get_k, pv_fn, bias, carries, heads=heads, qs=qs):
            sts = [_nt_dot(get_k(h), q) + bias for h, q in zip(heads, qs)]
            return _softmax_steps(sts, [pv_fn(h) for h in heads], carries)

        def new_body(jn, carries, step_all=step_all):
            r0 = pl.multiple_of(jn * wn, wn)
            return step_all(lambda h: kn_ref[pl.ds(r0, wn), h * LANE:(h + 1) * LANE],
                            lambda h: functools.partial(_tn_dot, vn_ref[pl.ds(r0, wn), h * LANE:(h + 1) * LANE]),
                            biasn_sc[jn], carries)

        def new_rows_and_finish(carries, heads=heads, new_body=new_body):
            carries = lax.fori_loop(0, nnew, new_body, carries)
            for h, (m, l, acc) in zip(heads, carries):
                o_ref[:, h * LANE:(h + 1) * LANE] = (acc / l).T[:tq].astype(o_ref.dtype)

        if ncp:
            @pl.when(c == 0)
            def _(heads=heads):
                _carry_store(scs, heads, tuple(_softmax_init(tqp) for _ in heads))

            carries = step_all(lambda h: pk_ref[:, h * LANE:(h + 1) * LANE].astype(BF16),
                               lambda h: functools.partial(_tn_dot, pv_ref[:, h * LANE:(h + 1) * LANE].astype(BF16)),
                               biasp_sc[c], _carry_load(scs, heads))
            _carry_store(scs, heads, carries)
            pl.when(c == ncp - 1)(lambda heads=heads, fin=new_rows_and_finish: fin(_carry_load(scs, heads)))
        else:
            new_rows_and_finish(tuple(_softmax_init(tqp) for _ in heads))


def _dsa_attention(qx, yb, zf, zb, past, layer, b, t):
    n = b * t
    plen = 0 if past is None else past[0].shape[2]
    topk = min(TOPK_MAX, (plen + t) // 4)
    tq, tqp, nq, wp, ncp = _attn_geometry(t, plen)
    in_specs = [
        pl.BlockSpec((tq, HEAD_W), lambda bi, i, *_: (bi * nq + i, 2)),
        pl.BlockSpec((tq, HEAD_W), lambda bi, i, *_: (bi * nq + i, 1)),
        pl.BlockSpec((tq, LANE), lambda bi, i, *_: (bi * nq + i, 1)),
        pl.BlockSpec((t, HEAD_W), lambda bi, i, *_: (bi, 1)),
        pl.BlockSpec((t, HEAD_W), lambda bi, i, *_: (bi, 3)),
        pl.BlockSpec((t, LANE), lambda bi, i, *_: (bi, 0)),
    ]
    args = [qx, qx, zf, yb, yb, zb]
    scratch = [pltpu.VMEM((nq, tq, tqp), I32), pltpu.VMEM((nq, tq, tqp), F32)]
    grid = (b, nq)
    if past is not None:
        chunk = pl.BlockSpec((None, None, wp, HEAD_W), lambda bi, i, c: (layer, bi, c, 0))
        in_specs += [chunk, chunk, pl.BlockSpec((None, None, plen, LANE), lambda bi, i, c: (layer, bi, 0, 0))]
        args += list(past)
        grid = (b, nq, ncp)
        scratch += [pltpu.VMEM((ncp, wp, tqp), I32), pltpu.VMEM((ncp, wp, tqp), F32),
                    pltpu.VMEM((N_DSA_HEADS, 1, tqp), F32), pltpu.VMEM((N_DSA_HEADS, 1, tqp), F32),
                    pltpu.VMEM((N_DSA_HEADS, LANE, tqp), F32)]
    return pl.pallas_call(
        functools.partial(_dsa_kernel, tq=tq, tqp=tqp, wp=wp, ncp=ncp, topk=topk),
        grid=grid,
        in_specs=in_specs,
        out_specs=pl.BlockSpec((tq, HEAD_W), lambda bi, i, *_: (bi * nq + i, 0)),
        out_shape=jax.ShapeDtypeStruct((n, HEAD_W), BF16),
        scratch_shapes=scratch,
        compiler_params=_cparams(("arbitrary",) * len(grid)),
        name="dsa_attn",
    )(*args)


def _merge_kernel(od_ref, os_ref, wa_ref, wb_ref, ga_ref, gb_ref, o_ref):
    sig = lambda z: 1.0 / (1.0 + jnp.exp(-z.astype(F32)))
    od, os_ = od_ref[...], os_ref[...]
    tn = o_ref.shape[1]
    sub = min(tn, PROJ_SUB)
    for s0 in range(0, tn, sub):
        cols = slice(s0, s0 + sub)
        a = jnp.dot(od, wa_ref[:, cols], preferred_element_type=F32)
        bq = jnp.dot(os_, wb_ref[:, cols], preferred_element_type=F32)
        o_ref[:, cols] = (sig(ga_ref[:, cols]) * a + sig(gb_ref[:, cols]) * bq).astype(o_ref.dtype)


def _merge(od, os_, wa, wb, qx, d):
    n = od.shape[0]
    tm = min(n, 1024)
    tn = min(d, 1024)
    gate0 = (3 * HEAD_W) // tn
    nd = d // tn
    return pl.pallas_call(
        _merge_kernel,
        grid=(n // tm, nd),
        in_specs=[
            pl.BlockSpec((tm, HEAD_W), lambda i, j: (i, 0)),
            pl.BlockSpec((tm, HEAD_W), lambda i, j: (i, 0)),
            pl.BlockSpec((HEAD_W, tn), lambda i, j: (0, j)),
            pl.BlockSpec((HEAD_W, tn), lambda i, j: (0, j)),
            pl.BlockSpec((tm, tn), lambda i, j: (i, gate0 + j)),
            pl.BlockSpec((tm, tn), lambda i, j: (i, gate0 + nd + j)),
        ],
        out_specs=pl.BlockSpec((tm, tn), lambda i, j: (i, j)),
        out_shape=jax.ShapeDtypeStruct((n, d), BF16),
        compiler_params=_cparams(("arbitrary", "arbitrary")),
        name="merge",
    )(od, os_, wa, wb, qx, qx)


def _outproj_kernel(m_ref, w_ref, x_ref, g1_ref, n2_ref, sc_ref, sh_ref, wr_ref, br_ref, x1_ref, hx_ref):
    d = x_ref.shape[-1]
    y = jnp.dot(m_ref[...], w_ref[...], preferred_element_type=F32)
    x1 = x_ref[...] + g1_ref[...] * y
    x1_ref[...] = x1
    h = x1 * lax.rsqrt(jnp.mean(x1 * x1, axis=-1, keepdims=True) + RMS_EPS) * n2_ref[...]
    h = h * (1.0 + sc_ref[...]) + sh_ref[...]
    hi = h.astype(BF16)
    lo = (h - hi.astype(F32)).astype(BF16)
    both = jnp.dot(hi, wr_ref[...], preferred_element_type=F32)
    lg = both[:, :LANE] + both[:, LANE:] + jnp.dot(lo, wr_ref[:, :LANE], preferred_element_type=F32)
    hx_ref[:, :d] = h
    hx_ref[:, d:] = lg + br_ref[...]


def _outproj(merged, w_out, x, mod5, layer, boff, n2, wr, br):
    b, t, d = x.shape
    tm = min(t, 512)
    nt = t // tm
    row = lambda k: pl.BlockSpec((None, None, None, 1, d), lambda bi, i, k=k: (layer, boff + bi, k, 0, 0))
    return pl.pallas_call(
        _outproj_kernel,
        grid=(b, nt),
        in_specs=[
            pl.BlockSpec((tm, d), lambda bi, i: (bi * nt + i, 0)),
            pl.BlockSpec((d, d), lambda bi, i: (0, 0)),
            pl.BlockSpec((None, tm, d), lambda bi, i: (bi, i, 0)),
            row(2),
            pl.BlockSpec((1, d), lambda bi, i: (0, 0)),
            row(4), row(3),
            pl.BlockSpec((d, 2 * LANE), lambda bi, i: (0, 0)),
            pl.BlockSpec((1, LANE), lambda bi, i: (0, 0)),
        ],
        out_specs=[
            pl.BlockSpec((None, tm, d), lambda bi, i: (bi, i, 0)),
            pl.BlockSpec((tm, d + LANE), lambda bi, i: (bi * nt + i, 0)),
        ],
        out_shape=[jax.ShapeDtypeStruct((b, t, d), F32),
                   jax.ShapeDtypeStruct((b * t, d + LANE), F32)],
        compiler_params=_cparams(("arbitrary", "arbitrary")),
        name="outproj",
    )(merged, w_out, x, mod5, n2.reshape(1, d), mod5, mod5, wr, br)


def _route_kernel(lg_ref, o_ref):
    lt = lg_ref[...].T
    tm = lt.shape[1]
    e = EXPERTS_PER_GROUP
    rid = lax.broadcasted_iota(I32, (e, tm), 0)
    gl = jnp.where(rid < N_GROUPS, lt[0:e], -jnp.inf)
    gmax = jnp.max(gl, axis=0, keepdims=True)
    ge = jnp.exp(gl - gmax)
    pg = ge / jnp.sum(ge, axis=0, keepdims=True)
    gidx = jnp.min(jnp.where(gl == gmax, rid, e), axis=0, keepdims=True)
    gw = jnp.where(rid == gidx, pg, 0.0)
    parts = []
    for g in range(N_GROUPS):
        el = lt[e * (g + 1):e * (g + 2)]
        m1 = jnp.max(el, axis=0, keepdims=True)
        i1 = jnp.min(jnp.where(el == m1, rid, e), axis=0, keepdims=True)
        el2 = jnp.where(rid == i1, -jnp.inf, el)
        m2 = jnp.max(el2, axis=0, keepdims=True)
        i2 = jnp.min(jnp.where(el2 == m2, rid, e), axis=0, keepdims=True)
        e2 = jnp.exp(m2 - m1)
        p1 = 1.0 / (1.0 + e2)
        p2 = e2 / (1.0 + e2)
        within = jnp.where(rid == i1, p1, 0.0) + jnp.where(rid == i2, p2, 0.0)
        parts.append(within * gw[g:g + 1])
    parts.append(jnp.broadcast_to(gidx.astype(F32), (e, tm)))
    parts.append(jnp.zeros((LANE - (N_GROUPS + 1) * e, tm), F32))
    o_ref[...] = jnp.concatenate(parts, axis=0).T


GROUP_ID_COL = N_GROUPS * EXPERTS_PER_GROUP


def _route(x, col_block):
    n = x.shape[0]
    tm = min(n, 512)
    return pl.pallas_call(
        _route_kernel,
        grid=(n // tm,),
        in_specs=[pl.BlockSpec((tm, LANE), lambda i: (i, col_block))],
        out_specs=pl.BlockSpec((tm, LANE), lambda i: (i, 0)),
        out_shape=jax.ShapeDtypeStruct((n, LANE), F32),
        compiler_params=_cparams(("arbitrary",)),
        name="route",
    )(x)


def _moe_plan(gid, ts):
    n = gid.shape[0]
    ntile = n // ts + N_GROUPS
    onehot = (gid[:, None] == jnp.arange(N_GROUPS, dtype=I32)[None, :]).astype(I32)
    csum = jnp.cumsum(onehot, axis=0)
    rank = jnp.sum((csum - onehot) * onehot, axis=1)
    tiles = (csum[-1] + ts - 1) // ts
    tile_end = jnp.cumsum(tiles)
    pos = jnp.sum(onehot * ((tile_end - tiles) * ts)[None, :], axis=1) + rank
    src = jnp.zeros((ntile * ts,), I32).at[pos].set(jnp.arange(n, dtype=I32))
    tile_gid = jnp.sum((jnp.arange(ntile, dtype=I32)[:, None] >= tile_end[None, :]).astype(I32), axis=1)
    return pos.astype(I32), src, jnp.minimum(tile_gid, N_GROUPS - 1).astype(I32)


def _row_copy(x_hbm, row, buf, r, sem):
    return pltpu.make_async_copy(x_hbm.at[pl.ds(row, 1)], buf.at[pl.ds(r, 1)], sem)


def _row_gather_pipelined(idx_ref, step, nsteps, x_hbm, bufs, sems):
    rows = bufs.shape[1]

    def start_all(s):
        slot = s % 2

        def start(k, carry):
            r0 = pl.multiple_of(k * SUBLANE, SUBLANE)
            for j in range(SUBLANE):
                _row_copy(x_hbm, idx_ref[s * rows + r0 + j], bufs.at[slot], r0 + j, sems.at[slot]).start()
            return carry
        lax.fori_loop(0, rows // SUBLANE, start, 0)

    pl.when(step == 0)(lambda: start_all(step))
    pl.when(step + 1 < nsteps)(lambda: start_all(step + 1))
    slot = step % 2

    def wait(k, carry):
        r0 = pl.multiple_of(k * SUBLANE, SUBLANE)
        for j in range(SUBLANE):
            _row_copy(x_hbm, 0, bufs.at[slot], r0 + j, sems.at[slot]).wait()
        return carry
    lax.fori_loop(0, rows // SUBLANE, wait, 0)
    return slot


def _dispatch_kernel(src_ref, x_hbm, h_ref, lg_ref, bufs, sems):
    slot = _row_gather_pipelined(src_ref, pl.program_id(0), pl.num_programs(0), x_hbm, bufs, sems)
    d = h_ref.shape[1]
    h_ref[...] = bufs[slot, :, :d].astype(h_ref.dtype)
    lg_ref[...] = bufs[slot, :, d:]


def _dispatch(hx, src, ts):
    npad = src.shape[0]
    d = hx.shape[1] - LANE
    return pl.pallas_call(
        _dispatch_kernel,
        grid_spec=pltpu.PrefetchScalarGridSpec(
            num_scalar_prefetch=1,
            grid=(npad // ts,),
            in_specs=[pl.BlockSpec(memory_space=pl.ANY)],
            out_specs=[pl.BlockSpec((ts, d), lambda i, s: (i, 0)),
                       pl.BlockSpec((ts, LANE), lambda i, s: (i, 0))],
            scratch_shapes=[pltpu.VMEM((2, ts, d + LANE), F32), pltpu.SemaphoreType.DMA((2,))],
        ),
        out_shape=[jax.ShapeDtypeStruct((npad, d), BF16), jax.ShapeDtypeStruct((npad, LANE), F32)],
        compiler_params=_cparams(("arbitrary",)),
        name="moe_dispatch",
    )(src, hx)


def _moe_up_kernel(gid_ref, h_ref, w1_ref, w3_ref, comb_ref, o_ref):
    col = gid_ref[pl.program_id(0)] * EXPERTS_PER_GROUP + pl.program_id(1)
    h = h_ref[...]
    a = jnp.dot(h, w1_ref[...], preferred_element_type=F32)
    u = jnp.dot(h, w3_ref[...], preferred_element_type=F32)
    comb = comb_ref[...]
    lane = lax.broadcasted_iota(I32, comb.shape, 1)
    cw = jnp.sum(jnp.where(lane == col, comb, 0.0), axis=1, keepdims=True)
    o_ref[...] = (a * (1.0 / (1.0 + jnp.exp(-a))) * u * cw).astype(o_ref.dtype)


def _moe_up(hs, w1, w3, comb, tile_gid, ts):
    npad, d = hs.shape
    f = w1.shape[2]
    e = EXPERTS_PER_GROUP
    return pl.pallas_call(
        _moe_up_kernel,
        grid_spec=pltpu.PrefetchScalarGridSpec(
            num_scalar_prefetch=1,
            grid=(npad // ts, e),
            in_specs=[
                pl.BlockSpec((ts, d), lambda i, j, g: (i, 0)),
                pl.BlockSpec((None, d, f), lambda i, j, g: (g[i] * e + j, 0, 0)),
                pl.BlockSpec((None, d, f), lambda i, j, g: (g[i] * e + j, 0, 0)),
                pl.BlockSpec((ts, LANE), lambda i, j, g: (i, 0)),
            ],
            out_specs=pl.BlockSpec((ts, f), lambda i, j, g: (i, j)),
        ),
        out_shape=jax.ShapeDtypeStruct((npad, e * f), BF16),
        compiler_params=_cparams(("arbitrary", "arbitrary")),
        name="moe_up",
    )(tile_gid, hs, w1, w3, comb)


def _moe_down_kernel(gid_ref, a_ref, w_ref, o_ref):
    o_ref[...] = jnp.dot(a_ref[...], w_ref[...], preferred_element_type=F32)


def _moe_down(act, w2, tile_gid, ts):
    npad, kk = act.shape
    d = w2.shape[2]
    return pl.pallas_call(
        _moe_down_kernel,
        grid_spec=pltpu.PrefetchScalarGridSpec(
            num_scalar_prefetch=1,
            grid=(npad // ts,),
            in_specs=[pl.BlockSpec((ts, kk), lambda i, g: (i, 0)),
                      pl.BlockSpec((None, kk, d), lambda i, g: (g[i], 0, 0))],
            out_specs=pl.BlockSpec((ts, d), lambda i, g: (i, 0)),
        ),
        out_shape=jax.ShapeDtypeStruct((npad, d), F32),
        compiler_params=_cparams(("arbitrary",)),
        name="moe_down",
    )(tile_gid, act, w2)


def _combine_kernel(pos_ref, ys_hbm, x_ref, g2_ref, gn_ref, sc_ref, sh_ref, x2_ref, hn_ref, bufs, sems, *, final):
    step = pl.program_id(0) * pl.num_programs(1) + pl.program_id(1)
    slot = _row_gather_pipelined(pos_ref, step, pl.num_programs(0) * pl.num_programs(1), ys_hbm, bufs, sems)
    x2 = x_ref[...] + g2_ref[...] * bufs[slot]
    x2_ref[...] = x2
    y = x2 * lax.rsqrt(jnp.mean(x2 * x2, axis=-1, keepdims=True) + RMS_EPS) * gn_ref[...]
    if not final:
        y = y * (1.0 + sc_ref[...]) + sh_ref[...]
    hn_ref[...] = y.astype(hn_ref.dtype)


def _combine(ys, pos, x1, mod5, layer, boff, g_next, next_layer, final):
    b, t, d = x1.shape
    tm = min(t, 256)
    nt = t // tm
    row = lambda l, k: pl.BlockSpec((None, None, None, 1, d), lambda bi, i, p, l=l, k=k: (l, boff + bi, k, 0, 0))
    return pl.pallas_call(
        functools.partial(_combine_kernel, final=final),
        grid_spec=pltpu.PrefetchScalarGridSpec(
            num_scalar_prefetch=1,
            grid=(b, nt),
            in_specs=[
                pl.BlockSpec(memory_space=pl.ANY),
                pl.BlockSpec((None, tm, d), lambda bi, i, p: (bi, i, 0)),
                row(layer, 5),
                pl.BlockSpec((1, d), lambda bi, i, p: (0, 0)),
                row(next_layer, 1), row(next_layer, 0),
            ],
            out_specs=[
                pl.BlockSpec((None, tm, d), lambda bi, i, p: (bi, i, 0)),
                pl.BlockSpec((None, tm, d), lambda bi, i, p: (bi, i, 0)),
            ],
            scratch_shapes=[pltpu.VMEM((2, tm, d), F32), pltpu.SemaphoreType.DMA((2,))],
        ),
        out_shape=[jax.ShapeDtypeStruct((b, t, d), F32),
                   jax.ShapeDtypeStruct((b, t, d), F32 if final else BF16)],
        compiler_params=_cparams(("arbitrary", "arbitrary")),
        name="moe_combine",
    )(pos, ys, x1, mod5, g_next.reshape(1, d), mod5, mod5)


def _pack_weights(w_in, w_br_diff, w_br_dsa, w_out, w_rg, b_rg, w_re, b_re, w_e1, w_e3, w_e2, d):
    hw = HEAD_W
    o_qd, o_kd, o_vd, o_qs, o_ks, o_vs, o_qi = (k * hw for k in range(7))
    o_ki = 7 * hw
    o_wi = o_ki + IDX_DIM
    o_gt = o_wi + N_IDX_HEADS
    sl = lambda o, n: w_in[:, :, o:o + n]
    wx = jnp.concatenate([sl(o_qd, hw), sl(o_qi, hw), sl(o_qs, hw), sl(o_gt, 2 * d)], axis=-1).astype(BF16)
    wy = jnp.concatenate([sl(o_kd, hw), sl(o_ks, hw), sl(o_vd, hw), sl(o_vs, hw)], axis=-1).astype(BF16)
    depth = w_in.shape[0]
    pad = jnp.zeros((depth, d, 2 * LANE - 2 * IDX_DIM - N_IDX_HEADS), w_in.dtype)
    wz = jnp.concatenate([sl(o_ki, IDX_DIM), sl(o_ki, IDX_DIM), sl(o_wi, N_IDX_HEADS), pad], axis=-1).astype(BF16)
    ne = N_GROUPS * EXPERTS_PER_GROUP
    rpad0 = jnp.zeros((depth, d, EXPERTS_PER_GROUP - N_GROUPS), F32)
    rpad1 = jnp.zeros((depth, d, LANE - EXPERTS_PER_GROUP - ne), F32)
    wr = jnp.concatenate([w_rg, rpad0, w_re, rpad1], axis=-1)
    wr_hi = wr.astype(BF16)
    wr_lo = (wr - wr_hi.astype(F32)).astype(BF16)
    br = jnp.concatenate([b_rg, jnp.zeros((depth, EXPERTS_PER_GROUP - N_GROUPS), F32), b_re,
                          jnp.zeros((depth, LANE - EXPERTS_PER_GROUP - ne), F32)], axis=-1).reshape(depth, 1, LANE)
    f = w_e1.shape[-1]
    return dict(
        wx=wx, wy=wy, wz=wz,
        wa=w_br_diff.astype(BF16), wb=w_br_dsa.astype(BF16), wo=w_out.astype(BF16),
        wr=jnp.concatenate([wr_hi, wr_lo], axis=-1), br=br,
        w1=w_e1.reshape(depth, ne, d, f).astype(BF16), w3=w_e3.reshape(depth, ne, d, f).astype(BF16),
        w2=w_e2.reshape(depth, N_GROUPS, EXPERTS_PER_GROUP * f, d).astype(BF16),
    )


def _trunk(x, pos, boff, mod5, past, pw, norm1, norm2, norm_f, lam_params, subln_g):
    b, t, d = x.shape
    n = b * t
    depth = norm1.shape[0]
    rows = n if t < 1024 else t

    def tables(head_dim, width, active, scale=1.0):
        tab = _rope_tables(pos, head_dim, width, active, scale)
        return jnp.tile(tab, (1, rows // t, 1)) if rows != t else tab

    log2e = math.log2(math.e)
    tabs = jnp.stack([tables(DIFF_QK_DIM, LANE, LANE), tables(DSA_HEAD_DIM, LANE, LANE),
                      tables(DIFF_QK_DIM, LANE, 0),
                      tables(DIFF_QK_DIM, LANE, LANE, DIFF_QK_DIM ** -0.5 * log2e),
                      tables(DSA_HEAD_DIM, LANE, LANE, DSA_HEAD_DIM ** -0.5 * log2e)])
    halves = (DIFF_QK_DIM // 8, DSA_HEAD_DIM // 8, DIFF_QK_DIM // 8, DIFF_QK_DIM // 8, DSA_HEAD_DIM // 8)
    R64, R128, PLAIN, R64_Q, R128_Q = range(5)
    tabz = tables(IDX_DIM, 2 * LANE, LANE)[None]
    gate_tiles = (2 * d) // HEAD_W
    ts = 512 if n >= 8192 else 128

    h = _normmod(x, norm1[0], mod5, 0, boff, 1, 0)
    new_rows = ([], [], [], [], [])
    y = None
    for l in range(depth):
        h2d = h.reshape(n, d)
        (qx,) = _proj(h2d, pw["wx"][l], tabs, halves, (R64_Q, R64, R128_Q) + (PLAIN,) * gate_tiles, HEAD_W,
                      (BF16,), "proj_q")
        yf, yb = _proj(h2d, pw["wy"][l], tabs, halves, (R64, R128, PLAIN, PLAIN), HEAD_W, (F32, BF16), "proj_kv")
        zf, zb = _proj(h2d, pw["wz"][l], tabz, (IDX_DIM // 8,), (0,), 2 * LANE, (F32, BF16), "proj_idx")
        for lst, r in zip(new_rows, (yf[:, 0:HEAD_W], yf[:, 2 * HEAD_W:3 * HEAD_W], yf[:, HEAD_W:2 * HEAD_W],
                                     yf[:, 3 * HEAD_W:4 * HEAD_W], zf[:, 0:IDX_DIM])):
            lst.append(r)
        lam_init = 0.8 - 0.6 * math.exp(-0.3 * l)
        lp = tuple(p[l] for p in lam_params)
        past_d = None if past is None else (past["dk"], past["dv"])
        past_s = None if past is None else (past["sk"], past["sv"], past["ik"])
        od = _diff_attention(qx, yb, lp, subln_g[l], past_d, l, b, t, lam_init)
        os_ = _dsa_attention(qx, yb, zf, zb, past_s, l, b, t)
        merged = _merge(od, os_, pw["wa"][l], pw["wb"][l], qx, d)
        x1, hx = _outproj(merged, pw["wo"][l], x, mod5, l, boff, norm2[l],
                          pw["wr"][l], pw["br"][l])
        gid = _route(hx, d // LANE)[:, GROUP_ID_COL].astype(I32)
        pos, src, tile_gid = _moe_plan(gid, ts)
        hs, lgs = _dispatch(hx, src, ts)
        act = _moe_up(hs, pw["w1"][l], pw["w3"][l], _route(lgs, 0), tile_gid, ts)
        ys = _moe_down(act, pw["w2"][l], tile_gid, ts)
        final = l == depth - 1
        g_next = norm_f if final else norm1[l + 1]
        x, h = _combine(ys, pos, x1, mod5, l, boff, g_next, 0 if final else l + 1, final)
        if final:
            y = h
    return y, new_rows


def kernel(x_prompt, x_sample, cache_diff_k, cache_diff_v, cache_dsa_k, cache_dsa_v, cache_idx_k, c_prompt, c_sample, norm1, norm2, norm_f, w_ada, b_ada, w_in, lambda_q1, lambda_k1, lambda_q2, lambda_k2, subln_g, w_br_diff, w_br_dsa, w_out, w_router_group, b_router_group, w_router_expert, b_router_expert, w_expert_gate, w_expert_up, w_expert_down):
    bp, tp, d = x_prompt.shape
    bs, ts, _ = x_sample.shape
    depth = norm1.shape[0]
    plen = cache_diff_k.shape[2]
    assert plen % CHUNK == 0 and tp % CHUNK == 0

    pw = _pack_weights(w_in, w_br_diff, w_br_dsa, w_out, w_router_group, b_router_group, w_router_expert,
                       b_router_expert, w_expert_gate, w_expert_up, w_expert_down, d)
    mod = _ada(jnp.concatenate([c_prompt, c_sample], axis=0), w_ada, b_ada)
    mod5 = mod.reshape(depth, bp + bs, 6, 1, d)
    lam_params = (lambda_q1, lambda_k1, lambda_q2, lambda_k2)

    pos_p = jnp.arange(tp, dtype=jnp.int32)
    y_p, rows_p = _trunk(x_prompt, pos_p, 0, mod5, None, pw, norm1, norm2, norm_f, lam_params, subln_g)

    rows2d = lambda c: c.reshape(depth, bs, plen, HEAD_W)
    past = dict(dk=rows2d(cache_diff_k), dv=rows2d(cache_diff_v), sk=rows2d(cache_dsa_k), sv=rows2d(cache_dsa_v),
                ik=jnp.concatenate([cache_idx_k, cache_idx_k], axis=-1).astype(BF16))
    pos_s = plen + jnp.arange(ts, dtype=jnp.int32)
    y_s, rows_s = _trunk(x_sample, pos_s, bp, mod5, past, pw, norm1, norm2, norm_f, lam_params, subln_g)

    def finish(rows, b, t):
        dk, dv, sk, sv, ik = (jnp.stack(r, axis=0) for r in rows)
        return (dk.reshape(depth, b, t, N_DIFF_HEADS, 2, DIFF_QK_DIM), dv.reshape(depth, b, t, N_DIFF_HEADS, DIFF_V_DIM),
                sk.reshape(depth, b, t, N_DSA_HEADS, DSA_HEAD_DIM), sv.reshape(depth, b, t, N_DSA_HEADS, DSA_HEAD_DIM),
                ik.reshape(depth, b, t, IDX_DIM))

    return (y_p, y_s) + finish(rows_p, bp, tp) + finish(rows_s, bs, ts)
```

```python
import functools
import math

import numpy as np
import jax
import jax.numpy as jnp
from jax import lax
from jax.experimental import pallas as pl
from jax.experimental.pallas import tpu as pltpu

F32 = jnp.float32
BF16 = jnp.bfloat16
I32 = jnp.int32

CHUNK = 64
ROPE_THETA = 500000.0
RMS_EPS = 1e-6
N_DIFF_HEADS = 8
DIFF_QK_DIM = 64
DIFF_V_DIM = 128
N_DSA_HEADS = 8
DSA_HEAD_DIM = 128
N_IDX_HEADS = 16
IDX_DIM = 64
TOPK_MAX = 256
N_GROUPS = 4
EXPERTS_PER_GROUP = 8
HEAD_W = 1024
LANE = 128
SUBLANE = 8
NEG = -1e30
VMEM_LIMIT = 56 * 1024 * 1024
HEADS_PER_LOOP = 8
PAST_CHUNK = 1024
COUNT_CHAINS = 4
PROJ_SUB = 256

_NEG_INF_BITS = int(np.array(-np.inf, np.float32).view(np.int32))
KEY_NEG_INF = int(np.int32(_NEG_INF_BITS ^ 0x7FFFFFFF))
INT_MIN = -(2 ** 31)


def _cparams(sem):
    return pltpu.CompilerParams(dimension_semantics=sem, vmem_limit_bytes=VMEM_LIMIT)


def _nt_dot(a, b):
    return lax.dot_general(a, b, (((1,), (1,)), ((), ())), preferred_element_type=F32)


def _ada_kernel(c_ref, w_ref, b_ref, o_ref):
    c = c_ref[...]
    a = (c * (1.0 / (1.0 + jnp.exp(-c)))).astype(BF16)
    o_ref[...] = jnp.dot(a, w_ref[...].astype(BF16), preferred_element_type=F32) + b_ref[...]


def _ada(c_all, w_ada, b_ada):
    depth, d, n6 = w_ada.shape
    r = c_all.shape[0]
    tn = 1024
    return pl.pallas_call(
        _ada_kernel,
        grid=(depth, n6 // tn),
        in_specs=[
            pl.BlockSpec((r, d), lambda l, j: (0, 0)),
            pl.BlockSpec((None, d, tn), lambda l, j: (l, 0, j)),
            pl.BlockSpec((None, 1, tn), lambda l, j: (l, 0, j)),
        ],
        out_specs=pl.BlockSpec((None, r, tn), lambda l, j: (l, 0, j)),
        out_shape=jax.ShapeDtypeStruct((depth, r, n6), F32),
        compiler_params=_cparams(("arbitrary", "arbitrary")),
        name="ada",
    )(c_all, w_ada, b_ada.reshape(depth, 1, n6))


def _normmod_kernel(x_ref, g_ref, sc_ref, sh_ref, o_ref):
    x = x_ref[...]
    y = x * lax.rsqrt(jnp.mean(x * x, axis=-1, keepdims=True) + RMS_EPS) * g_ref[...]
    o_ref[...] = (y * (1.0 + sc_ref[...]) + sh_ref[...]).astype(o_ref.dtype)


def _normmod(x, g, mod5, layer, boff, k_scale, k_shift):
    b, t, d = x.shape
    tt = min(t, 512)
    return pl.pallas_call(
        _normmod_kernel,
        grid=(b, t // tt),
        in_specs=[
            pl.BlockSpec((None, tt, d), lambda bi, i: (bi, i, 0)),
            pl.BlockSpec((1, d), lambda bi, i: (0, 0)),
            pl.BlockSpec((None, None, None, 1, d), lambda bi, i: (layer, boff + bi, k_scale, 0, 0)),
            pl.BlockSpec((None, None, None, 1, d), lambda bi, i: (layer, boff + bi, k_shift, 0, 0)),
        ],
        out_specs=pl.BlockSpec((None, tt, d), lambda bi, i: (bi, i, 0)),
        out_shape=jax.ShapeDtypeStruct((b, t, d), BF16),
        compiler_params=_cparams(("arbitrary", "arbitrary")),
        name="normmod",
    )(x, g.reshape(1, d), mod5, mod5)


def _rope_tables(pos, head_dim, width, active, scale=1.0):
    rot = head_dim // 4
    half = rot // 2
    inv = ROPE_THETA ** (-jnp.arange(half, dtype=F32) * (2.0 / rot))
    ang = pos.astype(F32)[:, None] * inv[None, :]
    cos, sin = jnp.cos(ang), jnp.sin(ang)
    col = np.arange(width)
    ch = col % head_dim
    first = (ch < half) & (col < active)
    second = (ch >= half) & (ch < rot) & (col < active)
    idx = np.where(ch < half, ch, np.where(ch < rot, ch - half, 0))
    c = jnp.where((first | second)[None, :], cos[:, idx], 1.0)
    s1 = jnp.where(first[None, :], -sin[:, idx], 0.0)
    s2 = jnp.where(second[None, :], sin[:, idx], 0.0)
    return (jnp.stack([c, s1, s2]) * scale).astype(F32)


def _proj_kernel(kind_ref, half_ref, h_ref, w_ref, tab_ref, *out_refs):
    half = half_ref[kind_ref[pl.program_id(1)]]
    h = h_ref[...]
    tn = w_ref.shape[1]
    tw = tab_ref.shape[2]
    sub = min(tn, PROJ_SUB)
    for s0 in range(0, tn, sub):
        acc = jnp.dot(h, w_ref[:, s0:s0 + sub], preferred_element_type=F32)
        for g0 in range(s0, s0 + sub, LANE):
            xg = acc[:, g0 - s0:g0 - s0 + LANE]
            t0 = g0 % tw
            c = tab_ref[0, :, t0:t0 + LANE]
            s1 = tab_ref[1, :, t0:t0 + LANE]
            s2 = tab_ref[2, :, t0:t0 + LANE]
            og = xg * c + pltpu.roll(xg, LANE - half, 1) * s1 + pltpu.roll(xg, half, 1) * s2
            for o in out_refs:
                o[:, g0:g0 + LANE] = og.astype(o.dtype)


def _proj(h2d, w, tabs, halves, kinds, tn, out_dtypes, name):
    n, d = h2d.shape
    c = w.shape[1]
    tm = min(n, 1024)
    assert c == tn * len(kinds)
    nblk = tabs.shape[2] // tm
    tw = tabs.shape[3]
    return pl.pallas_call(
        _proj_kernel,
        grid_spec=pltpu.PrefetchScalarGridSpec(
            num_scalar_prefetch=2,
            grid=(n // tm, c // tn),
            in_specs=[
                pl.BlockSpec((tm, d), lambda i, j, k, hf: (i, 0)),
                pl.BlockSpec((d, tn), lambda i, j, k, hf: (0, j)),
                pl.BlockSpec((None, 3, tm, tw), lambda i, j, k, hf: (k[j], 0, i % nblk, 0)),
            ],
            out_specs=[pl.BlockSpec((tm, tn), lambda i, j, k, hf: (i, j)) for _ in out_dtypes],
        ),
        out_shape=[jax.ShapeDtypeStruct((n, c), dt) for dt in out_dtypes],
        compiler_params=_cparams(("arbitrary", "arbitrary")),
        name=name,
    )(jnp.asarray(kinds, I32), jnp.asarray(halves, I32), h2d, w, tabs)


def _pad_rows(a, rows):
    if a.shape[0] == rows:
        return a
    return jnp.concatenate([a, jnp.zeros((rows - a.shape[0], a.shape[1]), a.dtype)], axis=0)


def _tn_dot(a, b):
    return lax.dot_general(a, b, (((0,), (0,)), ((), ())), preferred_element_type=F32)


def _softmax_steps(sts, pv_fns, carries):
    m_news = [jnp.maximum(c[0], jnp.max(st, axis=0, keepdims=True)) for st, c in zip(sts, carries)]
    ps = [jnp.exp2(st - m_new) for st, m_new in zip(sts, m_news)]
    pvs = [fn(p.astype(BF16)) for fn, p in zip(pv_fns, ps)]
    out = []
    for (m, l, acc), m_new, p, pv in zip(carries, m_news, ps, pvs):
        alpha = jnp.exp2(m - m_new)
        out.append((m_new, alpha * l + jnp.sum(p, axis=0, keepdims=True), alpha * acc + pv))
    return tuple(out)


def _softmax_init(n):
    return (jnp.full((1, n), NEG, F32), jnp.zeros((1, n), F32), jnp.zeros((LANE, n), F32))


def _attn_geometry(t, plen):
    tq = min(t, 256)
    tqp = max(tq, LANE)
    wp = min(plen, PAST_CHUNK) if plen else 0
    ncp = plen // wp if plen else 0
    assert t % tq == 0 and tq % CHUNK == 0 and (plen == 0 or plen % wp == 0)
    return tq, tqp, t // tq, wp, ncp


def _carry_load(scs, heads):
    m_sc, l_sc, acc_sc = scs
    return tuple((m_sc[h], l_sc[h], acc_sc[h]) for h in heads)


def _carry_store(scs, heads, carries):
    m_sc, l_sc, acc_sc = scs
    for h, (m, l, acc) in zip(heads, carries):
        m_sc[h] = m
        l_sc[h] = l
        acc_sc[h] = acc


def _diff_kernel(*refs, tq, tqp, ncp, lam_init):
    lq1, lk1, lq2, lk2, g_ref, q_ref, kn_ref, vn_ref = refs[:8]
    if ncp:
        pk_ref, pv_ref, o_ref = refs[8:11]
        scs = refs[11:]
        c = pl.program_id(2)
    else:
        (o_ref,) = refs[8:]
    i = pl.program_id(1)
    wn = tq
    n2 = 2 * tqp
    lane = lax.broadcasted_iota(I32, (tqp, LANE), 1)
    lam = (jnp.exp(jnp.sum(lq1[...] * lk1[...], axis=1, keepdims=True))
           - jnp.exp(jnp.sum(lq2[...] * lk2[...], axis=1, keepdims=True)) + lam_init)
    krow = lax.broadcasted_iota(I32, (wn, n2), 0)
    qcol = lax.broadcasted_iota(I32, (wn, n2), 1)
    qcol = jnp.where(qcol >= tqp, qcol - tqp, qcol)
    diag_mask = (krow // CHUNK) <= (qcol // CHUNK)

    for h0 in range(0, N_DIFF_HEADS, HEADS_PER_LOOP):
        heads = list(range(h0, h0 + HEADS_PER_LOOP))
        qqs = []
        for h in heads:
            q = _pad_rows(q_ref[:, h * LANE:(h + 1) * LANE], tqp)
            zero = jnp.zeros_like(q)
            qqs.append(jnp.concatenate([jnp.where(lane < DIFF_QK_DIM, q, zero),
                                        jnp.where(lane >= DIFF_QK_DIM, q, zero)], axis=0))

        def step_all(get_k, pv_fn, carries, mask, heads=heads, qqs=qqs):
            sts = [_nt_dot(get_k(h), qq) for h, qq in zip(heads, qqs)]
            if mask is not None:
                sts = [jnp.where(mask, st, NEG) for st in sts]
            return _softmax_steps(sts, [pv_fn(h) for h in heads], carries)

        def new_step(jn, carries, mask, step_all=step_all):
            r0 = pl.multiple_of(jn * wn, wn)
            return step_all(lambda h: kn_ref[pl.ds(r0, wn), h * LANE:(h + 1) * LANE],
                            lambda h: functools.partial(_tn_dot, vn_ref[pl.ds(r0, wn), h * LANE:(h + 1) * LANE]),
                            carries, mask)

        def new_rows_and_finish(carries, heads=heads, new_step=new_step):
            carries = lax.fori_loop(0, i, lambda jn, cs: new_step(jn, cs, None), carries)
            carries = new_step(i, carries, diag_mask)
            for h, (m, l, acc) in zip(heads, carries):
                o = acc / l
                od = o[:, :tqp] - lam * o[:, tqp:]
                od = od * lax.rsqrt(jnp.mean(od * od, axis=0, keepdims=True) + RMS_EPS)
                od = od * g_ref[...] * (1.0 - lam_init)
                o_ref[:, h * LANE:(h + 1) * LANE] = od.T[:tq].astype(o_ref.dtype)

        if ncp:
            @pl.when(c == 0)
            def _(heads=heads):
                _carry_store(scs, heads, tuple(_softmax_init(n2) for _ in heads))

            carries = step_all(lambda h: pk_ref[:, h * LANE:(h + 1) * LANE].astype(BF16),
                               lambda h: functools.partial(_tn_dot, pv_ref[:, h * LANE:(h + 1) * LANE].astype(BF16)),
                               _carry_load(scs, heads), None)
            _carry_store(scs, heads, carries)
            pl.when(c == ncp - 1)(lambda heads=heads, fin=new_rows_and_finish: fin(_carry_load(scs, heads)))
        else:
            new_rows_and_finish(tuple(_softmax_init(n2) for _ in heads))


def _diff_attention(qx, yb, lam_params, subln, past, layer, b, t, lam_init):
    n = b * t
    plen = 0 if past is None else past[0].shape[2]
    tq, tqp, nq, wp, ncp = _attn_geometry(t, plen)
    vec = lambda a: a.reshape(1, -1).astype(F32)
    small = pl.BlockSpec((1, DIFF_QK_DIM), lambda bi, i, *_: (0, 0))
    in_specs = [small, small, small, small,
                pl.BlockSpec((DIFF_V_DIM, 1), lambda bi, i, *_: (0, 0)),
                pl.BlockSpec((tq, HEAD_W), lambda bi, i, *_: (bi * nq + i, 0)),
                pl.BlockSpec((t, HEAD_W), lambda bi, i, *_: (bi, 0)),
                pl.BlockSpec((t, HEAD_W), lambda bi, i, *_: (bi, 2))]
    args = [vec(p) for p in lam_params] + [subln.reshape(-1, 1).astype(F32), qx, yb, yb]
    grid, scratch = (b, nq), []
    if past is not None:
        chunk = pl.BlockSpec((None, None, wp, HEAD_W), lambda bi, i, c: (layer, bi, c, 0))
        in_specs += [chunk, chunk]
        args += list(past)
        grid = (b, nq, ncp)
        scratch = [pltpu.VMEM((N_DIFF_HEADS, 1, 2 * tqp), F32), pltpu.VMEM((N_DIFF_HEADS, 1, 2 * tqp), F32),
                   pltpu.VMEM((N_DIFF_HEADS, LANE, 2 * tqp), F32)]
    return pl.pallas_call(
        functools.partial(_diff_kernel, tq=tq, tqp=tqp, ncp=ncp, lam_init=lam_init),
        grid=grid,
        in_specs=in_specs,
        out_specs=pl.BlockSpec((tq, HEAD_W), lambda bi, i, *_: (bi * nq + i, 0)),
        out_shape=jax.ShapeDtypeStruct((n, HEAD_W), BF16),
        scratch_shapes=scratch,
        compiler_params=_cparams(("arbitrary",) * len(grid)),
        name="diff_attn",
    )(*args)


def _dsa_indexer(qi_ref, wq_ref, zk_ref, pki_ref, keyn_sc, biasn_sc, keyp_sc, biasp_sc, *, tq, tqp, wp, ncp, topk):
    i = pl.program_id(1)
    wn = tq
    nnew = i + 1
    kf = float(topk)

    lane = lax.broadcasted_iota(I32, (tqp, LANE), 1)
    wit = _pad_rows(wq_ref[...], tqp).T * (N_IDX_HEADS ** -0.5 * IDX_DIM ** -0.5)
    qqs, wcats = [], []
    for p in range(N_IDX_HEADS // 2):
        qp = _pad_rows(qi_ref[:, p * LANE:(p + 1) * LANE], tqp)
        zero = jnp.zeros_like(qp)
        qqs.append(jnp.concatenate([jnp.where(lane < IDX_DIM, qp, zero),
                                    jnp.where(lane >= IDX_DIM, qp, zero)], axis=0))
        wcats.append(jnp.concatenate([wit[2 * p:2 * p + 1], wit[2 * p + 1:2 * p + 2]], axis=1))

    def scores_t(kdup):
        acc = jnp.zeros((kdup.shape[0], tqp), F32)
        for qq, wc in zip(qqs, wcats):
            r = jnp.maximum(_nt_dot(kdup, qq), 0.0) * wc
            acc = acc + r[:, :tqp] + r[:, tqp:]
        return acc

    def to_key(s, visible):
        s = jnp.where(s == 0.0, 0.0, s)
        if visible is not None:
            s = jnp.where(visible, s, -jnp.inf)
        bits = lax.bitcast_convert_type(s, I32)
        return bits ^ (jnp.right_shift(bits, 31) & 0x7FFFFFFF)

    if ncp:
        def past_keys(c, _):
            r0 = pl.multiple_of(c * wp, wp)
            keyp_sc[c] = to_key(scores_t(pki_ref[pl.ds(r0, wp), :]), None)
            return 0
        lax.fori_loop(0, ncp, past_keys, 0)

    def new_keys(jn, _):
        r0 = pl.multiple_of(jn * wn, wn)
        s = scores_t(zk_ref[pl.ds(r0, wn), :])
        kpos = jn * wn + lax.broadcasted_iota(I32, (wn, tqp), 0)
        qpos = i * tq + lax.broadcasted_iota(I32, (wn, tqp), 1)
        keyn_sc[jn] = to_key(s, (kpos // CHUNK) <= (qpos // CHUNK))
        return 0
    lax.fori_loop(0, nnew, new_keys, 0)

    def count(pred):
        fr = COUNT_CHAINS * SUBLANE

        def fold(kk):
            m = jnp.where(pred(kk), 1.0, 0.0)
            return jnp.sum(m.reshape(kk.shape[0] // fr, fr, tqp), axis=0)
        acc = jnp.zeros((fr, tqp), F32)
        if ncp:
            acc = lax.fori_loop(0, ncp, lambda c, a: a + fold(keyp_sc[c]), acc)
        acc = lax.fori_loop(0, nnew, lambda jn, a: a + fold(keyn_sc[jn]), acc)
        return jnp.sum(acc, axis=0, keepdims=True)

    def pass_body(t, kth):
        cand = kth + jnp.left_shift(jnp.int32(1), 31 - t)
        cnt = count(lambda kk: kk >= cand)
        return jnp.where(cnt >= kf, cand, kth)
    kth = lax.fori_loop(0, 32, pass_body, jnp.full((1, tqp), INT_MIN, I32))

    need = kf - count(lambda kk: kk > kth)

    def lower_tri(w):
        ra = lax.broadcasted_iota(I32, (w, w), 0)
        rb = lax.broadcasted_iota(I32, (w, w), 1)
        return jnp.where(rb <= ra, 1.0, 0.0).astype(BF16)

    def bias_chunk(kk, seen, tril):
        eq = kk == kth
        eqf = jnp.where(eq, 1.0, 0.0)
        rank = seen + jnp.dot(tril, eqf.astype(BF16), preferred_element_type=F32)
        sel = ((kk > kth) | (eq & (rank <= need))) & (kk != KEY_NEG_INF)
        return jnp.where(sel, 0.0, NEG), seen + jnp.sum(eqf, axis=0, keepdims=True)

    seen = jnp.zeros((1, tqp), F32)
    if ncp:
        tril_p = lower_tri(wp)

        def past_bias(c, seen):
            bias, seen = bias_chunk(keyp_sc[c], seen, tril_p)
            biasp_sc[c] = bias
            return seen
        seen = lax.fori_loop(0, ncp, past_bias, seen)
    tril_n = lower_tri(wn)

    def new_bias(jn, seen):
        bias, seen = bias_chunk(keyn_sc[jn], seen, tril_n)
        biasn_sc[jn] = bias
        return seen
    lax.fori_loop(0, nnew, new_bias, seen)

def _dsa_kernel(*refs, tq, tqp, wp, ncp, topk):
    qs_ref, qi_ref, wq_ref, kn_ref, vn_ref, zk_ref = refs[:6]
    idx = functools.partial(_dsa_indexer, tq=tq, tqp=tqp, wp=wp, ncp=ncp, topk=topk)
    if ncp:
        pk_ref, pv_ref, pki_ref, o_ref, keyn_sc, biasn_sc, keyp_sc, biasp_sc = refs[6:14]
        scs = refs[14:]
        c = pl.program_id(2)
        pl.when(c == 0)(lambda: idx(qi_ref, wq_ref, zk_ref, pki_ref, keyn_sc, biasn_sc, keyp_sc, biasp_sc))
    else:
        o_ref, keyn_sc, biasn_sc = refs[6:]
        idx(qi_ref, wq_ref, zk_ref, None, keyn_sc, biasn_sc, None, None)
    wn = tq
    nnew = pl.program_id(1) + 1

    for h0 in range(0, N_DSA_HEADS, HEADS_PER_LOOP):
        heads = list(range(h0, h0 + HEADS_PER_LOOP))
        qs = [_pad_rows(qs_ref[:, h * LANE:(h + 1) * LANE], tqp) for h in heads]

        def step_all(get_k, pv_fn, bias, carries, heads=heads, qs=qs):
            sts = [_nt_dot(get_k(h), q) + bias for h, q in zip(heads, qs)]
            return _softmax_steps(sts, [pv_fn(h) for h in heads], carries)

        def new_body(jn, carries, step_all=step_all):
            r0 = pl.multiple_of(jn * wn, wn)
            return step_all(lambda h: kn_ref[pl.ds(r0, wn), h * LANE:(h + 1) * LANE],
                            lambda h: functools.partial(_tn_dot, vn_ref[pl.ds(r0, wn), h * LANE:(h + 1) * LANE]),
                            biasn_sc[jn], carries)

        def new_rows_and_finish(carries, heads=heads, new_body=new_body):
            carries = lax.fori_loop(0, nnew, new_body, carries)
            for h, (m, l, acc) in zip(heads, carries):
                o_ref[:, h * LANE:(h + 1) * LANE] = (acc / l).T[:tq].astype(o_ref.dtype)

        if ncp:
            @pl.when(c == 0)
            def _(heads=heads):
                _carry_store(scs, heads, tuple(_softmax_init(tqp) for _ in heads))

            carries = step_all(lambda h: pk_ref[:, h * LANE:(h + 1) * LANE].astype(BF16),
                               lambda h: functools.partial(_tn_dot, pv_ref[:, h * LANE:(h + 1) * LANE].astype(BF16)),
                               biasp_sc[c], _carry_load(scs, heads))
            _carry_store(scs, heads, carries)
            pl.when(c == ncp - 1)(lambda heads=heads, fin=new_rows_and_finish: fin(_carry_load(scs, heads)))
        else:
            new_rows_and_finish(tuple(_softmax_init(tqp) for _ in heads))


def _dsa_attention(qx, yb, zf, zb, past, layer, b, t):
    n = b * t
    plen = 0 if past is None else past[0].shape[2]
    topk = min(TOPK_MAX, (plen + t) // 4)
    tq, tqp, nq, wp, ncp = _attn_geometry(t, plen)
    in_specs = [
        pl.BlockSpec((tq, HEAD_W), lambda bi, i, *_: (bi * nq + i, 2)),
        pl.BlockSpec((tq, HEAD_W), lambda bi, i, *_: (bi * nq + i, 1)),
        pl.BlockSpec((tq, LANE), lambda bi, i, *_: (bi * nq + i, 1)),
        pl.BlockSpec((t, HEAD_W), lambda bi, i, *_: (bi, 1)),
        pl.BlockSpec((t, HEAD_W), lambda bi, i, *_: (bi, 3)),
        pl.BlockSpec((t, LANE), lambda bi, i, *_: (bi, 0)),
    ]
    args = [qx, qx, zf, yb, yb, zb]
    scratch = [pltpu.VMEM((nq, tq, tqp), I32), pltpu.VMEM((nq, tq, tqp), F32)]
    grid = (b, nq)
    if past is not None:
        chunk = pl.BlockSpec((None, None, wp, HEAD_W), lambda bi, i, c: (layer, bi, c, 0))
        in_specs += [chunk, chunk, pl.BlockSpec((None, None, plen, LANE), lambda bi, i, c: (layer, bi, 0, 0))]
        args += list(past)
        grid = (b, nq, ncp)
        scratch += [pltpu.VMEM((ncp, wp, tqp), I32), pltpu.VMEM((ncp, wp, tqp), F32),
                    pltpu.VMEM((N_DSA_HEADS, 1, tqp), F32), pltpu.VMEM((N_DSA_HEADS, 1, tqp), F32),
                    pltpu.VMEM((N_DSA_HEADS, LANE, tqp), F32)]
    return pl.pallas_call(
        functools.partial(_dsa_kernel, tq=tq, tqp=tqp, wp=wp, ncp=ncp, topk=topk),
        grid=grid,
        in_specs=in_specs,
        out_specs=pl.BlockSpec((tq, HEAD_W), lambda bi, i, *_: (bi * nq + i, 0)),
        out_shape=jax.ShapeDtypeStruct((n, HEAD_W), BF16),
        scratch_shapes=scratch,
        compiler_params=_cparams(("arbitrary",) * len(grid)),
        name="dsa_attn",
    )(*args)


def _merge_kernel(od_ref, os_ref, wa_ref, wb_ref, ga_ref, gb_ref, o_ref):
    sig = lambda z: 1.0 / (1.0 + jnp.exp(-z.astype(F32)))
    od, os_ = od_ref[...], os_ref[...]
    tn = o_ref.shape[1]
    sub = min(tn, PROJ_SUB)
    for s0 in range(0, tn, sub):
        cols = slice(s0, s0 + sub)
        a = jnp.dot(od, wa_ref[:, cols], preferred_element_type=F32)
        bq = jnp.dot(os_, wb_ref[:, cols], preferred_element_type=F32)
        o_ref[:, cols] = (sig(ga_ref[:, cols]) * a + sig(gb_ref[:, cols]) * bq).astype(o_ref.dtype)


def _merge(od, os_, wa, wb, qx, d):
    n = od.shape[0]
    tm = min(n, 1024)
    tn = min(d, 1024)
    gate0 = (3 * HEAD_W) // tn
    nd = d // tn
    return pl.pallas_call(
        _merge_kernel,
        grid=(n // tm, nd),
        in_specs=[
            pl.BlockSpec((tm, HEAD_W), lambda i, j: (i, 0)),
            pl.BlockSpec((tm, HEAD_W), lambda i, j: (i, 0)),
            pl.BlockSpec((HEAD_W, tn), lambda i, j: (0, j)),
            pl.BlockSpec((HEAD_W, tn), lambda i, j: (0, j)),
            pl.BlockSpec((tm, tn), lambda i, j: (i, gate0 + j)),
            pl.BlockSpec((tm, tn), lambda i, j: (i, gate0 + nd + j)),
        ],
        out_specs=pl.BlockSpec((tm, tn), lambda i, j: (i, j)),
        out_shape=jax.ShapeDtypeStruct((n, d), BF16),
        compiler_params=_cparams(("arbitrary", "arbitrary")),
        name="merge",
    )(od, os_, wa, wb, qx, qx)


def _outproj_kernel(m_ref, w_ref, x_ref, g1_ref, n2_ref, sc_ref, sh_ref, wr_ref, br_ref, x1_ref, hx_ref):
    d = x_ref.shape[-1]
    y = jnp.dot(m_ref[...], w_ref[...], preferred_element_type=F32)
    x1 = x_ref[...] + g1_ref[...] * y
    x1_ref[...] = x1
    h = x1 * lax.rsqrt(jnp.mean(x1 * x1, axis=-1, keepdims=True) + RMS_EPS) * n2_ref[...]
    h = h * (1.0 + sc_ref[...]) + sh_ref[...]
    hi = h.astype(BF16)
    lo = (h - hi.astype(F32)).astype(BF16)
    both = jnp.dot(hi, wr_ref[...], preferred_element_type=F32)
    lg = both[:, :LANE] + both[:, LANE:] + jnp.dot(lo, wr_ref[:, :LANE], preferred_element_type=F32)
    hx_ref[:, :d] = h
    hx_ref[:, d:] = lg + br_ref[...]


def _outproj(merged, w_out, x, mod5, layer, boff, n2, wr, br):
    b, t, d = x.shape
    tm = min(t, 512)
    nt = t // tm
    row = lambda k: pl.BlockSpec((None, None, None, 1, d), lambda bi, i, k=k: (layer, boff + bi, k, 0, 0))
    return pl.pallas_call(
        _outproj_kernel,
        grid=(b, nt),
        in_specs=[
            pl.BlockSpec((tm, d), lambda bi, i: (bi * nt + i, 0)),
            pl.BlockSpec((d, d), lambda bi, i: (0, 0)),
            pl.BlockSpec((None, tm, d), lambda bi, i: (bi, i, 0)),
            row(2),
            pl.BlockSpec((1, d), lambda bi, i: (0, 0)),
            row(4), row(3),
            pl.BlockSpec((d, 2 * LANE), lambda bi, i: (0, 0)),
            pl.BlockSpec((1, LANE), lambda bi, i: (0, 0)),
        ],
        out_specs=[
            pl.BlockSpec((None, tm, d), lambda bi, i: (bi, i, 0)),
            pl.BlockSpec((tm, d + LANE), lambda bi, i: (bi * nt + i, 0)),
        ],
        out_shape=[jax.ShapeDtypeStruct((b, t, d), F32),
                   jax.ShapeDtypeStruct((b * t, d + LANE), F32)],
        compiler_params=_cparams(("arbitrary", "arbitrary")),
        name="outproj",
    )(merged, w_out, x, mod5, n2.reshape(1, d), mod5, mod5, wr, br)


def _route_kernel(lg_ref, o_ref):
    lt = lg_ref[...].T
    tm = lt.shape[1]
    e = EXPERTS_PER_GROUP
    rid = lax.broadcasted_iota(I32, (e, tm), 0)
    gl = jnp.where(rid < N_GROUPS, lt[0:e], -jnp.inf)
    gmax = jnp.max(gl, axis=0, keepdims=True)
    ge = jnp.exp(gl - gmax)
    pg = ge / jnp.sum(ge, axis=0, keepdims=True)
    gidx = jnp.min(jnp.where(gl == gmax, rid, e), axis=0, keepdims=True)
    gw = jnp.where(rid == gidx, pg, 0.0)
    parts = []
    for g in range(N_GROUPS):
        el = lt[e * (g + 1):e * (g + 2)]
        m1 = jnp.max(el, axis=0, keepdims=True)
        i1 = jnp.min(jnp.where(el == m1, rid, e), axis=0, keepdims=True)
        el2 = jnp.where(rid == i1, -jnp.inf, el)
        m2 = jnp.max(el2, axis=0, keepdims=True)
        i2 = jnp.min(jnp.where(el2 == m2, rid, e), axis=0, keepdims=True)
        e2 = jnp.exp(m2 - m1)
        p1 = 1.0 / (1.0 + e2)
        p2 = e2 / (1.0 + e2)
        within = jnp.where(rid == i1, p1, 0.0) + jnp.where(rid == i2, p2, 0.0)
        parts.append(within * gw[g:g + 1])
    parts.append(jnp.broadcast_to(gidx.astype(F32), (e, tm)))
    parts.append(jnp.zeros((LANE - (N_GROUPS + 1) * e, tm), F32))
    o_ref[...] = jnp.concatenate(parts, axis=0).T


GROUP_ID_COL = N_GROUPS * EXPERTS_PER_GROUP


def _route(x, col_block):
    n = x.shape[0]
    tm = min(n, 512)
    return pl.pallas_call(
        _route_kernel,
        grid=(n // tm,),
        in_specs=[pl.BlockSpec((tm, LANE), lambda i: (i, col_block))],
        out_specs=pl.BlockSpec((tm, LANE), lambda i: (i, 0)),
        out_shape=jax.ShapeDtypeStruct((n, LANE), F32),
        compiler_params=_cparams(("arbitrary",)),
        name="route",
    )(x)


def _moe_plan(gid, ts):
    n = gid.shape[0]
    ntile = n // ts + N_GROUPS
    onehot = (gid[:, None] == jnp.arange(N_GROUPS, dtype=I32)[None, :]).astype(I32)
    csum = jnp.cumsum(onehot, axis=0)
    rank = jnp.sum((csum - onehot) * onehot, axis=1)
    tiles = (csum[-1] + ts - 1) // ts
    tile_end = jnp.cumsum(tiles)
    pos = jnp.sum(onehot * ((tile_end - tiles) * ts)[None, :], axis=1) + rank
    src = jnp.zeros((ntile * ts,), I32).at[pos].set(jnp.arange(n, dtype=I32))
    tile_gid = jnp.sum((jnp.arange(ntile, dtype=I32)[:, None] >= tile_end[None, :]).astype(I32), axis=1)
    return pos.astype(I32), src, jnp.minimum(tile_gid, N_GROUPS - 1).astype(I32)


def _row_copy(x_hbm, row, buf, r, sem):
    return pltpu.make_async_copy(x_hbm.at[pl.ds(row, 1)], buf.at[pl.ds(r, 1)], sem)


def _row_gather_pipelined(idx_ref, step, nsteps, x_hbm, bufs, sems):
    rows = bufs.shape[1]

    def start_all(s):
        slot = s % 2

        def start(k, carry):
            r0 = pl.multiple_of(k * SUBLANE, SUBLANE)
            for j in range(SUBLANE):
                _row_copy(x_hbm, idx_ref[s * rows + r0 + j], bufs.at[slot], r0 + j, sems.at[slot]).start()
            return carry
        lax.fori_loop(0, rows // SUBLANE, start, 0)

    pl.when(step == 0)(lambda: start_all(step))
    pl.when(step + 1 < nsteps)(lambda: start_all(step + 1))
    slot = step % 2

    def wait(k, carry):
        r0 = pl.multiple_of(k * SUBLANE, SUBLANE)
        for j in range(SUBLANE):
            _row_copy(x_hbm, 0, bufs.at[slot], r0 + j, sems.at[slot]).wait()
        return carry
    lax.fori_loop(0, rows // SUBLANE, wait, 0)
    return slot


def _dispatch_kernel(src_ref, x_hbm, h_ref, lg_ref, bufs, sems):
    slot = _row_gather_pipelined(src_ref, pl.program_id(0), pl.num_programs(0), x_hbm, bufs, sems)
    d = h_ref.shape[1]
    h_ref[...] = bufs[slot, :, :d].astype(h_ref.dtype)
    lg_ref[...] = bufs[slot, :, d:]


def _dispatch(hx, src, ts):
    npad = src.shape[0]
    d = hx.shape[1] - LANE
    return pl.pallas_call(
        _dispatch_kernel,
        grid_spec=pltpu.PrefetchScalarGridSpec(
            num_scalar_prefetch=1,
            grid=(npad // ts,),
            in_specs=[pl.BlockSpec(memory_space=pl.ANY)],
            out_specs=[pl.BlockSpec((ts, d), lambda i, s: (i, 0)),
                       pl.BlockSpec((ts, LANE), lambda i, s: (i, 0))],
            scratch_shapes=[pltpu.VMEM((2, ts, d + LANE), F32), pltpu.SemaphoreType.DMA((2,))],
        ),
        out_shape=[jax.ShapeDtypeStruct((npad, d), BF16), jax.ShapeDtypeStruct((npad, LANE), F32)],
        compiler_params=_cparams(("arbitrary",)),
        name="moe_dispatch",
    )(src, hx)


def _moe_up_kernel(gid_ref, h_ref, w1_ref, w3_ref, comb_ref, o_ref):
    col = gid_ref[pl.program_id(0)] * EXPERTS_PER_GROUP + pl.program_id(1)
    h = h_ref[...]
    a = jnp.dot(h, w1_ref[...], preferred_element_type=F32)
    u = jnp.dot(h, w3_ref[...], preferred_element_type=F32)
    comb = comb_ref[...]
    lane = lax.broadcasted_iota(I32, comb.shape, 1)
    cw = jnp.sum(jnp.where(lane == col, comb, 0.0), axis=1, keepdims=True)
    o_ref[...] = (a * (1.0 / (1.0 + jnp.exp(-a))) * u * cw).astype(o_ref.dtype)


def _moe_up(hs, w1, w3, comb, tile_gid, ts):
    npad, d = hs.shape
    f = w1.shape[2]
    e = EXPERTS_PER_GROUP
    return pl.pallas_call(
        _moe_up_kernel,
        grid_spec=pltpu.PrefetchScalarGridSpec(
            num_scalar_prefetch=1,
            grid=(npad // ts, e),
            in_specs=[
                pl.BlockSpec((ts, d), lambda i, j, g: (i, 0)),
                pl.BlockSpec((None, d, f), lambda i, j, g: (g[i] * e + j, 0, 0)),
                pl.BlockSpec((None, d, f), lambda i, j, g: (g[i] * e + j, 0, 0)),
                pl.BlockSpec((ts, LANE), lambda i, j, g: (i, 0)),
            ],
            out_specs=pl.BlockSpec((ts, f), lambda i, j, g: (i, j)),
        ),
        out_shape=jax.ShapeDtypeStruct((npad, e * f), BF16),
        compiler_params=_cparams(("arbitrary", "arbitrary")),
        name="moe_up",
    )(tile_gid, hs, w1, w3, comb)


def _moe_down_kernel(gid_ref, a_ref, w_ref, o_ref):
    o_ref[...] = jnp.dot(a_ref[...], w_ref[...], preferred_element_type=F32)


def _moe_down(act, w2, tile_gid, ts):
    npad, kk = act.shape
    d = w2.shape[2]
    return pl.pallas_call(
        _moe_down_kernel,
        grid_spec=pltpu.PrefetchScalarGridSpec(
            num_scalar_prefetch=1,
            grid=(npad // ts,),
            in_specs=[pl.BlockSpec((ts, kk), lambda i, g: (i, 0)),
                      pl.BlockSpec((None, kk, d), lambda i, g: (g[i], 0, 0))],
            out_specs=pl.BlockSpec((ts, d), lambda i, g: (i, 0)),
        ),
        out_shape=jax.ShapeDtypeStruct((npad, d), F32),
        compiler_params=_cparams(("arbitrary",)),
        name="moe_down",
    )(tile_gid, act, w2)


def _combine_kernel(pos_ref, ys_hbm, x_ref, g2_ref, gn_ref, sc_ref, sh_ref, x2_ref, hn_ref, bufs, sems, *, final):
    step = pl.program_id(0) * pl.num_programs(1) + pl.program_id(1)
    slot = _row_gather_pipelined(pos_ref, step, pl.num_programs(0) * pl.num_programs(1), ys_hbm, bufs, sems)
    x2 = x_ref[...] + g2_ref[...] * bufs[slot]
    x2_ref[...] = x2
    y = x2 * lax.rsqrt(jnp.mean(x2 * x2, axis=-1, keepdims=True) + RMS_EPS) * gn_ref[...]
    if not final:
        y = y * (1.0 + sc_ref[...]) + sh_ref[...]
    hn_ref[...] = y.astype(hn_ref.dtype)


def _combine(ys, pos, x1, mod5, layer, boff, g_next, next_layer, final):
    b, t, d = x1.shape
    tm = min(t, 256)
    nt = t // tm
    row = lambda l, k: pl.BlockSpec((None, None, None, 1, d), lambda bi, i, p, l=l, k=k: (l, boff + bi, k, 0, 0))
    return pl.pallas_call(
        functools.partial(_combine_kernel, final=final),
        grid_spec=pltpu.PrefetchScalarGridSpec(
            num_scalar_prefetch=1,
            grid=(b, nt),
            in_specs=[
                pl.BlockSpec(memory_space=pl.ANY),
                pl.BlockSpec((None, tm, d), lambda bi, i, p: (bi, i, 0)),
                row(layer, 5),
                pl.BlockSpec((1, d), lambda bi, i, p: (0, 0)),
                row(next_layer, 1), row(next_layer, 0),
            ],
            out_specs=[
                pl.BlockSpec((None, tm, d), lambda bi, i, p: (bi, i, 0)),
                pl.BlockSpec((None, tm, d), lambda bi, i, p: (bi, i, 0)),
            ],
            scratch_shapes=[pltpu.VMEM((2, tm, d), F32), pltpu.SemaphoreType.DMA((2,))],
        ),
        out_shape=[jax.ShapeDtypeStruct((b, t, d), F32),
                   jax.ShapeDtypeStruct((b, t, d), F32 if final else BF16)],
        compiler_params=_cparams(("arbitrary", "arbitrary")),
        name="moe_combine",
    )(pos, ys, x1, mod5, g_next.reshape(1, d), mod5, mod5)


def _pack_weights(w_in, w_br_diff, w_br_dsa, w_out, w_rg, b_rg, w_re, b_re, w_e1, w_e3, w_e2, d):
    hw = HEAD_W
    o_qd, o_kd, o_vd, o_qs, o_ks, o_vs, o_qi = (k * hw for k in range(7))
    o_ki = 7 * hw
    o_wi = o_ki + IDX_DIM
    o_gt = o_wi + N_IDX_HEADS
    sl = lambda o, n: w_in[:, :, o:o + n]
    wx = jnp.concatenate([sl(o_qd, hw), sl(o_qi, hw), sl(o_qs, hw), sl(o_gt, 2 * d)], axis=-1).astype(BF16)
    wy = jnp.concatenate([sl(o_kd, hw), sl(o_ks, hw), sl(o_vd, hw), sl(o_vs, hw)], axis=-1).astype(BF16)
    depth = w_in.shape[0]
    pad = jnp.zeros((depth, d, 2 * LANE - 2 * IDX_DIM - N_IDX_HEADS), w_in.dtype)
    wz = jnp.concatenate([sl(o_ki, IDX_DIM), sl(o_ki, IDX_DIM), sl(o_wi, N_IDX_HEADS), pad], axis=-1).astype(BF16)
    ne = N_GROUPS * EXPERTS_PER_GROUP
    rpad0 = jnp.zeros((depth, d, EXPERTS_PER_GROUP - N_GROUPS), F32)
    rpad1 = jnp.zeros((depth, d, LANE - EXPERTS_PER_GROUP - ne), F32)
    wr = jnp.concatenate([w_rg, rpad0, w_re, rpad1], axis=-1)
    wr_hi = wr.astype(BF16)
    wr_lo = (wr - wr_hi.astype(F32)).astype(BF16)
    br = jnp.concatenate([b_rg, jnp.zeros((depth, EXPERTS_PER_GROUP - N_GROUPS), F32), b_re,
                          jnp.zeros((depth, LANE - EXPERTS_PER_GROUP - ne), F32)], axis=-1).reshape(depth, 1, LANE)
    f = w_e1.shape[-1]
    return dict(
        wx=wx, wy=wy, wz=wz,
        wa=w_br_diff.astype(BF16), wb=w_br_dsa.astype(BF16), wo=w_out.astype(BF16),
        wr=jnp.concatenate([wr_hi, wr_lo], axis=-1), br=br,
        w1=w_e1.reshape(depth, ne, d, f).astype(BF16), w3=w_e3.reshape(depth, ne, d, f).astype(BF16),
        w2=w_e2.reshape(depth, N_GROUPS, EXPERTS_PER_GROUP * f, d).astype(BF16),
    )


def _trunk(x, pos, boff, mod5, past, pw, norm1, norm2, norm_f, lam_params, subln_g):
    b, t, d = x.shape
    n = b * t
    depth = norm1.shape[0]
    rows = n if t < 1024 else t

    def tables(head_dim, width, active, scale=1.0):
        tab = _rope_tables(pos, head_dim, width, active, scale)
        return jnp.tile(tab, (1, rows // t, 1)) if rows != t else tab

    log2e = math.log2(math.e)
    tabs = jnp.stack([tables(DIFF_QK_DIM, LANE, LANE), tables(DSA_HEAD_DIM, LANE, LANE),
                      tables(DIFF_QK_DIM, LANE, 0),
                      tables(DIFF_QK_DIM, LANE, LANE, DIFF_QK_DIM ** -0.5 * log2e),
                      tables(DSA_HEAD_DIM, LANE, LANE, DSA_HEAD_DIM ** -0.5 * log2e)])
    halves = (DIFF_QK_DIM // 8, DSA_HEAD_DIM // 8, DIFF_QK_DIM // 8, DIFF_QK_DIM // 8, DSA_HEAD_DIM // 8)
    R64, R128, PLAIN, R64_Q, R128_Q = range(5)
    tabz = tables(IDX_DIM, 2 * LANE, LANE)[None]
    gate_tiles = (2 * d) // HEAD_W
    ts = 512 if n >= 8192 else 128

    h = _normmod(x, norm1[0], mod5, 0, boff, 1, 0)
    new_rows = ([], [], [], [], [])
    y = None
    for l in range(depth):
        h2d = h.reshape(n, d)
        (qx,) = _proj(h2d, pw["wx"][l], tabs, halves, (R64_Q, R64, R128_Q) + (PLAIN,) * gate_tiles, HEAD_W,
                      (BF16,), "proj_q")
        yf, yb = _proj(h2d, pw["wy"][l], tabs, halves, (R64, R128, PLAIN, PLAIN), HEAD_W, (F32, BF16), "proj_kv")
        zf, zb = _proj(h2d, pw["wz"][l], tabz, (IDX_DIM // 8,), (0,), 2 * LANE, (F32, BF16), "proj_idx")
        for lst, r in zip(new_rows, (yf[:, 0:HEAD_W], yf[:, 2 * HEAD_W:3 * HEAD_W], yf[:, HEAD_W:2 * HEAD_W],
                                     yf[:, 3 * HEAD_W:4 * HEAD_W], zf[:, 0:IDX_DIM])):
            lst.append(r)
        lam_init = 0.8 - 0.6 * math.exp(-0.3 * l)
        lp = tuple(p[l] for p in lam_params)
        past_d = None if past is None else (past["dk"], past["dv"])
        past_s = None if past is None else (past["sk"], past["sv"], past["ik"])
        od = _diff_attention(qx, yb, lp, subln_g[l], past_d, l, b, t, lam_init)
        os_ = _dsa_attention(qx, yb, zf, zb, past_s, l, b, t)
        merged = _merge(od, os_, pw["wa"][l], pw["wb"][l], qx, d)
        x1, hx = _outproj(merged, pw["wo"][l], x, mod5, l, boff, norm2[l],
                          pw["wr"][l], pw["br"][l])
        gid = _route(hx, d // LANE)[:, GROUP_ID_COL].astype(I32)
        pos, src, tile_gid = _moe_plan(gid, ts)
        hs, lgs = _dispatch(hx, src, ts)
        act = _moe_up(hs, pw["w1"][l], pw["w3"][l], _route(lgs, 0), tile_gid, ts)
        ys = _moe_down(act, pw["w2"][l], tile_gid, ts)
        final = l == depth - 1
        g_next = norm_f if final else norm1[l + 1]
        x, h = _combine(ys, pos, x1, mod5, l, boff, g_next, 0 if final else l + 1, final)
        if final:
            y = h
    return y, new_rows


def kernel(x_prompt, x_sample, cache_diff_k, cache_diff_v, cache_dsa_k, cache_dsa_v, cache_idx_k, c_prompt, c_sample, norm1, norm2, norm_f, w_ada, b_ada, w_in, lambda_q1, lambda_k1, lambda_q2, lambda_k2, subln_g, w_br_diff, w_br_dsa, w_out, w_router_group, b_router_group, w_router_expert, b_router_expert, w_expert_gate, w_expert_up, w_expert_down):
    bp, tp, d = x_prompt.shape
    bs, ts, _ = x_sample.shape
    depth = norm1.shape[0]
    plen = cache_diff_k.shape[2]
    assert plen % CHUNK == 0 and tp % CHUNK == 0

    pw = _pack_weights(w_in, w_br_diff, w_br_dsa, w_out, w_router_group, b_router_group, w_router_expert,
                       b_router_expert, w_expert_gate, w_expert_up, w_expert_down, d)
    mod = _ada(jnp.concatenate([c_prompt, c_sample], axis=0), w_ada, b_ada)
    mod5 = mod.reshape(depth, bp + bs, 6, 1, d)
    lam_params = (lambda_q1, lambda_k1, lambda_q2, lambda_k2)

    pos_p = jnp.arange(tp, dtype=jnp.int32)
    y_p, rows_p = _trunk(x_prompt, pos_p, 0, mod5, None, pw, norm1, norm2, norm_f, lam_params, subln_g)

    rows2d = lambda c: c.reshape(depth, bs, plen, HEAD_W)
    past = dict(dk=rows2d(cache_diff_k), dv=rows2d(cache_diff_v), sk=rows2d(cache_dsa_k), sv=rows2d(cache_dsa_v),
                ik=jnp.concatenate([cache_idx_k, cache_idx_k], axis=-1).astype(BF16))
    pos_s = plen + jnp.arange(ts, dtype=jnp.int32)
    y_s, rows_s = _trunk(x_sample, pos_s, bp, mod5, past, pw, norm1, norm2, norm_f, lam_params, subln_g)

    def finish(rows, b, t):
        dk, dv, sk, sv, ik = (jnp.stack(r, axis=0) for r in rows)
        return (dk.reshape(depth, b, t, N_DIFF_HEADS, 2, DIFF_QK_DIM), dv.reshape(depth, b, t, N_DIFF_HEADS, DIFF_V_DIM),
                sk.reshape(depth, b, t, N_DSA_HEADS, DSA_HEAD_DIM), sv.reshape(depth, b, t, N_DSA_HEADS, DSA_HEAD_DIM),
                ik.reshape(depth, b, t, IDX_DIM))

    return (y_p, y_s) + finish(rows_p, bp, tp) + finish(rows_s, bs, ts)
```

```python
import functools
import math

import numpy as np
import jax
import jax.numpy as jnp
from jax import lax
from jax.experimental import pallas as pl
from jax.experimental.pallas import tpu as pltpu

F32 = jnp.float32
BF16 = jnp.bfloat16
I32 = jnp.int32

CHUNK = 64
ROPE_THETA = 500000.0
RMS_EPS = 1e-6
N_DIFF_HEADS = 8
DIFF_QK_DIM = 64
DIFF_V_DIM = 128
N_DSA_HEADS = 8
DSA_HEAD_DIM = 128
N_IDX_HEADS = 16
IDX_DIM = 64
TOPK_MAX = 256
N_GROUPS = 4
EXPERTS_PER_GROUP = 8
HEAD_W = 1024
LANE = 128
SUBLANE = 8
NEG = -1e30
VMEM_LIMIT = 56 * 1024 * 1024
HEADS_PER_LOOP = 8
PAST_CHUNK = 1024
COUNT_CHAINS = 4
PROJ_SUB = 256

_NEG_INF_BITS = int(np.array(-np.inf, np.float32).view(np.int32))
KEY_NEG_INF = int(np.int32(_NEG_INF_BITS ^ 0x7FFFFFFF))
INT_MIN = -(2 ** 31)


def _cparams(sem):
    return pltpu.CompilerParams(dimension_semantics=sem, vmem_limit_bytes=VMEM_LIMIT)


def _nt_dot(a, b):
    return lax.dot_general(a, b, (((1,), (1,)), ((), ())), preferred_element_type=F32)


def _ada_kernel(c_ref, w_ref, b_ref, o_ref):
    c = c_ref[...]
    a = (c * (1.0 / (1.0 + jnp.exp(-c)))).astype(BF16)
    o_ref[...] = jnp.dot(a, w_ref[...].astype(BF16), preferred_element_type=F32) + b_ref[...]


def _ada(c_all, w_ada, b_ada):
    depth, d, n6 = w_ada.shape
    r = c_all.shape[0]
    tn = 1024
    return pl.pallas_call(
        _ada_kernel,
        grid=(depth, n6 // tn),
        in_specs=[
            pl.BlockSpec((r, d), lambda l, j: (0, 0)),
            pl.BlockSpec((None, d, tn), lambda l, j: (l, 0, j)),
            pl.BlockSpec((None, 1, tn), lambda l, j: (l, 0, j)),
        ],
        out_specs=pl.BlockSpec((None, r, tn), lambda l, j: (l, 0, j)),
        out_shape=jax.ShapeDtypeStruct((depth, r, n6), F32),
        compiler_params=_cparams(("arbitrary", "arbitrary")),
        name="ada",
    )(c_all, w_ada, b_ada.reshape(depth, 1, n6))


def _normmod_kernel(x_ref, g_ref, sc_ref, sh_ref, o_ref):
    x = x_ref[...]
    y = x * lax.rsqrt(jnp.mean(x * x, axis=-1, keepdims=True) + RMS_EPS) * g_ref[...]
    o_ref[...] = (y * (1.0 + sc_ref[...]) + sh_ref[...]).astype(o_ref.dtype)


def _normmod(x, g, mod5, layer, boff, k_scale, k_shift):
    b, t, d = x.shape
    tt = min(t, 512)
    return pl.pallas_call(
        _normmod_kernel,
        grid=(b, t // tt),
        in_specs=[
            pl.BlockSpec((None, tt, d), lambda bi, i: (bi, i, 0)),
            pl.BlockSpec((1, d), lambda bi, i: (0, 0)),
            pl.BlockSpec((None, None, None, 1, d), lambda bi, i: (layer, boff + bi, k_scale, 0, 0)),
            pl.BlockSpec((None, None, None, 1, d), lambda bi, i: (layer, boff + bi, k_shift, 0, 0)),
        ],
        out_specs=pl.BlockSpec((None, tt, d), lambda bi, i: (bi, i, 0)),
        out_shape=jax.ShapeDtypeStruct((b, t, d), BF16),
        compiler_params=_cparams(("arbitrary", "arbitrary")),
        name="normmod",
    )(x, g.reshape(1, d), mod5, mod5)


def _rope_tables(pos, head_dim, width, active, scale=1.0):
    rot = head_dim // 4
    half = rot // 2
    inv = ROPE_THETA ** (-jnp.arange(half, dtype=F32) * (2.0 / rot))
    ang = pos.astype(F32)[:, None] * inv[None, :]
    cos, sin = jnp.cos(ang), jnp.sin(ang)
    col = np.arange(width)
    ch = col % head_dim
    first = (ch < half) & (col < active)
    second = (ch >= half) & (ch < rot) & (col < active)
    idx = np.where(ch < half, ch, np.where(ch < rot, ch - half, 0))
    c = jnp.where((first | second)[None, :], cos[:, idx], 1.0)
    s1 = jnp.where(first[None, :], -sin[:, idx], 0.0)
    s2 = jnp.where(second[None, :], sin[:, idx], 0.0)
    return (jnp.stack([c, s1, s2]) * scale).astype(F32)


def _proj_kernel(kind_ref, half_ref, h_ref, w_ref, tab_ref, *out_refs):
    half = half_ref[kind_ref[pl.program_id(1)]]
    h = h_ref[...]
    tn = w_ref.shape[1]
    tw = tab_ref.shape[2]
    sub = min(tn, PROJ_SUB)
    for s0 in range(0, tn, sub):
        acc = jnp.dot(h, w_ref[:, s0:s0 + sub], preferred_element_type=F32)
        for g0 in range(s0, s0 + sub, LANE):
            xg = acc[:, g0 - s0:g0 - s0 + LANE]
            t0 = g0 % tw
            c = tab_ref[0, :, t0:t0 + LANE]
            s1 = tab_ref[1, :, t0:t0 + LANE]
            s2 = tab_ref[2, :, t0:t0 + LANE]
            og = xg * c + pltpu.roll(xg, LANE - half, 1) * s1 + pltpu.roll(xg, half, 1) * s2
            for o in out_refs:
                o[:, g0:g0 + LANE] = og.astype(o.dtype)


def _proj(h2d, w, tabs, halves, kinds, tn, out_dtypes, name):
    n, d = h2d.shape
    c = w.shape[1]
    tm = min(n, 1024)
    assert c == tn * len(kinds)
    nblk = tabs.shape[2] // tm
    tw = tabs.shape[3]
    return pl.pallas_call(
        _proj_kernel,
        grid_spec=pltpu.PrefetchScalarGridSpec(
            num_scalar_prefetch=2,
            grid=(n // tm, c // tn),
            in_specs=[
                pl.BlockSpec((tm, d), lambda i, j, k, hf: (i, 0)),
                pl.BlockSpec((d, tn), lambda i, j, k, hf: (0, j)),
                pl.BlockSpec((None, 3, tm, tw), lambda i, j, k, hf: (k[j], 0, i % nblk, 0)),
            ],
            out_specs=[pl.BlockSpec((tm, tn), lambda i, j, k, hf: (i, j)) for _ in out_dtypes],
        ),
        out_shape=[jax.ShapeDtypeStruct((n, c), dt) for dt in out_dtypes],
        compiler_params=_cparams(("arbitrary", "arbitrary")),
        name=name,
    )(jnp.asarray(kinds, I32), jnp.asarray(halves, I32), h2d, w, tabs)


def _pad_rows(a, rows):
    if a.shape[0] == rows:
        return a
    return jnp.concatenate([a, jnp.zeros((rows - a.shape[0], a.shape[1]), a.dtype)], axis=0)


def _tn_dot(a, b):
    return lax.dot_general(a, b, (((0,), (0,)), ((), ())), preferred_element_type=F32)


def _softmax_steps(sts, pv_fns, carries):
    m_news = [jnp.maximum(c[0], jnp.max(st, axis=0, keepdims=True)) for st, c in zip(sts, carries)]
    ps = [jnp.exp2(st - m_new) for st, m_new in zip(sts, m_news)]
    pvs = [fn(p.astype(BF16)) for fn, p in zip(pv_fns, ps)]
    out = []
    for (m, l, acc), m_new, p, pv in zip(carries, m_news, ps, pvs):
        alpha = jnp.exp2(m - m_new)
        out.append((m_new, alpha * l + jnp.sum(p, axis=0, keepdims=True), alpha * acc + pv))
    return tuple(out)


def _softmax_init(n):
    return (jnp.full((1, n), NEG, F32), jnp.zeros((1, n), F32), jnp.zeros((LANE, n), F32))


def _attn_geometry(t, plen):
    tq = min(t, 256)
    tqp = max(tq, LANE)
    wp = min(plen, PAST_CHUNK) if plen else 0
    ncp = plen // wp if plen else 0
    assert t % tq == 0 and tq % CHUNK == 0 and (plen == 0 or plen % wp == 0)
    return tq, tqp, t // tq, wp, ncp


def _carry_load(scs, heads):
    m_sc, l_sc, acc_sc = scs
    return tuple((m_sc[h], l_sc[h], acc_sc[h]) for h in heads)


def _carry_store(scs, heads, carries):
    m_sc, l_sc, acc_sc = scs
    for h, (m, l, acc) in zip(heads, carries):
        m_sc[h] = m
        l_sc[h] = l
        acc_sc[h] = acc


def _diff_kernel(*refs, tq, tqp, ncp, lam_init):
    lq1, lk1, lq2, lk2, g_ref, q_ref, kn_ref, vn_ref = refs[:8]
    if ncp:
        pk_ref, pv_ref, o_ref = refs[8:11]
        scs = refs[11:]
        c = pl.program_id(2)
    else:
        (o_ref,) = refs[8:]
    i = pl.program_id(1)
    wn = tq
    n2 = 2 * tqp
    lane = lax.broadcasted_iota(I32, (tqp, LANE), 1)
    lam = (jnp.exp(jnp.sum(lq1[...] * lk1[...], axis=1, keepdims=True))
           - jnp.exp(jnp.sum(lq2[...] * lk2[...], axis=1, keepdims=True)) + lam_init)
    krow = lax.broadcasted_iota(I32, (wn, n2), 0)
    qcol = lax.broadcasted_iota(I32, (wn, n2), 1)
    qcol = jnp.where(qcol >= tqp, qcol - tqp, qcol)
    diag_mask = (krow // CHUNK) <= (qcol // CHUNK)

    for h0 in range(0, N_DIFF_HEADS, HEADS_PER_LOOP):
        heads = list(range(h0, h0 + HEADS_PER_LOOP))
        qqs = []
        for h in heads:
            q = _pad_rows(q_ref[:, h * LANE:(h + 1) * LANE], tqp)
            zero = jnp.zeros_like(q)
            qqs.append(jnp.concatenate([jnp.where(lane < DIFF_QK_DIM, q, zero),
                                        jnp.where(lane >= DIFF_QK_DIM, q, zero)], axis=0))

        def step_all(get_k, pv_fn, carries, mask, heads=heads, qqs=qqs):
            sts = [_nt_dot(get_k(h), qq) for h, qq in zip(heads, qqs)]
            if mask is not None:
                sts = [jnp.where(mask, st, NEG) for st in sts]
            return _softmax_steps(sts, [pv_fn(h) for h in heads], carries)

        def new_step(jn, carries, mask, step_all=step_all):
            r0 = pl.multiple_of(jn * wn, wn)
            return step_all(lambda h: kn_ref[pl.ds(r0, wn), h * LANE:(h + 1) * LANE],
                            lambda h: functools.partial(_tn_dot, vn_ref[pl.ds(r0, wn), h * LANE:(h + 1) * LANE]),
                            carries, mask)

        def new_rows_and_finish(carries, heads=heads, new_step=new_step):
            carries = lax.fori_loop(0, i, lambda jn, cs: new_step(jn, cs, None), carries)
            carries = new_step(i, carries, diag_mask)
            for h, (m, l, acc) in zip(heads, carries):
                o = acc / l
                od = o[:, :tqp] - lam * o[:, tqp:]
                od = od * lax.rsqrt(jnp.mean(od * od, axis=0, keepdims=True) + RMS_EPS)
                od = od * g_ref[...] * (1.0 - lam_init)
                o_ref[:, h * LANE:(h + 1) * LANE] = od.T[:tq].astype(o_ref.dtype)

        if ncp:
            @pl.when(c == 0)
            def _(heads=heads):
                _carry_store(scs, heads, tuple(_softmax_init(n2) for _ in heads))

            carries = step_all(lambda h: pk_ref[:, h * LANE:(h + 1) * LANE].astype(BF16),
                               lambda h: functools.partial(_tn_dot, pv_ref[:, h * LANE:(h + 1) * LANE].astype(BF16)),
                               _carry_load(scs, heads), None)
            _carry_store(scs, heads, carries)
            pl.when(c == ncp - 1)(lambda heads=heads, fin=new_rows_and_finish: fin(_carry_load(scs, heads)))
        else:
            new_rows_and_finish(tuple(_softmax_init(n2) for _ in heads))


def _diff_attention(qx, yb, lam_params, subln, past, layer, b, t, lam_init):
    n = b * t
    plen = 0 if past is None else past[0].shape[2]
    tq, tqp, nq, wp, ncp = _attn_geometry(t, plen)
    vec = lambda a: a.reshape(1, -1).astype(F32)
    small = pl.BlockSpec((1, DIFF_QK_DIM), lambda bi, i, *_: (0, 0))
    in_specs = [small, small, small, small,
                pl.BlockSpec((DIFF_V_DIM, 1), lambda bi, i, *_: (0, 0)),
                pl.BlockSpec((tq, HEAD_W), lambda bi, i, *_: (bi * nq + i, 0)),
                pl.BlockSpec((t, HEAD_W), lambda bi, i, *_: (bi, 0)),
                pl.BlockSpec((t, HEAD_W), lambda bi, i, *_: (bi, 2))]
    args = [vec(p) for p in lam_params] + [subln.reshape(-1, 1).astype(F32), qx, yb, yb]
    grid, scratch = (b, nq), []
    if past is not None:
        chunk = pl.BlockSpec((None, None, wp, HEAD_W), lambda bi, i, c: (layer, bi, c, 0))
        in_specs += [chunk, chunk]
        args += list(past)
        grid = (b, nq, ncp)
        scratch = [pltpu.VMEM((N_DIFF_HEADS, 1, 2 * tqp), F32), pltpu.VMEM((N_DIFF_HEADS, 1, 2 * tqp), F32),
                   pltpu.VMEM((N_DIFF_HEADS, LANE, 2 * tqp), F32)]
    return pl.pallas_call(
        functools.partial(_diff_kernel, tq=tq, tqp=tqp, ncp=ncp, lam_init=lam_init),
        grid=grid,
        in_specs=in_specs,
        out_specs=pl.BlockSpec((tq, HEAD_W), lambda bi, i, *_: (bi * nq + i, 0)),
        out_shape=jax.ShapeDtypeStruct((n, HEAD_W), BF16),
        scratch_shapes=scratch,
        compiler_params=_cparams(("arbitrary",) * len(grid)),
        name="diff_attn",
    )(*args)


def _dsa_indexer(qi_ref, wq_ref, zk_ref, pki_ref, keyn_sc, biasn_sc, keyp_sc, biasp_sc, *, tq, tqp, wp, ncp, topk):
    i = pl.program_id(1)
    wn = tq
    nnew = i + 1
    kf = float(topk)

    lane = lax.broadcasted_iota(I32, (tqp, LANE), 1)
    wit = _pad_rows(wq_ref[...], tqp).T * (N_IDX_HEADS ** -0.5 * IDX_DIM ** -0.5)
    qqs, wcats = [], []
    for p in range(N_IDX_HEADS // 2):
        qp = _pad_rows(qi_ref[:, p * LANE:(p + 1) * LANE], tqp)
        zero = jnp.zeros_like(qp)
        qqs.append(jnp.concatenate([jnp.where(lane < IDX_DIM, qp, zero),
                                    jnp.where(lane >= IDX_DIM, qp, zero)], axis=0))
        wcats.append(jnp.concatenate([wit[2 * p:2 * p + 1], wit[2 * p + 1:2 * p + 2]], axis=1))

    def scores_t(kdup):
        acc = jnp.zeros((kdup.shape[0], tqp), F32)
        for qq, wc in zip(qqs, wcats):
            r = jnp.maximum(_nt_dot(kdup, qq), 0.0) * wc
            acc = acc + r[:, :tqp] + r[:, tqp:]
        return acc

    def to_key(s, visible):
        s = jnp.where(s == 0.0, 0.0, s)
        if visible is not None:
            s = jnp.where(visible, s, -jnp.inf)
        bits = lax.bitcast_convert_type(s, I32)
        return bits ^ (jnp.right_shift(bits, 31) & 0x7FFFFFFF)

    if ncp:
        def past_keys(c, _):
            r0 = pl.multiple_of(c * wp, wp)
            keyp_sc[c] = to_key(scores_t(pki_ref[pl.ds(r0, wp), :]), None)
            return 0
        lax.fori_loop(0, ncp, past_keys, 0)

    def new_keys(jn, _):
        r0 = pl.multiple_of(jn * wn, wn)
        s = scores_t(zk_ref[pl.ds(r0, wn), :])
        kpos = jn * wn + lax.broadcasted_iota(I32, (wn, tqp), 0)
        qpos = i * tq + lax.broadcasted_iota(I32, (wn, tqp), 1)
        keyn_sc[jn] = to_key(s, (kpos // CHUNK) <= (qpos // CHUNK))
        return 0
    lax.fori_loop(0, nnew, new_keys, 0)

    def count(pred):
        fr = COUNT_CHAINS * SUBLANE

        def fold(kk):
            m = jnp.where(pred(kk), 1.0, 0.0)
            return jnp.sum(m.reshape(kk.shape[0] // fr, fr, tqp), axis=0)
        acc = jnp.zeros((fr, tqp), F32)
        if ncp:
            acc = lax.fori_loop(0, ncp, lambda c, a: a + fold(keyp_sc[c]), acc)
        acc = lax.fori_loop(0, nnew, lambda jn, a: a + fold(keyn_sc[jn]), acc)
        return jnp.sum(acc, axis=0, keepdims=True)

    def pass_body(t, kth):
        cand = kth + jnp.left_shift(jnp.int32(1), 31 - t)
        cnt = count(lambda kk: kk >= cand)
        return jnp.where(cnt >= kf, cand, kth)
    kth = lax.fori_loop(0, 32, pass_body, jnp.full((1, tqp), INT_MIN, I32))

    need = kf - count(lambda kk: kk > kth)

    def lower_tri(w):
        ra = lax.broadcasted_iota(I32, (w, w), 0)
        rb = lax.broadcasted_iota(I32, (w, w), 1)
        return jnp.where(rb <= ra, 1.0, 0.0).astype(BF16)

    def bias_chunk(kk, seen, tril):
        eq = kk == kth
        eqf = jnp.where(eq, 1.0, 0.0)
        rank = seen + jnp.dot(tril, eqf.astype(BF16), preferred_element_type=F32)
        sel = ((kk > kth) | (eq & (rank <= need))) & (kk != KEY_NEG_INF)
        return jnp.where(sel, 0.0, NEG), seen + jnp.sum(eqf, axis=0, keepdims=True)

    seen = jnp.zeros((1, tqp), F32)
    if ncp:
        tril_p = lower_tri(wp)

        def past_bias(c, seen):
            bias, seen = bias_chunk(keyp_sc[c], seen, tril_p)
            biasp_sc[c] = bias
            return seen
        seen = lax.fori_loop(0, ncp, past_bias, seen)
    tril_n = lower_tri(wn)

    def new_bias(jn, seen):
        bias, seen = bias_chunk(keyn_sc[jn], seen, tril_n)
        biasn_sc[jn] = bias
        return seen
    lax.fori_loop(0, nnew, new_bias, seen)

def _dsa_kernel(*refs, tq, tqp, wp, ncp, topk):
    qs_ref, qi_ref, wq_ref, kn_ref, vn_ref, zk_ref = refs[:6]
    idx = functools.partial(_dsa_indexer, tq=tq, tqp=tqp, wp=wp, ncp=ncp, topk=topk)
    if ncp:
        pk_ref, pv_ref, pki_ref, o_ref, keyn_sc, biasn_sc, keyp_sc, biasp_sc = refs[6:14]
        scs = refs[14:]
        c = pl.program_id(2)
        pl.when(c == 0)(lambda: idx(qi_ref, wq_ref, zk_ref, pki_ref, keyn_sc, biasn_sc, keyp_sc, biasp_sc))
    else:
        o_ref, keyn_sc, biasn_sc = refs[6:]
        idx(qi_ref, wq_ref, zk_ref, None, keyn_sc, biasn_sc, None, None)
    wn = tq
    nnew = pl.program_id(1) + 1

    for h0 in range(0, N_DSA_HEADS, HEADS_PER_LOOP):
        heads = list(range(h0, h0 + HEADS_PER_LOOP))
        qs = [_pad_rows(qs_ref[:, h * LANE:(h + 1) * LANE], tqp) for h in heads]

        def step_all(get_k, pv_fn, bias, carries, heads=heads, qs=qs):
            sts = [_nt_dot(get_k(h), q) + bias for h, q in zip(heads, qs)]
            return _softmax_steps(sts, [pv_fn(h) for h in heads], carries)

        def new_body(jn, carries, step_all=step_all):
            r0 = pl.multiple_of(jn * wn, wn)
            return step_all(lambda h: kn_ref[pl.ds(r0, wn), h * LANE:(h + 1) * LANE],
                            lambda h: functools.partial(_tn_dot, vn_ref[pl.ds(r0, wn), h * LANE:(h + 1) * LANE]),
                            biasn_sc[jn], carries)

        def new_rows_and_finish(carries, heads=heads, new_body=new_body):
            carries = lax.fori_loop(0, nnew, new_body, carries)
            for h, (m, l, acc) in zip(heads, carries):
                o_ref[:, h * LANE:(h + 1) * LANE] = (acc / l).T[:tq].astype(o_ref.dtype)

        if ncp:
            @pl.when(c == 0)
            def _(heads=heads):
                _carry_store(scs, heads, tuple(_softmax_init(tqp) for _ in heads))

            carries = step_all(lambda h: pk_ref[:, h * LANE:(h + 1) * LANE].astype(BF16),
                               lambda h: functools.partial(_tn_dot, pv_ref[:, h * LANE:(h + 1) * LANE].astype(BF16)),
                               biasp_sc[c], _carry_load(scs, heads))
            _carry_store(scs, heads, carries)
            pl.when(c == ncp - 1)(lambda heads=heads, fin=new_rows_and_finish: fin(_carry_load(scs, heads)))
        else:
            new_rows_and_finish(tuple(_softmax_init(tqp) for _ in heads))


def _dsa_attention(qx, yb, zf, zb, past, layer, b, t):
    n = b * t
    plen = 0 if past is None else past[0].shape[2]
    topk = min(TOPK_MAX, (plen + t) // 4)
    tq, tqp, nq, wp, ncp = _attn_geometry(t, plen)
    in_specs = [
        pl.BlockSpec((tq, HEAD_W), lambda bi, i, *_: (bi * nq + i, 2)),
        pl.BlockSpec((tq, HEAD_W), lambda bi, i, *_: (bi * nq + i, 1)),
        pl.BlockSpec((tq, LANE), lambda bi, i, *_: (bi * nq + i, 1)),
        pl.BlockSpec((t, HEAD_W), lambda bi, i, *_: (bi, 1)),
        pl.BlockSpec((t, HEAD_W), lambda bi, i, *_: (bi, 3)),
        pl.BlockSpec((t, LANE), lambda bi, i, *_: (bi, 0)),
    ]
    args = [qx, qx, zf, yb, yb, zb]
    scratch = [pltpu.VMEM((nq, tq, tqp), I32), pltpu.VMEM((nq, tq, tqp), F32)]
    grid = (b, nq)
    if past is not None:
        chunk = pl.BlockSpec((None, None, wp, HEAD_W), lambda bi, i, c: (layer, bi, c, 0))
        in_specs += [chunk, chunk, pl.BlockSpec((None, None, plen, LANE), lambda bi, i, c: (layer, bi, 0, 0))]
        args += list(past)
        grid = (b, nq, ncp)
        scratch += [pltpu.VMEM((ncp, wp, tqp), I32), pltpu.VMEM((ncp, wp, tqp), F32),
                    pltpu.VMEM((N_DSA_HEADS, 1, tqp), F32), pltpu.VMEM((N_DSA_HEADS, 1, tqp), F32),
                    pltpu.VMEM((N_DSA_HEADS, LANE, tqp), F32)]
    return pl.pallas_call(
        functools.partial(_dsa_kernel, tq=tq, tqp=tqp, wp=wp, ncp=ncp, topk=topk),
        grid=grid,
        in_specs=in_specs,
        out_specs=pl.BlockSpec((tq, HEAD_W), lambda bi, i, *_: (bi * nq + i, 0)),
        out_shape=jax.ShapeDtypeStruct((n, HEAD_W), BF16),
        scratch_shapes=scratch,
        compiler_params=_cparams(("arbitrary",) * len(grid)),
        name="dsa_attn",
    )(*args)


def _merge_kernel(od_ref, os_ref, wa_ref, wb_ref, ga_ref, gb_ref, o_ref):
    sig = lambda z: 1.0 / (1.0 + jnp.exp(-z.astype(F32)))
    od, os_ = od_ref[...], os_ref[...]
    tn = o_ref.shape[1]
    sub = min(tn, PROJ_SUB)
    for s0 in range(0, tn, sub):
        cols = slice(s0, s0 + sub)
        a = jnp.dot(od, wa_ref[:, cols], preferred_element_type=F32)
        bq = jnp.dot(os_, wb_ref[:, cols], preferred_element_type=F32)
        o_ref[:, cols] = (sig(ga_ref[:, cols]) * a + sig(gb_ref[:, cols]) * bq).astype(o_ref.dtype)


def _merge(od, os_, wa, wb, qx, d):
    n = od.shape[0]
    tm = min(n, 1024)
    tn = min(d, 1024)
    gate0 = (3 * HEAD_W) // tn
    nd = d // tn
    return pl.pallas_call(
        _merge_kernel,
        grid=(n // tm, nd),
        in_specs=[
            pl.BlockSpec((tm, HEAD_W), lambda i, j: (i, 0)),
            pl.BlockSpec((tm, HEAD_W), lambda i, j: (i, 0)),
            pl.BlockSpec((HEAD_W, tn), lambda i, j: (0, j)),
            pl.BlockSpec((HEAD_W, tn), lambda i, j: (0, j)),
            pl.BlockSpec((tm, tn), lambda i, j: (i, gate0 + j)),
            pl.BlockSpec((tm, tn), lambda i, j: (i, gate0 + nd + j)),
        ],
        out_specs=pl.BlockSpec((tm, tn), lambda i, j: (i, j)),
        out_shape=jax.ShapeDtypeStruct((n, d), BF16),
        compiler_params=_cparams(("arbitrary", "arbitrary")),
        name="merge",
    )(od, os_, wa, wb, qx, qx)


def _outproj_kernel(m_ref, w_ref, x_ref, g1_ref, n2_ref, sc_ref, sh_ref, wr_ref, br_ref, x1_ref, hx_ref):
    d = x_ref.shape[-1]
    y = jnp.dot(m_ref[...], w_ref[...], preferred_element_type=F32)
    x1 = x_ref[...] + g1_ref[...] * y
    x1_ref[...] = x1
    h = x1 * lax.rsqrt(jnp.mean(x1 * x1, axis=-1, keepdims=True) + RMS_EPS) * n2_ref[...]
    h = h * (1.0 + sc_ref[...]) + sh_ref[...]
    hi = h.astype(BF16)
    lo = (h - hi.astype(F32)).astype(BF16)
    both = jnp.dot(hi, wr_ref[...], preferred_element_type=F32)
    lg = both[:, :LANE] + both[:, LANE:] + jnp.dot(lo, wr_ref[:, :LANE], preferred_element_type=F32)
    hx_ref[:, :d] = h
    hx_ref[:, d:] = lg + br_ref[...]


def _outproj(merged, w_out, x, mod5, layer, boff, n2, wr, br):
    b, t, d = x.shape
    tm = min(t, 512)
    nt = t // tm
    row = lambda k: pl.BlockSpec((None, None, None, 1, d), lambda bi, i, k=k: (layer, boff + bi, k, 0, 0))
    return pl.pallas_call(
        _outproj_kernel,
        grid=(b, nt),
        in_specs=[
            pl.BlockSpec((tm, d), lambda bi, i: (bi * nt + i, 0)),
            pl.BlockSpec((d, d), lambda bi, i: (0, 0)),
            pl.BlockSpec((None, tm, d), lambda bi, i: (bi, i, 0)),
            row(2),
            pl.BlockSpec((1, d), lambda bi, i: (0, 0)),
            row(4), row(3),
            pl.BlockSpec((d, 2 * LANE), lambda bi, i: (0, 0)),
            pl.BlockSpec((1, LANE), lambda bi, i: (0, 0)),
        ],
        out_specs=[
            pl.BlockSpec((None, tm, d), lambda bi, i: (bi, i, 0)),
            pl.BlockSpec((tm, d + LANE), lambda bi, i: (bi * nt + i, 0)),
        ],
        out_shape=[jax.ShapeDtypeStruct((b, t, d), F32),
                   jax.ShapeDtypeStruct((b * t, d + LANE), F32)],
        compiler_params=_cparams(("arbitrary", "arbitrary")),
        name="outproj",
    )(merged, w_out, x, mod5, n2.reshape(1, d), mod5, mod5, wr, br)


def _route_kernel(lg_ref, o_ref):
    lt = lg_ref[...].T
    tm = lt.shape[1]
    e = EXPERTS_PER_GROUP
    rid = lax.broadcasted_iota(I32, (e, tm), 0)
    gl = jnp.where(rid < N_GROUPS, lt[0:e], -jnp.inf)
    gmax = jnp.max(gl, axis=0, keepdims=True)
    ge = jnp.exp(gl - gmax)
    pg = ge / jnp.sum(ge, axis=0, keepdims=True)
    gidx = jnp.min(jnp.where(gl == gmax, rid, e), axis=0, keepdims=True)
    gw = jnp.where(rid == gidx, pg, 0.0)
    parts = []
    for g in range(N_GROUPS):
        el = lt[e * (g + 1):e * (g + 2)]
        m1 = jnp.max(el, axis=0, keepdims=True)
        i1 = jnp.min(jnp.where(el == m1, rid, e), axis=0, keepdims=True)
        el2 = jnp.where(rid == i1, -jnp.inf, el)
        m2 = jnp.max(el2, axis=0, keepdims=True)
        i2 = jnp.min(jnp.where(el2 == m2, rid, e), axis=0, keepdims=True)
        e2 = jnp.exp(m2 - m1)
        p1 = 1.0 / (1.0 + e2)
        p2 = e2 / (1.0 + e2)
        within = jnp.where(rid == i1, p1, 0.0) + jnp.where(rid == i2, p2, 0.0)
        parts.append(within * gw[g:g + 1])
    parts.append(jnp.broadcast_to(gidx.astype(F32), (e, tm)))
    parts.append(jnp.zeros((LANE - (N_GROUPS + 1) * e, tm), F32))
    o_ref[...] = jnp.concatenate(parts, axis=0).T


GROUP_ID_COL = N_GROUPS * EXPERTS_PER_GROUP


def _route(x, col_block):
    n = x.shape[0]
    tm = min(n, 512)
    return pl.pallas_call(
        _route_kernel,
        grid=(n // tm,),
        in_specs=[pl.BlockSpec((tm, LANE), lambda i: (i, col_block))],
        out_specs=pl.BlockSpec((tm, LANE), lambda i: (i, 0)),
        out_shape=jax.ShapeDtypeStruct((n, LANE), F32),
        compiler_params=_cparams(("arbitrary",)),
        name="route",
    )(x)


def _moe_plan(gid, ts):
    n = gid.shape[0]
    ntile = n // ts + N_GROUPS
    onehot = (gid[:, None] == jnp.arange(N_GROUPS, dtype=I32)[None, :]).astype(I32)
    csum = jnp.cumsum(onehot, axis=0)
    rank = jnp.sum((csum - onehot) * onehot, axis=1)
    tiles = (csum[-1] + ts - 1) // ts
    tile_end = jnp.cumsum(tiles)
    pos = jnp.sum(onehot * ((tile_end - tiles) * ts)[None, :], axis=1) + rank
    src = jnp.zeros((ntile * ts,), I32).at[pos].set(jnp.arange(n, dtype=I32))
    tile_gid = jnp.sum((jnp.arange(ntile, dtype=I32)[:, None] >= tile_end[None, :]).astype(I32), axis=1)
    return pos.astype(I32), src, jnp.minimum(tile_gid, N_GROUPS - 1).astype(I32)


def _row_copy(x_hbm, row, buf, r, sem):
    return pltpu.make_async_copy(x_hbm.at[pl.ds(row, 1)], buf.at[pl.ds(r, 1)], sem)


def _row_gather_pipelined(idx_ref, step, nsteps, x_hbm, bufs, sems):
    rows = bufs.shape[1]

    def start_all(s):
        slot = s % 2

        def start(k, carry):
            r0 = pl.multiple_of(k * SUBLANE, SUBLANE)
            for j in range(SUBLANE):
                _row_copy(x_hbm, idx_ref[s * rows + r0 + j], bufs.at[slot], r0 + j,
                          sems.at[slot]).start(priority=j % 2)
            return carry
        lax.fori_loop(0, rows // SUBLANE, start, 0)

    pl.when(step == 0)(lambda: start_all(step))
    pl.when(step + 1 < nsteps)(lambda: start_all(step + 1))
    slot = step % 2

    def wait(k, carry):
        r0 = pl.multiple_of(k * SUBLANE, SUBLANE)
        for j in range(SUBLANE):
            _row_copy(x_hbm, 0, bufs.at[slot], r0 + j, sems.at[slot]).wait()
        return carry
    lax.fori_loop(0, rows // SUBLANE, wait, 0)
    return slot


def _dispatch_kernel(src_ref, x_hbm, h_ref, lg_ref, bufs, sems):
    slot = _row_gather_pipelined(src_ref, pl.program_id(0), pl.num_programs(0), x_hbm, bufs, sems)
    d = h_ref.shape[1]
    h_ref[...] = bufs[slot, :, :d].astype(h_ref.dtype)
    lg_ref[...] = bufs[slot, :, d:]


def _dispatch(hx, src, ts):
    npad = src.shape[0]
    d = hx.shape[1] - LANE
    return pl.pallas_call(
        _dispatch_kernel,
        grid_spec=pltpu.PrefetchScalarGridSpec(
            num_scalar_prefetch=1,
            grid=(npad // ts,),
            in_specs=[pl.BlockSpec(memory_space=pl.ANY)],
            out_specs=[pl.BlockSpec((ts, d), lambda i, s: (i, 0)),
                       pl.BlockSpec((ts, LANE), lambda i, s: (i, 0))],
            scratch_shapes=[pltpu.VMEM((2, ts, d + LANE), F32), pltpu.SemaphoreType.DMA((2,))],
        ),
        out_shape=[jax.ShapeDtypeStruct((npad, d), BF16), jax.ShapeDtypeStruct((npad, LANE), F32)],
        compiler_params=_cparams(("arbitrary",)),
        name="moe_dispatch",
    )(src, hx)


def _moe_up_kernel(gid_ref, h_ref, w1_ref, w3_ref, comb_ref, o_ref):
    col = gid_ref[pl.program_id(0)] * EXPERTS_PER_GROUP + pl.program_id(1)
    h = h_ref[...]
    a = jnp.dot(h, w1_ref[...], preferred_element_type=F32)
    u = jnp.dot(h, w3_ref[...], preferred_element_type=F32)
    comb = comb_ref[...]
    lane = lax.broadcasted_iota(I32, comb.shape, 1)
    cw = jnp.sum(jnp.where(lane == col, comb, 0.0), axis=1, keepdims=True)
    o_ref[...] = (a * (1.0 / (1.0 + jnp.exp(-a))) * u * cw).astype(o_ref.dtype)


def _moe_up(hs, w1, w3, comb, tile_gid, ts):
    npad, d = hs.shape
    f = w1.shape[2]
    e = EXPERTS_PER_GROUP
    return pl.pallas_call(
        _moe_up_kernel,
        grid_spec=pltpu.PrefetchScalarGridSpec(
            num_scalar_prefetch=1,
            grid=(npad // ts, e),
            in_specs=[
                pl.BlockSpec((ts, d), lambda i, j, g: (i, 0)),
                pl.BlockSpec((None, d, f), lambda i, j, g: (g[i] * e + j, 0, 0)),
                pl.BlockSpec((None, d, f), lambda i, j, g: (g[i] * e + j, 0, 0)),
                pl.BlockSpec((ts, LANE), lambda i, j, g: (i, 0)),
            ],
            out_specs=pl.BlockSpec((ts, f), lambda i, j, g: (i, j)),
        ),
        out_shape=jax.ShapeDtypeStruct((npad, e * f), BF16),
        compiler_params=_cparams(("arbitrary", "arbitrary")),
        name="moe_up",
    )(tile_gid, hs, w1, w3, comb)


def _moe_down_kernel(gid_ref, a_ref, w_ref, o_ref):
    o_ref[...] = jnp.dot(a_ref[...], w_ref[...], preferred_element_type=F32)


def _moe_down(act, w2, tile_gid, ts):
    npad, kk = act.shape
    d = w2.shape[2]
    return pl.pallas_call(
        _moe_down_kernel,
        grid_spec=pltpu.PrefetchScalarGridSpec(
            num_scalar_prefetch=1,
            grid=(npad // ts,),
            in_specs=[pl.BlockSpec((ts, kk), lambda i, g: (i, 0)),
                      pl.BlockSpec((None, kk, d), lambda i, g: (g[i], 0, 0))],
            out_specs=pl.BlockSpec((ts, d), lambda i, g: (i, 0)),
        ),
        out_shape=jax.ShapeDtypeStruct((npad, d), F32),
        compiler_params=_cparams(("arbitrary",)),
        name="moe_down",
    )(tile_gid, act, w2)


def _combine_kernel(pos_ref, ys_hbm, x_ref, g2_ref, gn_ref, sc_ref, sh_ref, x2_ref, hn_ref, bufs, sems, *, final):
    step = pl.program_id(0) * pl.num_programs(1) + pl.program_id(1)
    slot = _row_gather_pipelined(pos_ref, step, pl.num_programs(0) * pl.num_programs(1), ys_hbm, bufs, sems)
    x2 = x_ref[...] + g2_ref[...] * bufs[slot]
    x2_ref[...] = x2
    y = x2 * lax.rsqrt(jnp.mean(x2 * x2, axis=-1, keepdims=True) + RMS_EPS) * gn_ref[...]
    if not final:
        y = y * (1.0 + sc_ref[...]) + sh_ref[...]
    hn_ref[...] = y.astype(hn_ref.dtype)


def _combine(ys, pos, x1, mod5, layer, boff, g_next, next_layer, final):
    b, t, d = x1.shape
    tm = min(t, 256)
    nt = t // tm
    row = lambda l, k: pl.BlockSpec((None, None, None, 1, d), lambda bi, i, p, l=l, k=k: (l, boff + bi, k, 0, 0))
    return pl.pallas_call(
        functools.partial(_combine_kernel, final=final),
        grid_spec=pltpu.PrefetchScalarGridSpec(
            num_scalar_prefetch=1,
            grid=(b, nt),
            in_specs=[
                pl.BlockSpec(memory_space=pl.ANY),
                pl.BlockSpec((None, tm, d), lambda bi, i, p: (bi, i, 0)),
                row(layer, 5),
                pl.BlockSpec((1, d), lambda bi, i, p: (0, 0)),
                row(next_layer, 1), row(next_layer, 0),
            ],
            out_specs=[
                pl.BlockSpec((None, tm, d), lambda bi, i, p: (bi, i, 0)),
                pl.BlockSpec((None, tm, d), lambda bi, i, p: (bi, i, 0)),
            ],
            scratch_shapes=[pltpu.VMEM((2, tm, d), F32), pltpu.SemaphoreType.DMA((2,))],
        ),
        out_shape=[jax.ShapeDtypeStruct((b, t, d), F32),
                   jax.ShapeDtypeStruct((b, t, d), F32 if final else BF16)],
        compiler_params=_cparams(("arbitrary", "arbitrary")),
        name="moe_combine",
    )(pos, ys, x1, mod5, g_next.reshape(1, d), mod5, mod5)


def _pack_weights(w_in, w_br_diff, w_br_dsa, w_out, w_rg, b_rg, w_re, b_re, w_e1, w_e3, w_e2, d):
    hw = HEAD_W
    o_qd, o_kd, o_vd, o_qs, o_ks, o_vs, o_qi = (k * hw for k in range(7))
    o_ki = 7 * hw
    o_wi = o_ki + IDX_DIM
    o_gt = o_wi + N_IDX_HEADS
    sl = lambda o, n: w_in[:, :, o:o + n]
    wx = jnp.concatenate([sl(o_qd, hw), sl(o_qi, hw), sl(o_qs, hw), sl(o_gt, 2 * d)], axis=-1).astype(BF16)
    wy = jnp.concatenate([sl(o_kd, hw), sl(o_ks, hw), sl(o_vd, hw), sl(o_vs, hw)], axis=-1).astype(BF16)
    depth = w_in.shape[0]
    pad = jnp.zeros((depth, d, 2 * LANE - 2 * IDX_DIM - N_IDX_HEADS), w_in.dtype)
    wz = jnp.concatenate([sl(o_ki, IDX_DIM), sl(o_ki, IDX_DIM), sl(o_wi, N_IDX_HEADS), pad], axis=-1).astype(BF16)
    ne = N_GROUPS * EXPERTS_PER_GROUP
    rpad0 = jnp.zeros((depth, d, EXPERTS_PER_GROUP - N_GROUPS), F32)
    rpad1 = jnp.zeros((depth, d, LANE - EXPERTS_PER_GROUP - ne), F32)
    wr = jnp.concatenate([w_rg, rpad0, w_re, rpad1], axis=-1)
    wr_hi = wr.astype(BF16)
    wr_lo = (wr - wr_hi.astype(F32)).astype(BF16)
    br = jnp.concatenate([b_rg, jnp.zeros((depth, EXPERTS_PER_GROUP - N_GROUPS), F32), b_re,
                          jnp.zeros((depth, LANE - EXPERTS_PER_GROUP - ne), F32)], axis=-1).reshape(depth, 1, LANE)
    f = w_e1.shape[-1]
    return dict(
        wx=wx, wy=wy, wz=wz,
        wa=w_br_diff.astype(BF16), wb=w_br_dsa.astype(BF16), wo=w_out.astype(BF16),
        wr=jnp.concatenate([wr_hi, wr_lo], axis=-1), br=br,
        w1=w_e1.reshape(depth, ne, d, f).astype(BF16), w3=w_e3.reshape(depth, ne, d, f).astype(BF16),
        w2=w_e2.reshape(depth, N_GROUPS, EXPERTS_PER_GROUP * f, d).astype(BF16),
    )


def _trunk(x, pos, boff, mod5, past, pw, norm1, norm2, norm_f, lam_params, subln_g):
    b, t, d = x.shape
    n = b * t
    depth = norm1.shape[0]
    rows = n if t < 1024 else t

    def tables(head_dim, width, active, scale=1.0):
        tab = _rope_tables(pos, head_dim, width, active, scale)
        return jnp.tile(tab, (1, rows // t, 1)) if rows != t else tab

    log2e = math.log2(math.e)
    tabs = jnp.stack([tables(DIFF_QK_DIM, LANE, LANE), tables(DSA_HEAD_DIM, LANE, LANE),
                      tables(DIFF_QK_DIM, LANE, 0),
                      tables(DIFF_QK_DIM, LANE, LANE, DIFF_QK_DIM ** -0.5 * log2e),
                      tables(DSA_HEAD_DIM, LANE, LANE, DSA_HEAD_DIM ** -0.5 * log2e)])
    halves = (DIFF_QK_DIM // 8, DSA_HEAD_DIM // 8, DIFF_QK_DIM // 8, DIFF_QK_DIM // 8, DSA_HEAD_DIM // 8)
    R64, R128, PLAIN, R64_Q, R128_Q = range(5)
    tabz = tables(IDX_DIM, 2 * LANE, LANE)[None]
    gate_tiles = (2 * d) // HEAD_W
    ts = 512 if n >= 8192 else 128

    h = _normmod(x, norm1[0], mod5, 0, boff, 1, 0)
    new_rows = ([], [], [], [], [])
    y = None
    for l in range(depth):
        h2d = h.reshape(n, d)
        (qx,) = _proj(h2d, pw["wx"][l], tabs, halves, (R64_Q, R64, R128_Q) + (PLAIN,) * gate_tiles, HEAD_W,
                      (BF16,), "proj_q")
        yf, yb = _proj(h2d, pw["wy"][l], tabs, halves, (R64, R128, PLAIN, PLAIN), HEAD_W, (F32, BF16), "proj_kv")
        zf, zb = _proj(h2d, pw["wz"][l], tabz, (IDX_DIM // 8,), (0,), 2 * LANE, (F32, BF16), "proj_idx")
        for lst, r in zip(new_rows, (yf[:, 0:HEAD_W], yf[:, 2 * HEAD_W:3 * HEAD_W], yf[:, HEAD_W:2 * HEAD_W],
                                     yf[:, 3 * HEAD_W:4 * HEAD_W], zf[:, 0:IDX_DIM])):
            lst.append(r)
        lam_init = 0.8 - 0.6 * math.exp(-0.3 * l)
        lp = tuple(p[l] for p in lam_params)
        past_d = None if past is None else (past["dk"], past["dv"])
        past_s = None if past is None else (past["sk"], past["sv"], past["ik"])
        od = _diff_attention(qx, yb, lp, subln_g[l], past_d, l, b, t, lam_init)
        os_ = _dsa_attention(qx, yb, zf, zb, past_s, l, b, t)
        merged = _merge(od, os_, pw["wa"][l], pw["wb"][l], qx, d)
        x1, hx = _outproj(merged, pw["wo"][l], x, mod5, l, boff, norm2[l],
                          pw["wr"][l], pw["br"][l])
        gid = _route(hx, d // LANE)[:, GROUP_ID_COL].astype(I32)
        pos, src, tile_gid = _moe_plan(gid, ts)
        hs, lgs = _dispatch(hx, src, ts)
        act = _moe_up(hs, pw["w1"][l], pw["w3"][l], _route(lgs, 0), tile_gid, ts)
        ys = _moe_down(act, pw["w2"][l], tile_gid, ts)
        final = l == depth - 1
        g_next = norm_f if final else norm1[l + 1]
        x, h = _combine(ys, pos, x1, mod5, l, boff, g_next, 0 if final else l + 1, final)
        if final:
            y = h
    return y, new_rows


def kernel(x_prompt, x_sample, cache_diff_k, cache_diff_v, cache_dsa_k, cache_dsa_v, cache_idx_k, c_prompt, c_sample, norm1, norm2, norm_f, w_ada, b_ada, w_in, lambda_q1, lambda_k1, lambda_q2, lambda_k2, subln_g, w_br_diff, w_br_dsa, w_out, w_router_group, b_router_group, w_router_expert, b_router_expert, w_expert_gate, w_expert_up, w_expert_down):
    bp, tp, d = x_prompt.shape
    bs, ts, _ = x_sample.shape
    depth = norm1.shape[0]
    plen = cache_diff_k.shape[2]
    assert plen % CHUNK == 0 and tp % CHUNK == 0

    pw = _pack_weights(w_in, w_br_diff, w_br_dsa, w_out, w_router_group, b_router_group, w_router_expert,
                       b_router_expert, w_expert_gate, w_expert_up, w_expert_down, d)
    mod = _ada(jnp.concatenate([c_prompt, c_sample], axis=0), w_ada, b_ada)
    mod5 = mod.reshape(depth, bp + bs, 6, 1, d)
    lam_params = (lambda_q1, lambda_k1, lambda_q2, lambda_k2)

    pos_p = jnp.arange(tp, dtype=jnp.int32)
    y_p, rows_p = _trunk(x_prompt, pos_p, 0, mod5, None, pw, norm1, norm2, norm_f, lam_params, subln_g)

    rows2d = lambda c: c.reshape(depth, bs, plen, HEAD_W)
    past = dict(dk=rows2d(cache_diff_k), dv=rows2d(cache_diff_v), sk=rows2d(cache_dsa_k), sv=rows2d(cache_dsa_v),
                ik=jnp.concatenate([cache_idx_k, cache_idx_k], axis=-1).astype(BF16))
    pos_s = plen + jnp.arange(ts, dtype=jnp.int32)
    y_s, rows_s = _trunk(x_sample, pos_s, bp, mod5, past, pw, norm1, norm2, norm_f, lam_params, subln_g)

    def finish(rows, b, t):
        dk, dv, sk, sv, ik = (jnp.stack(r, axis=0) for r in rows)
        return (dk.reshape(depth, b, t, N_DIFF_HEADS, 2, DIFF_QK_DIM), dv.reshape(depth, b, t, N_DIFF_HEADS, DIFF_V_DIM),
                sk.reshape(depth, b, t, N_DSA_HEADS, DSA_HEAD_DIM), sv.reshape(depth, b, t, N_DSA_HEADS, DSA_HEAD_DIM),
                ik.reshape(depth, b, t, IDX_DIM))

    return (y_p, y_s) + finish(rows_p, bp, tp) + finish(rows_s, bs, ts)
```
